```python
import math
import jax
import jax.numpy as jnp
from jax import lax
import numpy as np

D_MODEL = 1024
BATCH = 8
SEQ = 2048
DEPTH = 4
DEC_BATCH = 128
DEC_SEQ = 8
PAST_LEN = 16384
PAGE_SIZE = 128

HEAD_DIM = 64
N_HEADS_GROUP = 4
GROUP_WIDTH = N_HEADS_GROUP * HEAD_DIM
N_MIXERS = 4
D_MIX = N_MIXERS * GROUP_WIDTH
SSM_STATE = 128
SSM_GROUPS = 2
CONV_WIDTH = 4
CHUNK = 64
ROPE_BASE = 10000.0
FFN_DENSE = 2816
N_EXPERTS = 8
TOP_K = 2
FFN_EXPERT = 3584
PLE_DIM = 256
NORM_EPS = 1e-6
N_DENSE = (DEPTH + 1) // 2
N_MOE = DEPTH // 2
RET_COLS = 4 * GROUP_WIDTH
MLSTM_COLS = 4 * GROUP_WIDTH + 2 * N_HEADS_GROUP
GDN_COLS = 4 * GROUP_WIDTH + 2 * N_HEADS_GROUP
SSM_BC = 2 * SSM_GROUPS * SSM_STATE
GDN_CONV_CH = 3 * GROUP_WIDTH
SSM_CONV_CH = GROUP_WIDTH + SSM_BC
SSM_COLS = GROUP_WIDTH + SSM_CONV_CH + N_HEADS_GROUP
D_IN = RET_COLS + MLSTM_COLS + GDN_COLS + SSM_COLS

kernel_name = "hybrid_ret_mlstm_gdn_ssd_decoder_step"


def _rmsnorm(x, g):
    x = x.astype(jnp.float32)
    return x * lax.rsqrt(jnp.mean(x * x, -1, keepdims=True) + NORM_EPS) * g.astype(jnp.float32)


def _group_rmsnorm(x, g):
    y = x * lax.rsqrt(jnp.mean(x * x, -1, keepdims=True) + NORM_EPS)
    return y.reshape(x.shape[:-2] + (-1,)) * g.astype(jnp.float32)


def _group_layernorm(x, g):
    xc = x - jnp.mean(x, -1, keepdims=True)
    y = xc * lax.rsqrt(jnp.mean(xc * xc, -1, keepdims=True) + NORM_EPS)
    return y.reshape(x.shape[:-2] + (-1,)) * g.astype(jnp.float32)


def _l2norm(x):
    return x * lax.rsqrt(jnp.sum(x * x, -1, keepdims=True) + NORM_EPS)


def _heads(t, n_heads):
    b, s = t.shape[:2]
    return t.reshape(b, s, n_heads, -1).transpose(0, 2, 1, 3)


def _rope(x, pos):
    half = x.shape[-1] // 2
    inv_freq = ROPE_BASE ** (-jnp.arange(half, dtype=jnp.float32) / half)
    ang = pos[:, None] * inv_freq[None, :]
    cos, sin = jnp.cos(ang), jnp.sin(ang)
    x1, x2 = x[..., :half], x[..., half:]
    return jnp.concatenate([x1 * cos - x2 * sin, x1 * sin + x2 * cos], axis=-1)


def _chunk_len(t):
    return t if t <= CHUNK else CHUNK


def _to_chunks(x, L):
    b, h, t = x.shape[:3]
    return jnp.moveaxis(x.reshape((b, h, t // L, L) + x.shape[3:]), 2, 0)


def _from_chunks(x):
    x = jnp.moveaxis(x, 0, 2)
    return x.reshape((x.shape[0], x.shape[1], -1) + x.shape[4:])


def _causal_conv(x, buf, w, bias):
    xp = jnp.concatenate([buf, x], axis=1)
    y = lax.conv_general_dilated(xp, w.astype(jnp.float32)[:, None, :], (1,), 'VALID',
                                 dimension_numbers=('NWC', 'WIO', 'NWC'),
                                 feature_group_count=x.shape[-1])
    if bias is not None:
        y = y + bias.astype(jnp.float32)
    return y, xp[:, -(CONV_WIDTH - 1):]


def _decay_linear_attn(q, k, v, log_a, s0):
    L = _chunk_len(q.shape[2])
    causal = jnp.tril(jnp.ones((L, L), dtype=bool))
    xs = tuple(_to_chunks(z, L) for z in (q, k, v, log_a))

    def step(S, inp):
        qc, kc, vc, ac = inp
        b = jnp.cumsum(ac, axis=-1)
        decay = jnp.exp(jnp.where(causal, b[..., :, None] - b[..., None, :], -jnp.inf))
        scores = jnp.einsum('bhlk,bhsk->bhls', qc, kc) * decay
        o = (jnp.einsum('bhls,bhsv->bhlv', scores, vc)
             + jnp.einsum('bhlk,bhkv->bhlv', qc * jnp.exp(b)[..., None], S))
        k_end = kc * jnp.exp(b[..., -1:] - b)[..., None]
        S = S * jnp.exp(b[..., -1])[..., None, None] + jnp.einsum('bhsk,bhsv->bhkv', k_end, vc)
        return S, o

    S, o = lax.scan(step, s0, xs)
    return _from_chunks(o), S


def _mlstm(q, k, v, i_pre, log_f, c0, n0, m0):
    L = _chunk_len(q.shape[2])
    causal = jnp.tril(jnp.ones((L, L), dtype=bool))
    xs = tuple(_to_chunks(z, L) for z in (q, k, v, i_pre, log_f))

    def step(carry, inp):
        c, n, m = carry
        qc, kc, vc, ic, fc = inp
        b = jnp.cumsum(fc, axis=-1)
        d = jnp.where(causal, b[..., :, None] - b[..., None, :] + ic[..., None, :], -jnp.inf)
        inter = b + m[..., None]
        m_row = jnp.maximum(jnp.max(d, axis=-1), inter)
        w_intra = jnp.exp(d - m_row[..., None])
        w_state = jnp.exp(inter - m_row)
        s = jnp.einsum('bhlk,bhsk->bhls', qc, kc) * w_intra
        num = (jnp.einsum('bhls,bhsv->bhlv', s, vc)
               + w_state[..., None] * jnp.einsum('bhlk,bhkv->bhlv', qc, c))
        den = jnp.sum(s, axis=-1) + w_state * jnp.einsum('bhlk,bhk->bhl', qc, n)
        h = num / jnp.maximum(jnp.abs(den), jnp.exp(-m_row))[..., None]
        g_end = b[..., -1:] - b + ic
        m_new = jnp.maximum(b[..., -1] + m, jnp.max(g_end, axis=-1))
        w_c = jnp.exp(b[..., -1] + m - m_new)
        kw = kc * jnp.exp(g_end - m_new[..., None])[..., None]
        c = w_c[..., None, None] * c + jnp.einsum('bhsk,bhsv->bhkv', kw, vc)
        n = w_c[..., None] * n + jnp.sum(kw, axis=2)
        return (c, n, m_new), h

    (c, n, m), h = lax.scan(step, (c0, n0, m0), xs)
    return _from_chunks(h), c, n, m


def _gated_delta(q, k, v, g, beta, s0):
    L = _chunk_len(q.shape[2])
    V = v.shape[-1]
    causal = jnp.tril(jnp.ones((L, L), dtype=bool))
    strict = jnp.tril(jnp.ones((L, L), dtype=bool), -1)
    eye = jnp.eye(L, dtype=jnp.float32)
    xs = tuple(_to_chunks(z, L) for z in (q, k, v, g, beta))

    def step(S, inp):
        qc, kc, vc, gc, bc = inp
        b = jnp.cumsum(gc, axis=-1)
        decay = jnp.exp(jnp.where(causal, b[..., :, None] - b[..., None, :], -jnp.inf))
        kb = kc * bc[..., None]
        a_low = jnp.where(strict, jnp.einsum('bhlk,bhsk->bhls', kb, kc) * decay, 0.0)
        rhs = jnp.concatenate([vc * bc[..., None], kb * jnp.exp(b)[..., None]], axis=-1)
        sol = lax.linalg.triangular_solve(eye + a_low, rhs, left_side=True, lower=True,
                                          unit_diagonal=True)
        u, w = sol[..., :V], sol[..., V:]
        v_new = u - jnp.einsum('bhlk,bhkv->bhlv', w, S)
        attn = jnp.einsum('bhlk,bhsk->bhls', qc, kc) * decay
        o = (jnp.einsum('bhlk,bhkv->bhlv', qc * jnp.exp(b)[..., None], S)
             + jnp.einsum('bhls,bhsv->bhlv', attn, v_new))
        k_end = kc * jnp.exp(b[..., -1:] - b)[..., None]
        S = S * jnp.exp(b[..., -1])[..., None, None] + jnp.einsum('bhsk,bhsv->bhkv', k_end, v_new)
        return S, o

    S, o = lax.scan(step, s0, xs)
    return _from_chunks(o), S


def _mixer_layer(a, pos, states, w_in, w_out, ret_norm, mlstm_i_bias, mlstm_f_bias, mlstm_norm,
                 gdn_conv_w, gdn_a_log, gdn_dt_bias, gdn_norm,
                 ssm_conv_w, ssm_conv_b, ssm_dt_bias, ssm_a_log, ssm_d, ssm_norm):
    f32 = jnp.float32
    H, Dh, W = N_HEADS_GROUP, HEAD_DIM, GROUP_WIDTH
    bsz, t = a.shape[:2]
    s_ret, s_mc, s_mn, s_mm, s_gdn, s_gconv, s_ssm, s_sconv = (s.astype(f32) for s in states)
    proj = a @ w_in.astype(f32)
    ret_p, ml_p, gdn_p, ssm_p = jnp.split(
        proj, [RET_COLS, RET_COLS + MLSTM_COLS, RET_COLS + MLSTM_COLS + GDN_COLS], axis=-1)

    rq, rk, rv, rg = jnp.split(ret_p, 4, axis=-1)
    rq = _rope(_heads(rq, H), pos)
    rk = _rope(_heads(rk, H), pos) * Dh ** -0.5
    log_gamma = jnp.log1p(-jnp.exp2(-5.0 - jnp.arange(H, dtype=f32)))
    ro, n_ret = _decay_linear_attn(rq, rk, _heads(rv, H),
                                   jnp.broadcast_to(log_gamma[None, :, None], (bsz, H, t)), s_ret)
    ret_out = jax.nn.silu(rg) * _group_layernorm(ro.transpose(0, 2, 1, 3), ret_norm)

    mq, mk, mv, mo, mi, mf = jnp.split(ml_p, [W, 2 * W, 3 * W, 4 * W, 4 * W + H], axis=-1)
    i_pre = (mi + mlstm_i_bias).transpose(0, 2, 1)
    log_f = jax.nn.log_sigmoid(mf + mlstm_f_bias).transpose(0, 2, 1)
    mh, n_mc, n_mn, n_mm = _mlstm(_heads(mq, H), _heads(mk, H) * Dh ** -0.5, _heads(mv, H),
                                  i_pre, log_f, s_mc, s_mn, s_mm)
    mlstm_out = jax.nn.sigmoid(mo) * _group_layernorm(mh.transpose(0, 2, 1, 3), mlstm_norm)

    g_qkv, g_z, g_b, g_a = jnp.split(gdn_p, [3 * W, 4 * W, 4 * W + H], axis=-1)
    g_qkv, n_gconv = _causal_conv(g_qkv, s_gconv, gdn_conv_w, None)
    gq, gk, gv = jnp.split(jax.nn.silu(g_qkv), 3, axis=-1)
    beta = jax.nn.sigmoid(g_b).transpose(0, 2, 1)
    g_log = (-jnp.exp(gdn_a_log) * jax.nn.softplus(g_a + gdn_dt_bias)).transpose(0, 2, 1)
    go, n_gdn = _gated_delta(_l2norm(_heads(gq, H)) * Dh ** -0.5, _l2norm(_heads(gk, H)),
                             _heads(gv, H), g_log, beta, s_gdn)
    gdn_out = _group_rmsnorm(go.transpose(0, 2, 1, 3), gdn_norm) * jax.nn.silu(g_z)

    s_z, s_xbc, s_dt = jnp.split(ssm_p, [W, W + SSM_CONV_CH], axis=-1)
    s_xbc, n_sconv = _causal_conv(s_xbc, s_sconv, ssm_conv_w, ssm_conv_b)
    sx, sb, sc = jnp.split(jax.nn.silu(s_xbc), [W, W + SSM_GROUPS * SSM_STATE], axis=-1)
    dt = jax.nn.softplus(s_dt + ssm_dt_bias).transpose(0, 2, 1)
    rep = H // SSM_GROUPS
    sb_h = jnp.repeat(_heads(sb, SSM_GROUPS), rep, axis=1)
    sc_h = jnp.repeat(_heads(sc, SSM_GROUPS), rep, axis=1)
    xh = _heads(sx, H)
    a_dt = dt * (-jnp.exp(ssm_a_log.astype(f32)))[None, :, None]
    so, n_ssm = _decay_linear_attn(sc_h, sb_h, xh * dt[..., None], a_dt, s_ssm)
    y = (so + ssm_d.astype(f32)[None, :, None, None] * xh).transpose(0, 2, 1, 3)
    y = y.reshape(bsz, t, W) * jax.nn.silu(s_z)
    ssm_out = _group_rmsnorm(y.reshape(bsz, t, SSM_GROUPS, -1), ssm_norm)

    mixed = jnp.concatenate([ret_out, mlstm_out, gdn_out, ssm_out], axis=-1) @ w_out.astype(f32)
    return mixed, (n_ret, n_mc, n_mn, n_mm, n_gdn, n_gconv, n_ssm, n_sconv)


def _swiglu(x, w_gate, w_up, w_down):
    return (jax.nn.silu(x @ w_gate.astype(jnp.float32)) * (x @ w_up.astype(jnp.float32))) @ w_down.astype(jnp.float32)


def _moe(x, router, w_gate, w_up, w_down):
    logits = x @ router.astype(jnp.float32)
    top_val, top_idx = lax.top_k(logits, TOP_K)
    probs = jax.nn.softmax(top_val, axis=-1)
    weight = jnp.einsum('btk,btke->bte', probs,
                        jax.nn.one_hot(top_idx, N_EXPERTS, dtype=jnp.float32))
    out = jnp.zeros_like(x)
    for e in range(N_EXPERTS):
        out = out + weight[..., e:e + 1] * _swiglu(x, w_gate[e], w_up[e], w_down[e])
    return out


def _trunk(x, p, pos, states, weights):
    (norm_mix, w_in, w_out, ret_norm, mlstm_i_bias, mlstm_f_bias, mlstm_norm,
     gdn_conv_w, gdn_a_log, gdn_dt_bias, gdn_norm,
     ssm_conv_w, ssm_conv_b, ssm_dt_bias, ssm_a_log, ssm_d, ssm_norm,
     norm_ffn, ffn_w_gate, ffn_w_up, ffn_w_down,
     moe_router, moe_w_gate, moe_w_up, moe_w_down,
     ple_w_proj, ple_norm, ple_w_gate, norm_final) = weights
    f32 = jnp.float32
    h = x.astype(f32)
    new_states = [[] for _ in states]
    for i in range(DEPTH):
        mixed, layer_new = _mixer_layer(
            _rmsnorm(h, norm_mix[i]), pos, tuple(s[i] for s in states), w_in[i], w_out[i],
            ret_norm[i], mlstm_i_bias[i], mlstm_f_bias[i], mlstm_norm[i],
            gdn_conv_w[i], gdn_a_log[i], gdn_dt_bias[i], gdn_norm[i],
            ssm_conv_w[i], ssm_conv_b[i], ssm_dt_bias[i], ssm_a_log[i], ssm_d[i], ssm_norm[i])
        h = h + mixed
        c = _rmsnorm(h, norm_ffn[i])
        j = i // 2
        if i % 2 == 0:
            h = h + _swiglu(c, ffn_w_gate[j], ffn_w_up[j], ffn_w_down[j])
        else:
            h = h + _moe(c, moe_router[j], moe_w_gate[j], moe_w_up[j], moe_w_down[j])
        gate = jax.nn.sigmoid(_rmsnorm(h, ple_norm[i]) @ ple_w_gate[i].astype(f32))
        h = h + gate * (p[i].astype(f32) @ ple_w_proj[i].astype(f32))
        for acc, s in zip(new_states, layer_new):
            acc.append(s)
    y = _rmsnorm(h, norm_final).astype(x.dtype)
    return y, tuple(jnp.stack(acc) for acc in new_states)


def setup_inputs(seed: int = 0) -> dict:
    key = jax.random.key(seed)
    keys = iter(jax.random.split(key, 64))
    H = N_HEADS_GROUP

    def normal(shape, scale):
        return jax.random.normal(next(keys), shape, jnp.float32) * scale

    def gain(shape):
        return 1.0 + normal(shape, 0.02)

    def dt_bias():
        dt = jnp.exp(jax.random.uniform(next(keys), (DEPTH, H), jnp.float32,
                                        math.log(1e-3), math.log(1e-1)))
        return dt + jnp.log(-jnp.expm1(-dt))

    def a_log():
        return jnp.log(jax.random.uniform(next(keys), (DEPTH, H), jnp.float32, 1.0, 16.0))

    return {
        'x_prompt': normal((BATCH, SEQ, D_MODEL), 1.0),
        'x_sample': normal((DEC_BATCH, DEC_SEQ, D_MODEL), 1.0),
        'state_ret': normal((DEPTH, DEC_BATCH, H, HEAD_DIM, HEAD_DIM), 0.3),
        'state_mlstm_c': normal((DEPTH, DEC_BATCH, H, HEAD_DIM, HEAD_DIM), 0.3),
        'state_mlstm_n': normal((DEPTH, DEC_BATCH, H, HEAD_DIM), 0.3),
        'state_mlstm_m': normal((DEPTH, DEC_BATCH, H), 0.5),
        'state_gdn': normal((DEPTH, DEC_BATCH, H, HEAD_DIM, HEAD_DIM), 0.1),
        'state_gdn_conv': normal((DEPTH, DEC_BATCH, CONV_WIDTH - 1, GDN_CONV_CH), 1.0),
        'state_ssm': normal((DEPTH, DEC_BATCH, H, SSM_STATE, HEAD_DIM), 0.3),
        'state_ssm_conv': normal((DEPTH, DEC_BATCH, CONV_WIDTH - 1, SSM_CONV_CH), 1.0),
        'p_prompt': normal((DEPTH, BATCH, SEQ, PLE_DIM), 1.0),
        'p_sample': normal((DEPTH, DEC_BATCH, DEC_SEQ, PLE_DIM), 1.0),
        'norm_mix': gain((DEPTH, D_MODEL)),
        'w_in': normal((DEPTH, D_MODEL, D_IN), D_MODEL ** -0.5),
        'w_out': normal((DEPTH, D_MIX, D_MODEL), D_MIX ** -0.5),
        'ret_norm': gain((DEPTH, GROUP_WIDTH)),
        'mlstm_i_bias': normal((DEPTH, H), 0.1),
        'mlstm_f_bias': jnp.linspace(3.0, 6.0, H, dtype=jnp.float32)[None, :] + normal((DEPTH, H), 0.1),
        'mlstm_norm': gain((DEPTH, GROUP_WIDTH)),
        'gdn_conv_w': normal((DEPTH, CONV_WIDTH, GDN_CONV_CH), 0.5),
        'gdn_a_log': a_log(),
        'gdn_dt_bias': dt_bias(),
        'gdn_norm': gain((DEPTH, GROUP_WIDTH)),
        'ssm_conv_w': normal((DEPTH, CONV_WIDTH, SSM_CONV_CH), 0.5),
        'ssm_conv_b': normal((DEPTH, SSM_CONV_CH), 0.02),
        'ssm_dt_bias': dt_bias(),
        'ssm_a_log': a_log(),
        'ssm_d': gain((DEPTH, H)),
        'ssm_norm': gain((DEPTH, GROUP_WIDTH)),
        'norm_ffn': gain((DEPTH, D_MODEL)),
        'ffn_w_gate': normal((N_DENSE, D_MODEL, FFN_DENSE), D_MODEL ** -0.5),
        'ffn_w_up': normal((N_DENSE, D_MODEL, FFN_DENSE), D_MODEL ** -0.5),
        'ffn_w_down': normal((N_DENSE, FFN_DENSE, D_MODEL), FFN_DENSE ** -0.5),
        'moe_router': normal((N_MOE, D_MODEL, N_EXPERTS), D_MODEL ** -0.5),
        'moe_w_gate': normal((N_MOE, N_EXPERTS, D_MODEL, FFN_EXPERT), D_MODEL ** -0.5),
        'moe_w_up': normal((N_MOE, N_EXPERTS, D_MODEL, FFN_EXPERT), D_MODEL ** -0.5),
        'moe_w_down': normal((N_MOE, N_EXPERTS, FFN_EXPERT, D_MODEL), FFN_EXPERT ** -0.5),
        'ple_w_proj': normal((DEPTH, PLE_DIM, D_MODEL), PLE_DIM ** -0.5),
        'ple_norm': gain((DEPTH, D_MODEL)),
        'ple_w_gate': normal((DEPTH, D_MODEL, D_MODEL), D_MODEL ** -0.5),
        'norm_final': gain((D_MODEL,)),
    }


def reference(x_prompt, x_sample, state_ret, state_mlstm_c, state_mlstm_n, state_mlstm_m,
              state_gdn, state_gdn_conv, state_ssm, state_ssm_conv, p_prompt, p_sample,
              norm_mix, w_in, w_out, ret_norm, mlstm_i_bias, mlstm_f_bias, mlstm_norm,
              gdn_conv_w, gdn_a_log, gdn_dt_bias, gdn_norm,
              ssm_conv_w, ssm_conv_b, ssm_dt_bias, ssm_a_log, ssm_d, ssm_norm,
              norm_ffn, ffn_w_gate, ffn_w_up, ffn_w_down,
              moe_router, moe_w_gate, moe_w_up, moe_w_down,
              ple_w_proj, ple_norm, ple_w_gate, norm_final):
    f32 = jnp.float32
    H, Dh = N_HEADS_GROUP, HEAD_DIM
    weights = (norm_mix, w_in, w_out, ret_norm, mlstm_i_bias, mlstm_f_bias, mlstm_norm,
               gdn_conv_w, gdn_a_log, gdn_dt_bias, gdn_norm,
               ssm_conv_w, ssm_conv_b, ssm_dt_bias, ssm_a_log, ssm_d, ssm_norm,
               norm_ffn, ffn_w_gate, ffn_w_up, ffn_w_down,
               moe_router, moe_w_gate, moe_w_up, moe_w_down,
               ple_w_proj, ple_norm, ple_w_gate, norm_final)
    bp, tp = x_prompt.shape[:2]
    zero_states = (
        jnp.zeros((DEPTH, bp, H, Dh, Dh), f32),
        jnp.zeros((DEPTH, bp, H, Dh, Dh), f32),
        jnp.zeros((DEPTH, bp, H, Dh), f32),
        jnp.zeros((DEPTH, bp, H), f32),
        jnp.zeros((DEPTH, bp, H, Dh, Dh), f32),
        jnp.zeros((DEPTH, bp, CONV_WIDTH - 1, GDN_CONV_CH), f32),
        jnp.zeros((DEPTH, bp, H, SSM_STATE, Dh), f32),
        jnp.zeros((DEPTH, bp, CONV_WIDTH - 1, SSM_CONV_CH), f32),
    )
    pos_prompt = jnp.arange(tp, dtype=f32)
    pos_sample = PAST_LEN + jnp.arange(x_sample.shape[1], dtype=f32)
    y_prompt, new_prompt = _trunk(x_prompt, p_prompt, pos_prompt, zero_states, weights)
    sample_states = (state_ret, state_mlstm_c, state_mlstm_n, state_mlstm_m,
                     state_gdn, state_gdn_conv, state_ssm, state_ssm_conv)
    y_sample, new_sample = _trunk(x_sample, p_sample, pos_sample, sample_states, weights)
    (pr_ret, pr_mc, pr_mn, pr_mm, pr_gdn, pr_gconv, pr_ssm, pr_sconv) = new_prompt
    (sa_ret, sa_mc, sa_mn, sa_mm, sa_gdn, sa_gconv, sa_ssm, sa_sconv) = new_sample
    return (y_prompt, y_sample,
            pr_ret, pr_mc, pr_mn, pr_mm, pr_gdn, pr_gconv, pr_ssm, pr_sconv,
            sa_ret, sa_mc, sa_mn, sa_mm, sa_gdn, sa_gconv, sa_ssm, sa_sconv)
```

```python
import functools
import math

import jax
import jax.numpy as jnp
from jax import lax
from jax.experimental import pallas as pl
from jax.experimental.pallas import tpu as pltpu

F32 = jnp.float32
BF16 = jnp.bfloat16
HIGHEST = lax.Precision.HIGHEST

D_MODEL = 1024
DEPTH = 4
PAST_LEN = 16384
N_HEADS = 4
HEAD_DIM = 64
GROUP_W = N_HEADS * HEAD_DIM
SSM_STATE = 128
SSM_GROUPS = 2
CONV_W = 4
CONV_CH = 3 * GROUP_W
PROMPT_CHUNK = 64
ROPE_BASE = 10000.0
FFN_DENSE = 2816
N_EXPERTS = 8
FFN_EXPERT = 3584
PLE_DIM = 256
EPS = 1e-6

LANES = 128
SUBLANES = 8
VMEM_LIMIT = 56 * 1024 * 1024

C_RET = 0
C_ML = 1024
C_GDN = 2048
C_SSM = 3072
C_GATE = 4096
D_PROJ = C_GATE + LANES
G_MI, G_MF, G_GB, G_GA, G_DT = 0, 4, 8, 12, 16

TM = 512
TMM = 512
FFN_SPLIT = 2


def _dot(a, b):
    return jnp.dot(a.astype(BF16), b.astype(BF16), preferred_element_type=F32)


def _dot_nt(a, b):
    return lax.dot_general(a.astype(BF16), b.astype(BF16), (((1,), (1,)), ((), ())),
                           preferred_element_type=F32)


def _dot_hi(a, b):
    return jnp.dot(a, b, precision=HIGHEST, preferred_element_type=F32)


def _sigmoid(x):
    return 1.0 / (1.0 + jnp.exp(-x))


def _silu(x):
    return x * _sigmoid(x)


def _softplus(x):
    return jnp.maximum(x, 0.0) + jnp.log1p(jnp.exp(-jnp.abs(x)))


def _rms(x, g):
    return x * lax.rsqrt(jnp.mean(x * x, -1, keepdims=True) + EPS) * g


def _proj_kernel(h_ref, g_ref, w_ref, o_ref):
    a = _rms(h_ref[...], g_ref[...])
    o_ref[...] = jnp.dot(a.astype(BF16), w_ref[...], preferred_element_type=F32)


def _proj_call(h, g, w):
    t = h.shape[0]
    return pl.pallas_call(
        _proj_kernel,
        grid=(t // TM,),
        in_specs=[
            pl.BlockSpec((TM, D_MODEL), lambda i: (i, 0)),
            pl.BlockSpec((1, D_MODEL), lambda i: (0, 0)),
            pl.BlockSpec((D_MODEL, D_PROJ), lambda i: (0, 0), pipeline_mode=pl.Buffered(1)),
        ],
        out_specs=pl.BlockSpec((TM, D_PROJ), lambda i: (i, 0)),
        out_shape=jax.ShapeDtypeStruct((t, D_PROJ), F32),
        compiler_params=pltpu.CompilerParams(
            dimension_semantics=("arbitrary",), vmem_limit_bytes=VMEM_LIMIT),
        name="proj",
    )(h, g, w)


def _mixer_kernel(L, proj_ref, cos_ref, sin_ref, gp_ref, norms_ref, gcw_ref, scw_ref, scb_ref,
                  sd_ref, s_ret, s_mc, s_mm, s_gdn, s_gconv, s_ssm, s_sconv, y_alias,
                  y_ref, n_ret, n_mc, n_mm, n_gdn, n_gconv, n_ssm, n_sconv,
                  xg_ref, xs_ref):
    del y_alias
    c = pl.program_id(1)

    @pl.when(c == 0)
    def _init():
        n_ret[...] = s_ret[...]
        n_mc[...] = s_mc[...]
        n_mm[...] = s_mm[...]
        n_gdn[...] = s_gdn[...]
        n_ssm[...] = s_ssm[...]
        xg_ref[0:SUBLANES, :] = s_gconv[0]
        xs_ref[0:SUBLANES, :] = s_sconv[0]

    P = proj_ref
    gp = gp_ref[...]
    norms = norms_ref[...]
    lane = lax.broadcasted_iota(jnp.int32, (L, LANES), 1)
    lane1 = lax.broadcasted_iota(jnp.int32, (1, LANES), 1)
    row_i = lax.broadcasted_iota(jnp.int32, (L, L), 0)
    col_i = lax.broadcasted_iota(jnp.int32, (L, L), 1)
    causal = row_i >= col_i
    strict = row_i > col_i

    gt = P[:, C_GATE:C_GATE + LANES] + gp[0:1, :]
    sp = _softplus(gt)
    sig = _sigmoid(gt)
    lsig = -_softplus(-gt)
    neg_a = -jnp.exp(gp[1:2, :])
    gdec = jnp.where(lane < G_MF, gp[2:3, :],
                     jnp.where(lane < G_GB, lsig,
                               jnp.where(lane < G_GA, 0.0,
                                         jnp.where(lane < G_DT + N_HEADS, neg_a * sp, 0.0))))
    bc_all = _dot_hi(causal.astype(F32), gdec)
    bc_all_t = bc_all.T
    gt_t = gt.T

    def decays(j):
        b_col = bc_all[:, j:j + 1]
        b_row = bc_all_t[j:j + 1, :]
        b_last = bc_all[L - 1:L, j:j + 1]
        return b_col, b_row, b_last

    def head(x, h):
        return x[:, HEAD_DIM * h:HEAD_DIM * (h + 1)]

    def layernorm(x):
        xc = x - jnp.mean(x, -1, keepdims=True)
        return xc * lax.rsqrt(jnp.mean(xc * xc, -1, keepdims=True) + EPS)

    def rmsnorm(x):
        return x * lax.rsqrt(jnp.mean(x * x, -1, keepdims=True) + EPS)

    def l2norm(x):
        return x * lax.rsqrt(jnp.sum(x * x, -1, keepdims=True) + EPS)

    cosv = cos_ref[...]
    sinv = sin_ref[...]
    lane_w = lax.broadcasted_iota(jnp.int32, (L, GROUP_W), 1)
    first_half = (lane_w % HEAD_DIM) < (HEAD_DIM // 2)

    def rope(x):
        swapped = jnp.where(first_half, pltpu.roll(x, GROUP_W - HEAD_DIM // 2, 1),
                            pltpu.roll(x, HEAD_DIM // 2, 1))
        return x * cosv + swapped * sinv

    rq = rope(P[:, C_RET:C_RET + GROUP_W])
    rk = rope(P[:, C_RET + GROUP_W:C_RET + 2 * GROUP_W]) * HEAD_DIM ** -0.5
    rv = P[:, C_RET + 2 * GROUP_W:C_RET + 3 * GROUP_W]
    rg = P[:, C_RET + 3 * GROUP_W:C_RET + 4 * GROUP_W]
    outs = []
    for h in range(N_HEADS):
        q, k, v = head(rq, h), head(rk, h), head(rv, h)
        b_col, b_row, b_last = decays(h)
        decay = jnp.where(causal, jnp.exp(b_col - b_row), 0.0)
        S = n_ret[0, h]
        scores = _dot_nt(q, k) * decay
        o = _dot(scores, v) + _dot(q * jnp.exp(b_col), S)
        k_end = k * jnp.exp(b_last - b_col)
        n_ret[0, h] = S * jnp.exp(b_last) + _dot(k_end.T, v)
        outs.append(layernorm(o))
    ro = jnp.concatenate(outs, axis=1) * norms[0:1, :]
    y_ref[:, 0:GROUP_W] = (_silu(rg) * ro).astype(y_ref.dtype)

    mq = P[:, C_ML:C_ML + GROUP_W]
    mk = P[:, C_ML + GROUP_W:C_ML + 2 * GROUP_W] * HEAD_DIM ** -0.5
    mv = P[:, C_ML + 2 * GROUP_W:C_ML + 3 * GROUP_W]
    mo = P[:, C_ML + 3 * GROUP_W:C_ML + 4 * GROUP_W]
    ones_col = (lax.broadcasted_iota(jnp.int32, (L, HEAD_DIM), 1) == 0).astype(F32)
    m_vec = n_mm[0]
    new_m = m_vec
    outs = []
    for h in range(N_HEADS):
        q, k, v = head(mq, h), head(mk, h), head(mv, h)
        v_aug = jnp.concatenate([v, ones_col], axis=1)
        b_col, b_row, b_last = decays(G_MF + h)
        i_row = gt_t[G_MI + h:G_MI + h + 1, :]
        i_col = gt[:, G_MI + h:G_MI + h + 1]
        m = m_vec[:, h:h + 1]
        dd = b_col - b_row + i_row
        inter = b_col + m
        m_row = jnp.maximum(jnp.max(jnp.where(causal, dd, -jnp.inf), axis=1, keepdims=True), inter)
        w_intra = jnp.where(causal, jnp.exp(dd - m_row), 0.0)
        w_state = jnp.exp(inter - m_row)
        C = n_mc[0, h]
        s = _dot_nt(q, k) * w_intra
        num = _dot(s, v_aug) + w_state * _dot(q, C)
        den = num[:, HEAD_DIM:HEAD_DIM + 1]
        hh = num[:, :HEAD_DIM] / jnp.maximum(jnp.abs(den), jnp.exp(-m_row))
        g_end = b_last - b_col + i_col
        m_new = jnp.maximum(b_last + m, jnp.max(g_end, axis=0, keepdims=True))
        w_c = jnp.exp(b_last + m - m_new)
        kw = k * jnp.exp(g_end - m_new)
        n_mc[0, h] = w_c * C + _dot(kw.T, v_aug)
        new_m = jnp.where(lane1 == h, m_new, new_m)
        outs.append(layernorm(hh))
    n_mm[0] = new_m
    mh = jnp.concatenate(outs, axis=1) * norms[1:2, :]
    y_ref[:, GROUP_W:2 * GROUP_W] = (_sigmoid(mo) * mh).astype(y_ref.dtype)

    xg_ref[SUBLANES:SUBLANES + L, :] = P[:, C_GDN:C_GDN + CONV_CH]
    gcw = gcw_ref[...]
    conv = gcw[0:1, :] * xg_ref[pl.ds(SUBLANES - 3, L), :]
    for j in range(1, CONV_W):
        conv = conv + gcw[j:j + 1, :] * xg_ref[pl.ds(SUBLANES - 3 + j, L), :]
    hist = xg_ref[L:L + SUBLANES, :]
    n_gconv[0] = hist
    xg_ref[0:SUBLANES, :] = hist
    act = _silu(conv)
    gq, gk, gv = act[:, 0:GROUP_W], act[:, GROUP_W:2 * GROUP_W], act[:, 2 * GROUP_W:3 * GROUP_W]
    gz = P[:, C_GDN + CONV_CH:C_GDN + CONV_CH + GROUP_W]
    outs = []
    for h in range(N_HEADS):
        q = l2norm(head(gq, h)) * HEAD_DIM ** -0.5
        k = l2norm(head(gk, h))
        v = head(gv, h)
        beta = sig[:, G_GB + h:G_GB + h + 1]
        b_col, b_row, b_last = decays(G_GA + h)
        decay = jnp.where(causal, jnp.exp(b_col - b_row), 0.0)
        kb = k * beta
        a_low = jnp.where(strict, _dot_nt(kb, k) * decay, 0.0)
        rhs = jnp.concatenate([v * beta, kb * jnp.exp(b_col)], axis=1)
        sol = rhs - _dot_hi(a_low, rhs)
        pw = a_low
        for _ in range(int(math.log2(L)) - 1):
            pw = _dot_hi(pw, pw)
            sol = sol + _dot_hi(pw, sol)
        u, w = sol[:, :HEAD_DIM], sol[:, HEAD_DIM:]
        S = n_gdn[0, h]
        v_new = u - _dot(w, S)
        attn = _dot_nt(q, k) * decay
        o = _dot(q * jnp.exp(b_col), S) + _dot(attn, v_new)
        k_end = k * jnp.exp(b_last - b_col)
        n_gdn[0, h] = S * jnp.exp(b_last) + _dot(k_end.T, v_new)
        outs.append(rmsnorm(o))
    go = jnp.concatenate(outs, axis=1) * norms[2:3, :]
    y_ref[:, 2 * GROUP_W:3 * GROUP_W] = (go * _silu(gz)).astype(y_ref.dtype)

    xs_ref[SUBLANES:SUBLANES + L, :] = P[:, C_SSM + GROUP_W:C_SSM + GROUP_W + CONV_CH]
    scw = scw_ref[...]
    conv = scw[0:1, :] * xs_ref[pl.ds(SUBLANES - 3, L), :]
    for j in range(1, CONV_W):
        conv = conv + scw[j:j + 1, :] * xs_ref[pl.ds(SUBLANES - 3 + j, L), :]
    conv = conv + scb_ref[...]
    hist = xs_ref[L:L + SUBLANES, :]
    n_sconv[0] = hist
    xs_ref[0:SUBLANES, :] = hist
    act = _silu(conv)
    sx = act[:, 0:GROUP_W]
    sz = P[:, C_SSM:C_SSM + GROUP_W]
    sd = sd_ref[...]
    heads_per_group = N_HEADS // SSM_GROUPS
    outs = []
    for g in range(SSM_GROUPS):
        b_g = act[:, GROUP_W + SSM_STATE * g:GROUP_W + SSM_STATE * (g + 1)]
        c_g = act[:, GROUP_W + SSM_GROUPS * SSM_STATE + SSM_STATE * g:
                  GROUP_W + SSM_GROUPS * SSM_STATE + SSM_STATE * (g + 1)]
        cb = _dot_nt(c_g, b_g)
        for hh in range(heads_per_group):
            h = g * heads_per_group + hh
            x = head(sx, h)
            dt = sp[:, G_DT + h:G_DT + h + 1]
            b_col, b_row, b_last = decays(G_DT + h)
            decay = jnp.where(causal, jnp.exp(b_col - b_row), 0.0)
            v = x * dt
            S = n_ssm[0, h]
            o = _dot(cb * decay, v) + _dot(c_g * jnp.exp(b_col), S)
            k_end = b_g * jnp.exp(b_last - b_col)
            n_ssm[0, h] = S * jnp.exp(b_last) + _dot(k_end.T, v)
            outs.append(o + head(sd, h) * x)
    yv = jnp.concatenate(outs, axis=1) * _silu(sz)
    gw = GROUP_W // SSM_GROUPS
    yn = jnp.concatenate([rmsnorm(yv[:, gw * g:gw * (g + 1)]) for g in range(SSM_GROUPS)], axis=1)
    y_ref[:, 3 * GROUP_W:4 * GROUP_W] = (yn * norms[3:4, :]).astype(y_ref.dtype)


def _mixer_call(proj, y_all, row0, n_seq, n_chunks, L, tables, lp, states):
    cos_t, sin_t = tables
    blk0 = row0 // L
    tmap = lambda b, c: (blk0 + b * n_chunks + c, 0)
    const2 = lambda b, c: (0, 0)
    st4 = lambda b, c: (b, 0, 0, 0)
    st3 = lambda b, c: (b, 0, 0)
    state_specs = [
        pl.BlockSpec((1, N_HEADS, HEAD_DIM, HEAD_DIM), st4),
        pl.BlockSpec((1, N_HEADS, HEAD_DIM, LANES), st4),
        pl.BlockSpec((1, 1, LANES), st3),
        pl.BlockSpec((1, N_HEADS, HEAD_DIM, HEAD_DIM), st4),
        pl.BlockSpec((1, SUBLANES, CONV_CH), st3),
        pl.BlockSpec((1, N_HEADS, SSM_STATE, HEAD_DIM), st4),
        pl.BlockSpec((1, SUBLANES, CONV_CH), st3),
    ]
    in_specs = [
        pl.BlockSpec((L, D_PROJ), tmap),
        pl.BlockSpec((L, GROUP_W), lambda b, c: (c, 0)),
        pl.BlockSpec((L, GROUP_W), lambda b, c: (c, 0)),
        pl.BlockSpec((SUBLANES, LANES), const2),
        pl.BlockSpec((N_HEADS, GROUP_W), const2),
        pl.BlockSpec((CONV_W, CONV_CH), const2),
        pl.BlockSpec((CONV_W, CONV_CH), const2),
        pl.BlockSpec((1, CONV_CH), const2),
        pl.BlockSpec((1, GROUP_W), const2),
    ] + state_specs + [pl.BlockSpec(memory_space=pl.ANY)]
    out_specs = [pl.BlockSpec((L, D_MODEL), tmap)] + state_specs
    out_shape = [jax.ShapeDtypeStruct(y_all.shape, y_all.dtype)] + [
        jax.ShapeDtypeStruct(s.shape, F32) for s in states]
    n_in = len(in_specs)
    res = pl.pallas_call(
        functools.partial(_mixer_kernel, L),
        grid=(n_seq, n_chunks),
        in_specs=in_specs,
        out_specs=out_specs,
        out_shape=out_shape,
        scratch_shapes=[pltpu.VMEM((L + SUBLANES, CONV_CH), F32),
                        pltpu.VMEM((L + SUBLANES, CONV_CH), F32)],
        input_output_aliases={n_in - 1: 0},
        compiler_params=pltpu.CompilerParams(
            dimension_semantics=("arbitrary", "arbitrary"), vmem_limit_bytes=VMEM_LIMIT),
        name=f"mixer_L{L}",
    )(proj, cos_t, sin_t, lp["gp"], lp["norms"], lp["gcw"], lp["scw"], lp["scb"], lp["sd"],
      *states, y_all)
    return res[0], res[1:]


def _ple(h3, p_ref, pn_ref, pg_ref, pp_ref):
    r = _rms(h3, pn_ref[...])
    gate = _sigmoid(jnp.dot(r.astype(BF16), pg_ref[...], preferred_element_type=F32))
    proj = jnp.dot(p_ref[...].astype(BF16), pp_ref[...], preferred_element_type=F32)
    return h3 + gate * proj


def _dense_kernel(h_ref, y_ref, wo_ref, nf_ref, wg_ref, wu_ref, wd_ref, p_ref, pn_ref, pg_ref,
                  pp_ref, o_ref):
    h2 = h_ref[...] + jnp.dot(y_ref[...], wo_ref[...], preferred_element_type=F32)
    c = _rms(h2, nf_ref[...]).astype(BF16)
    fw = FFN_DENSE // FFN_SPLIT
    h3 = h2
    for s in range(FFN_SPLIT):
        g = jnp.dot(c, wg_ref[:, fw * s:fw * (s + 1)], preferred_element_type=F32)
        u = jnp.dot(c, wu_ref[:, fw * s:fw * (s + 1)], preferred_element_type=F32)
        a = (_silu(g) * u).astype(BF16)
        h3 = h3 + jnp.dot(a, wd_ref[fw * s:fw * (s + 1), :], preferred_element_type=F32)
    o_ref[...] = _ple(h3, p_ref, pn_ref, pg_ref, pp_ref)


def _resident(shape):
    return pl.BlockSpec(shape, lambda i: (0,) * len(shape), pipeline_mode=pl.Buffered(1))


def _dense_call(h, y, p_all, layer, lw):
    t = h.shape[0]
    rows = lambda w: pl.BlockSpec((TM, w), lambda i: (i, 0))
    return pl.pallas_call(
        _dense_kernel,
        grid=(t // TM,),
        in_specs=[
            rows(D_MODEL), rows(D_MODEL),
            _resident((D_MODEL, D_MODEL)), _resident((1, D_MODEL)),
            _resident((D_MODEL, FFN_DENSE)), _resident((D_MODEL, FFN_DENSE)),
            _resident((FFN_DENSE, D_MODEL)),
            pl.BlockSpec((None, TM, PLE_DIM), lambda i: (layer, i, 0)),
            _resident((1, D_MODEL)), _resident((D_MODEL, D_MODEL)), _resident((PLE_DIM, D_MODEL)),
        ],
        out_specs=rows(D_MODEL),
        out_shape=jax.ShapeDtypeStruct((t, D_MODEL), F32),
        compiler_params=pltpu.CompilerParams(
            dimension_semantics=("arbitrary",), vmem_limit_bytes=VMEM_LIMIT),
        name="dense_ffn",
    )(h, y, lw["w_out"], lw["norm_ffn"], lw["wg"], lw["wu"], lw["wd"], p_all,
      lw["ple_norm"], lw["ple_w_gate"], lw["ple_w_proj"])


R_I0, R_I1, R_W0, R_W1 = 0, 1, 2, 3


def _moe_pre_kernel(h_ref, y_ref, wo_ref, nf_ref, rt_ref, h2_ref, c_ref, r_ref):
    h2 = h_ref[...] + jnp.dot(y_ref[...], wo_ref[...], preferred_element_type=F32)
    h2_ref[...] = h2
    c = _rms(h2, nf_ref[...])
    c_ref[...] = c.astype(BF16)
    lane = lax.broadcasted_iota(jnp.int32, (TM, LANES), 1)
    logits = jnp.where(lane < N_EXPERTS, _dot_hi(c, rt_ref[...]), -jnp.inf)
    m1 = jnp.max(logits, axis=1, keepdims=True)
    i1 = jnp.min(jnp.where(logits == m1, lane, LANES), axis=1, keepdims=True)
    rest = jnp.where(lane == i1, -jnp.inf, logits)
    m2 = jnp.max(rest, axis=1, keepdims=True)
    i2 = jnp.min(jnp.where(rest == m2, lane, LANES), axis=1, keepdims=True)
    e2 = jnp.exp(m2 - m1)
    den = 1.0 + e2
    r_ref[...] = jnp.where(lane == R_I0, i1.astype(F32),
                           jnp.where(lane == R_I1, i2.astype(F32),
                                     jnp.where(lane == R_W0, 1.0 / den,
                                               jnp.where(lane == R_W1, e2 / den, 0.0))))


def _moe_pre_call(h, y, lw):
    t = h.shape[0]
    rows = lambda w: pl.BlockSpec((TM, w), lambda i: (i, 0))
    return pl.pallas_call(
        _moe_pre_kernel,
        grid=(t // TM,),
        in_specs=[rows(D_MODEL), rows(D_MODEL), _resident((D_MODEL, D_MODEL)),
                  _resident((1, D_MODEL)), _resident((D_MODEL, LANES))],
        out_specs=[rows(D_MODEL), rows(D_MODEL), rows(LANES)],
        out_shape=[jax.ShapeDtypeStruct((t, D_MODEL), F32),
                   jax.ShapeDtypeStruct((t, D_MODEL), BF16),
                   jax.ShapeDtypeStruct((t, LANES), F32)],
        compiler_params=pltpu.CompilerParams(
            dimension_semantics=("arbitrary",), vmem_limit_bytes=VMEM_LIMIT),
        name="moe_pre",
    )(h, y, lw["w_out"], lw["norm_ffn"], lw["router"])


def _moe_ffn_kernel(be_ref, nb_ref, x_ref, wg_ref, wu_ref, wd_ref, o_ref):
    del be_ref
    i = pl.program_id(0)

    @pl.when(i < nb_ref[0])
    def _compute():
        x = x_ref[...]
        fw = FFN_EXPERT // FFN_SPLIT
        acc = jnp.zeros((TMM, D_MODEL), F32)
        for s in range(FFN_SPLIT):
            g = jnp.dot(x, wg_ref[0, :, fw * s:fw * (s + 1)], preferred_element_type=F32)
            u = jnp.dot(x, wu_ref[0, :, fw * s:fw * (s + 1)], preferred_element_type=F32)
            a = (_silu(g) * u).astype(BF16)
            acc = acc + jnp.dot(a, wd_ref[0, fw * s:fw * (s + 1), :], preferred_element_type=F32)
        o_ref[...] = acc.astype(o_ref.dtype)

    @pl.when(i >= nb_ref[0])
    def _skip():
        o_ref[...] = jnp.zeros_like(o_ref)


def _moe_ffn_call(xs, blk_e, nblk, lw):
    n_rows = xs.shape[0]
    wspec = lambda shape: pl.BlockSpec(shape, lambda i, be, nb: (be[i], 0, 0),
                                       pipeline_mode=pl.Buffered(1))
    grid_spec = pltpu.PrefetchScalarGridSpec(
        num_scalar_prefetch=2,
        grid=(n_rows // TMM,),
        in_specs=[
            pl.BlockSpec((TMM, D_MODEL), lambda i, be, nb: (i, 0)),
            wspec((1, D_MODEL, FFN_EXPERT)), wspec((1, D_MODEL, FFN_EXPERT)),
            wspec((1, FFN_EXPERT, D_MODEL)),
        ],
        out_specs=pl.BlockSpec((TMM, D_MODEL), lambda i, be, nb: (i, 0)),
    )
    return pl.pallas_call(
        _moe_ffn_kernel,
        grid_spec=grid_spec,
        out_shape=jax.ShapeDtypeStruct((n_rows, D_MODEL), BF16),
        compiler_params=pltpu.CompilerParams(
            dimension_semantics=("arbitrary",), vmem_limit_bytes=VMEM_LIMIT),
        name="moe_ffn",
    )(blk_e, nblk, xs, lw["wg"], lw["wu"], lw["wd"])


def _moe_post_kernel(final, h2_ref, y0_ref, y1_ref, r_ref, p_ref, pn_ref, pg_ref, pp_ref, nfin_ref,
                     o_ref):
    r = r_ref[...]
    w0 = r[:, R_W0:R_W0 + 1]
    w1 = r[:, R_W1:R_W1 + 1]
    h3 = h2_ref[...] + (w0 * y0_ref[...].astype(F32) + w1 * y1_ref[...].astype(F32))
    h4 = _ple(h3, p_ref, pn_ref, pg_ref, pp_ref)
    o_ref[...] = _rms(h4, nfin_ref[...]) if final else h4


def _moe_post_call(h2, y0, y1, route, p_all, layer, lw, norm_final, final):
    t = h2.shape[0]
    rows = lambda w: pl.BlockSpec((TM, w), lambda i: (i, 0))
    return pl.pallas_call(
        functools.partial(_moe_post_kernel, final),
        grid=(t // TM,),
        in_specs=[rows(D_MODEL), rows(D_MODEL), rows(D_MODEL), rows(LANES),
                  pl.BlockSpec((None, TM, PLE_DIM), lambda i: (layer, i, 0)),
                  _resident((1, D_MODEL)), _resident((D_MODEL, D_MODEL)),
                  _resident((PLE_DIM, D_MODEL)), _resident((1, D_MODEL))],
        out_specs=rows(D_MODEL),
        out_shape=jax.ShapeDtypeStruct((t, D_MODEL), F32),
        compiler_params=pltpu.CompilerParams(
            dimension_semantics=("arbitrary",), vmem_limit_bytes=VMEM_LIMIT),
        name="moe_post",
    )(h2, y0, y1, route, p_all, lw["ple_norm"], lw["ple_w_gate"], lw["ple_w_proj"], norm_final)


def _route_plan(route, n_rows):
    t = route.shape[0]
    e_flat = route[:, R_I0:R_I1 + 1].astype(jnp.int32).reshape(-1)
    onehot = (e_flat[:, None] == jnp.arange(N_EXPERTS, dtype=jnp.int32)[None, :]).astype(jnp.int32)
    csum = jnp.cumsum(onehot, axis=0)
    rank = jnp.sum((csum - onehot) * onehot, axis=1)
    counts = csum[-1]
    pcounts = ((counts + TMM - 1) // TMM) * TMM
    ends = jnp.cumsum(pcounts)
    starts = ends - pcounts
    pos = starts[e_flat] + rank
    src = jnp.zeros((n_rows,), jnp.int32).at[pos].set(jnp.arange(2 * t, dtype=jnp.int32) // 2)
    blk_start = jnp.arange(n_rows // TMM, dtype=jnp.int32) * TMM
    blk_e = jnp.minimum(jnp.sum((blk_start[:, None] >= ends[None, :]).astype(jnp.int32), axis=1),
                        N_EXPERTS - 1).astype(jnp.int32)
    nblk = (ends[-1:] // TMM).astype(jnp.int32)
    return src, pos.reshape(t, 2), blk_e, nblk


def _rope_tables(pos):
    half = HEAD_DIM // 2
    inv_freq = ROPE_BASE ** (-jnp.arange(half, dtype=F32) / half)
    ang = pos[:, None] * inv_freq[None, :]
    cos, sin = jnp.cos(ang), jnp.sin(ang)
    cos_h = jnp.concatenate([cos, cos], axis=-1)
    sin_h = jnp.concatenate([-sin, sin], axis=-1)
    return jnp.tile(cos_h, (1, N_HEADS)), jnp.tile(sin_h, (1, N_HEADS))


def _pad_lanes(x, width=LANES):
    return jnp.pad(x, [(0, 0)] * (x.ndim - 1) + [(0, width - x.shape[-1])])


def kernel(x_prompt, x_sample, state_ret, state_mlstm_c, state_mlstm_n, state_mlstm_m, state_gdn, state_gdn_conv, state_ssm, state_ssm_conv, p_prompt, p_sample, norm_mix, w_in, w_out, ret_norm, mlstm_i_bias, mlstm_f_bias, mlstm_norm, gdn_conv_w, gdn_a_log, gdn_dt_bias, gdn_norm, ssm_conv_w, ssm_conv_b, ssm_dt_bias, ssm_a_log, ssm_d, ssm_norm, norm_ffn, ffn_w_gate, ffn_w_up, ffn_w_down, moe_router, moe_w_gate, moe_w_up, moe_w_down, ple_w_proj, ple_norm, ple_w_gate, norm_final):
    bp, tp = x_prompt.shape[:2]
    bs, ts = x_sample.shape[:2]
    n_prompt = bp * tp
    n_all = n_prompt + bs * ts
    assert tp % PROMPT_CHUNK == 0 and n_all % TM == 0 and ts == SUBLANES and DEPTH % 2 == 0

    h = jnp.concatenate([x_prompt.reshape(n_prompt, D_MODEL), x_sample.reshape(bs * ts, D_MODEL)])
    h = h.astype(F32)
    p_all = jnp.concatenate([p_prompt.reshape(DEPTH, n_prompt, PLE_DIM),
                             p_sample.reshape(DEPTH, bs * ts, PLE_DIM)], axis=1)

    o_ml = 4 * GROUP_W
    o_gdn = o_ml + 4 * GROUP_W + 2 * N_HEADS
    o_ssm = o_gdn + 4 * GROUP_W + 2 * N_HEADS
    w_in_r = jnp.concatenate([
        w_in[:, :, 0:o_ml + 4 * GROUP_W],
        w_in[:, :, o_gdn:o_gdn + 4 * GROUP_W],
        w_in[:, :, o_ssm:o_ssm + 4 * GROUP_W],
        w_in[:, :, o_ml + 4 * GROUP_W:o_gdn],
        w_in[:, :, o_gdn + 4 * GROUP_W:o_ssm],
        w_in[:, :, o_ssm + 4 * GROUP_W:],
    ], axis=-1)
    w_in_r = _pad_lanes(w_in_r, D_PROJ).astype(BF16)

    zeros4 = jnp.zeros((DEPTH, N_HEADS), F32)
    gate_bias = jnp.concatenate([mlstm_i_bias, mlstm_f_bias, zeros4, gdn_dt_bias, ssm_dt_bias], axis=-1)
    gate_alog = jnp.concatenate([zeros4, zeros4, zeros4, gdn_a_log, ssm_a_log], axis=-1)
    log_gamma = jnp.log1p(-jnp.exp2(-5.0 - jnp.arange(N_HEADS, dtype=F32)))
    gp = jnp.stack([_pad_lanes(gate_bias.astype(F32)), _pad_lanes(gate_alog.astype(F32)),
                    jnp.broadcast_to(_pad_lanes(log_gamma)[None], (DEPTH, LANES))], axis=1)
    gp = jnp.pad(gp, ((0, 0), (0, SUBLANES - gp.shape[1]), (0, 0)))
    norms = jnp.stack([ret_norm, mlstm_norm, gdn_norm, ssm_norm], axis=1).astype(F32)
    ssm_d_w = jnp.repeat(ssm_d.astype(F32), HEAD_DIM, axis=-1)[:, None, :]

    tab_prompt = _rope_tables(jnp.arange(tp, dtype=F32))
    tab_sample = _rope_tables(PAST_LEN + jnp.arange(ts, dtype=F32))

    def conv_in(s):
        return jnp.pad(s.astype(F32), ((0, 0), (SUBLANES - (CONV_W - 1), 0), (0, 0)))

    def states_in(i, n_seq, given):
        if given:
            mc = jnp.concatenate([state_mlstm_c[i], state_mlstm_n[i][..., None]], axis=-1)
            return (state_ret[i].astype(F32), _pad_lanes(mc.astype(F32)),
                    _pad_lanes(state_mlstm_m[i].astype(F32))[:, None, :], state_gdn[i].astype(F32),
                    conv_in(state_gdn_conv[i]), state_ssm[i].astype(F32), conv_in(state_ssm_conv[i]))
        z = lambda *s: jnp.zeros((n_seq,) + s, F32)
        return (z(N_HEADS, HEAD_DIM, HEAD_DIM), z(N_HEADS, HEAD_DIM, LANES), z(1, LANES),
                z(N_HEADS, HEAD_DIM, HEAD_DIM), z(SUBLANES, CONV_CH),
                z(N_HEADS, SSM_STATE, HEAD_DIM), z(SUBLANES, CONV_CH))

    def states_out(st):
        n_ret, n_mc, n_mm, n_gdn, n_gconv, n_ssm, n_sconv = st
        keep = SUBLANES - (CONV_W - 1)
        return (n_ret, n_mc[..., :HEAD_DIM], n_mc[..., HEAD_DIM], n_mm[:, 0, :N_HEADS], n_gdn,
                n_gconv[:, keep:], n_ssm, n_sconv[:, keep:])

    n_moe_rows = 2 * n_all + N_EXPERTS * TMM
    new_prompt, new_sample = [], []
    for i in range(DEPTH):
        lp = dict(gp=gp[i], norms=norms[i], gcw=gdn_conv_w[i].astype(F32),
                  scw=ssm_conv_w[i].astype(F32), scb=ssm_conv_b[i].astype(F32)[None, :],
                  sd=ssm_d_w[i])
        proj = _proj_call(h, norm_mix[i].astype(F32)[None, :], w_in_r[i])
        y_all = jnp.zeros((n_all, D_MODEL), BF16)
        y_all, st_p = _mixer_call(proj, y_all, 0, bp, tp // PROMPT_CHUNK, PROMPT_CHUNK, tab_prompt,
                                  lp, states_in(i, bp, False))
        y_all, st_s = _mixer_call(proj, y_all, n_prompt, bs, 1, ts, tab_sample, lp,
                                  states_in(i, bs, True))
        new_prompt.append(states_out(st_p))
        new_sample.append(states_out(st_s))

        j = i // 2
        lw = dict(w_out=w_out[i].astype(BF16), norm_ffn=norm_ffn[i].astype(F32)[None, :],
                  ple_norm=ple_norm[i].astype(F32)[None, :], ple_w_gate=ple_w_gate[i].astype(BF16),
                  ple_w_proj=ple_w_proj[i].astype(BF16))
        if i % 2 == 0:
            lw.update(wg=ffn_w_gate[j].astype(BF16), wu=ffn_w_up[j].astype(BF16),
                      wd=ffn_w_down[j].astype(BF16))
            h = _dense_call(h, y_all, p_all, i, lw)
        else:
            lw.update(router=_pad_lanes(moe_router[j].astype(F32)), wg=moe_w_gate[j].astype(BF16),
                      wu=moe_w_up[j].astype(BF16), wd=moe_w_down[j].astype(BF16))
            h2, c, route = _moe_pre_call(h, y_all, lw)
            src, pos, blk_e, nblk = _route_plan(route, n_moe_rows)
            ys = _moe_ffn_call(jnp.take(c, src, axis=0), blk_e, nblk, lw)
            y0 = jnp.take(ys, pos[:, 0], axis=0)
            y1 = jnp.take(ys, pos[:, 1], axis=0)
            h = _moe_post_call(h2, y0, y1, route, p_all, i, lw, norm_final.astype(F32)[None, :],
                               final=(i == DEPTH - 1))

    y = h.astype(x_prompt.dtype)
    y_prompt = y[:n_prompt].reshape(bp, tp, D_MODEL)
    y_sample = y[n_prompt:].reshape(bs, ts, D_MODEL)
    stack = lambda lst: tuple(jnp.stack([l[k] for l in lst]) for k in range(8))
    return (y_prompt, y_sample) + stack(new_prompt) + stack(new_sample)
```

```python
import functools
import math

import jax
import jax.numpy as jnp
from jax import lax
from jax.experimental import pallas as pl
from jax.experimental.pallas import tpu as pltpu

F32 = jnp.float32
BF16 = jnp.bfloat16
HIGHEST = lax.Precision.HIGHEST

D_MODEL = 1024
DEPTH = 4
PAST_LEN = 16384
N_HEADS = 4
HEAD_DIM = 64
GROUP_W = N_HEADS * HEAD_DIM
SSM_STATE = 128
SSM_GROUPS = 2
CONV_W = 4
CONV_CH = 3 * GROUP_W
PROMPT_CHUNK = 64
ROPE_BASE = 10000.0
FFN_DENSE = 2816
N_EXPERTS = 8
FFN_EXPERT = 3584
PLE_DIM = 256
EPS = 1e-6

LANES = 128
SUBLANES = 8
VMEM_LIMIT = 56 * 1024 * 1024

C_RET = 0
C_ML = 1024
C_GDN = 2048
C_SSM = 3072
C_GATE = 4096
C_XMI = C_GATE + LANES
C_XMF = C_XMI + GROUP_W
C_XGB = C_XMF + GROUP_W
C_XGA = C_XGB + GROUP_W
C_XDT = C_XGA + GROUP_W
D_PROJ = C_XDT + GROUP_W
G_MI, G_MF, G_GB, G_GA, G_DT = 0, 4, 8, 12, 16

PROMPT_SEQS = 2
N_LEVELS = PROMPT_CHUNK.bit_length() - 1
TM = 512
TMM = 512
FFN_SPLIT = 2


def _dot(a, b):
    return jnp.dot(a.astype(BF16), b.astype(BF16), preferred_element_type=F32)


def _dot_nt(a, b):
    return lax.dot_general(a.astype(BF16), b.astype(BF16), (((1,), (1,)), ((), ())),
                           preferred_element_type=F32)


def _dot_hi(a, b):
    return jnp.dot(a, b, precision=HIGHEST, preferred_element_type=F32)


def _sigmoid(x):
    return 1.0 / (1.0 + jnp.exp(-x))


def _silu(x):
    return x * _sigmoid(x)


def _softplus(x):
    return jnp.maximum(x, 0.0) + jnp.log1p(jnp.exp(-jnp.abs(x)))


def _rms(x, g):
    return x * lax.rsqrt(jnp.mean(x * x, -1, keepdims=True) + EPS) * g


def _proj_kernel(h_ref, g_ref, w_ref, o_ref):
    a = _rms(h_ref[...], g_ref[...])
    o_ref[...] = jnp.dot(a.astype(BF16), w_ref[...], preferred_element_type=F32)


def _proj_call(h, g, w):
    t = h.shape[0]
    return pl.pallas_call(
        _proj_kernel,
        grid=(t // TM,),
        in_specs=[
            pl.BlockSpec((TM, D_MODEL), lambda i: (i, 0)),
            pl.BlockSpec((1, D_MODEL), lambda i: (0, 0)),
            pl.BlockSpec((D_MODEL, D_PROJ), lambda i: (0, 0), pipeline_mode=pl.Buffered(1)),
        ],
        out_specs=pl.BlockSpec((TM, D_PROJ), lambda i: (i, 0)),
        out_shape=jax.ShapeDtypeStruct((t, D_PROJ), F32),
        compiler_params=pltpu.CompilerParams(
            dimension_semantics=("arbitrary",), vmem_limit_bytes=VMEM_LIMIT),
        name="proj",
    )(h, g, w)


def _mixer_kernel(L, proj_ref, cos_ref, sin_ref, gp_ref, norms_ref, gcw_ref, scw_ref, scb_ref,
                  sd_ref, s_ret, s_mc, s_mm, s_gdn, s_gconv, s_ssm, s_sconv,
                  y_ref, n_ret, n_mc, n_mm, n_gdn, n_gconv, n_ssm, n_sconv,
                  xg_ref, xs_ref):
    c = pl.program_id(1)

    @pl.when(c == 0)
    def _init():
        n_ret[...] = s_ret[...]
        n_mc[...] = s_mc[...]
        n_mm[...] = s_mm[...]
        n_gdn[...] = s_gdn[...]
        n_ssm[...] = s_ssm[...]
        xg_ref[0:SUBLANES, :] = s_gconv[0]
        xs_ref[0:SUBLANES, :] = s_sconv[0]

    P = proj_ref
    gp = gp_ref[...]
    norms = norms_ref[...]
    lane = lax.broadcasted_iota(jnp.int32, (L, LANES), 1)
    lane1 = lax.broadcasted_iota(jnp.int32, (1, LANES), 1)
    row_i = lax.broadcasted_iota(jnp.int32, (L, L), 0)
    col_i = lax.broadcasted_iota(jnp.int32, (L, L), 1)
    causal = row_i >= col_i
    strict = row_i > col_i

    gt = P[:, C_GATE:C_GATE + LANES] + gp[0:1, :]
    sp = _softplus(gt)
    sig = _sigmoid(gt)
    lsig = -_softplus(-gt)
    neg_a = -jnp.exp(gp[1:2, :])
    gdec = jnp.where(lane < G_MF, gp[2:3, :],
                     jnp.where(lane < G_GB, lsig,
                               jnp.where(lane < G_GA, 0.0,
                                         jnp.where(lane < G_DT + N_HEADS, neg_a * sp, 0.0))))
    bc_all = _dot_hi(causal.astype(F32), gdec)
    bc_all_t = bc_all.T
    gt_t = gt.T

    def decays(j):
        b_col = bc_all[:, j:j + 1]
        b_row = bc_all_t[j:j + 1, :]
        b_last = bc_all[L - 1:L, j:j + 1]
        return b_col, b_row, b_last

    def head(x, h):
        return x[:, HEAD_DIM * h:HEAD_DIM * (h + 1)]

    def layernorm(x):
        xc = x - jnp.mean(x, -1, keepdims=True)
        return xc * lax.rsqrt(jnp.mean(xc * xc, -1, keepdims=True) + EPS)

    def rmsnorm(x):
        return x * lax.rsqrt(jnp.mean(x * x, -1, keepdims=True) + EPS)

    def l2norm(x):
        return x * lax.rsqrt(jnp.sum(x * x, -1, keepdims=True) + EPS)

    cosv = cos_ref[...]
    sinv = sin_ref[...]
    lane_w = lax.broadcasted_iota(jnp.int32, (L, GROUP_W), 1)
    first_half = (lane_w % HEAD_DIM) < (HEAD_DIM // 2)

    def rope(x):
        swapped = jnp.where(first_half, pltpu.roll(x, GROUP_W - HEAD_DIM // 2, 1),
                            pltpu.roll(x, HEAD_DIM // 2, 1))
        return x * cosv + swapped * sinv

    rq = rope(P[:, C_RET:C_RET + GROUP_W])
    rk = rope(P[:, C_RET + GROUP_W:C_RET + 2 * GROUP_W]) * HEAD_DIM ** -0.5
    rv = P[:, C_RET + 2 * GROUP_W:C_RET + 3 * GROUP_W]
    rg = P[:, C_RET + 3 * GROUP_W:C_RET + 4 * GROUP_W]
    outs = []
    for h in range(N_HEADS):
        q, k, v = head(rq, h), head(rk, h), head(rv, h)
        b_col, b_row, b_last = decays(h)
        decay = jnp.where(causal, jnp.exp(b_col - b_row), 0.0)
        S = n_ret[0, h]
        scores = _dot_nt(q, k) * decay
        o = _dot(scores, v) + _dot(q * jnp.exp(b_col), S)
        k_end = k * jnp.exp(b_last - b_col)
        n_ret[0, h] = S * jnp.exp(b_last) + _dot(k_end.T, v)
        outs.append(layernorm(o))
    ro = jnp.concatenate(outs, axis=1) * norms[0:1, :]
    y_ref[:, 0:GROUP_W] = (_silu(rg) * ro).astype(y_ref.dtype)

    mq = P[:, C_ML:C_ML + GROUP_W]
    mk = P[:, C_ML + GROUP_W:C_ML + 2 * GROUP_W] * HEAD_DIM ** -0.5
    mv = P[:, C_ML + 2 * GROUP_W:C_ML + 3 * GROUP_W]
    mo = P[:, C_ML + 3 * GROUP_W:C_ML + 4 * GROUP_W]
    ones_col = (lax.broadcasted_iota(jnp.int32, (L, HEAD_DIM), 1) == 0).astype(F32)
    m_vec = n_mm[0]
    new_m = m_vec
    outs = []
    for h in range(N_HEADS):
        q, k, v = head(mq, h), head(mk, h), head(mv, h)
        v_aug = jnp.concatenate([v, ones_col], axis=1)
        b_col, b_row, b_last = decays(G_MF + h)
        i_row = gt_t[G_MI + h:G_MI + h + 1, :]
        i_col = gt[:, G_MI + h:G_MI + h + 1]
        m = m_vec[:, h:h + 1]
        dd = b_col - b_row + i_row
        inter = b_col + m
        m_row = jnp.maximum(jnp.max(jnp.where(causal, dd, -jnp.inf), axis=1, keepdims=True), inter)
        w_intra = jnp.where(causal, jnp.exp(dd - m_row), 0.0)
        w_state = jnp.exp(inter - m_row)
        C = n_mc[0, h]
        s = _dot_nt(q, k) * w_intra
        num = _dot(s, v_aug) + w_state * _dot(q, C)
        den = num[:, HEAD_DIM:HEAD_DIM + 1]
        hh = num[:, :HEAD_DIM] / jnp.maximum(jnp.abs(den), jnp.exp(-m_row))
        g_end = b_last - b_col + i_col
        m_new = jnp.maximum(b_last + m, jnp.max(g_end, axis=0, keepdims=True))
        w_c = jnp.exp(b_last + m - m_new)
        kw = k * jnp.exp(g_end - m_new)
        n_mc[0, h] = w_c * C + _dot(kw.T, v_aug)
        new_m = jnp.where(lane1 == h, m_new, new_m)
        outs.append(layernorm(hh))
    n_mm[0] = new_m
    mh = jnp.concatenate(outs, axis=1) * norms[1:2, :]
    y_ref[:, GROUP_W:2 * GROUP_W] = (_sigmoid(mo) * mh).astype(y_ref.dtype)

    xg_ref[SUBLANES:SUBLANES + L, :] = P[:, C_GDN:C_GDN + CONV_CH]
    gcw = gcw_ref[...]
    conv = gcw[0:1, :] * xg_ref[pl.ds(SUBLANES - 3, L), :]
    for j in range(1, CONV_W):
        conv = conv + gcw[j:j + 1, :] * xg_ref[pl.ds(SUBLANES - 3 + j, L), :]
    hist = xg_ref[L:L + SUBLANES, :]
    n_gconv[0] = hist
    xg_ref[0:SUBLANES, :] = hist
    act = _silu(conv)
    gq, gk, gv = act[:, 0:GROUP_W], act[:, GROUP_W:2 * GROUP_W], act[:, 2 * GROUP_W:3 * GROUP_W]
    gz = P[:, C_GDN + CONV_CH:C_GDN + CONV_CH + GROUP_W]
    outs = []
    for h in range(N_HEADS):
        q = l2norm(head(gq, h)) * HEAD_DIM ** -0.5
        k = l2norm(head(gk, h))
        v = head(gv, h)
        beta = sig[:, G_GB + h:G_GB + h + 1]
        b_col, b_row, b_last = decays(G_GA + h)
        decay = jnp.where(causal, jnp.exp(b_col - b_row), 0.0)
        kb = k * beta
        a_low = jnp.where(strict, _dot_nt(kb, k) * decay, 0.0)
        rhs = jnp.concatenate([v * beta, kb * jnp.exp(b_col)], axis=1)
        sol = rhs - _dot_hi(a_low, rhs)
        pw = a_low
        for _ in range(int(math.log2(L)) - 1):
            pw = _dot_hi(pw, pw)
            sol = sol + _dot_hi(pw, sol)
        u, w = sol[:, :HEAD_DIM], sol[:, HEAD_DIM:]
        S = n_gdn[0, h]
        v_new = u - _dot(w, S)
        attn = _dot_nt(q, k) * decay
        o = _dot(q * jnp.exp(b_col), S) + _dot(attn, v_new)
        k_end = k * jnp.exp(b_last - b_col)
        n_gdn[0, h] = S * jnp.exp(b_last) + _dot(k_end.T, v_new)
        outs.append(rmsnorm(o))
    go = jnp.concatenate(outs, axis=1) * norms[2:3, :]
    y_ref[:, 2 * GROUP_W:3 * GROUP_W] = (go * _silu(gz)).astype(y_ref.dtype)

    xs_ref[SUBLANES:SUBLANES + L, :] = P[:, C_SSM + GROUP_W:C_SSM + GROUP_W + CONV_CH]
    scw = scw_ref[...]
    conv = scw[0:1, :] * xs_ref[pl.ds(SUBLANES - 3, L), :]
    for j in range(1, CONV_W):
        conv = conv + scw[j:j + 1, :] * xs_ref[pl.ds(SUBLANES - 3 + j, L), :]
    conv = conv + scb_ref[...]
    hist = xs_ref[L:L + SUBLANES, :]
    n_sconv[0] = hist
    xs_ref[0:SUBLANES, :] = hist
    act = _silu(conv)
    sx = act[:, 0:GROUP_W]
    sz = P[:, C_SSM:C_SSM + GROUP_W]
    sd = sd_ref[...]
    heads_per_group = N_HEADS // SSM_GROUPS
    outs = []
    for g in range(SSM_GROUPS):
        b_g = act[:, GROUP_W + SSM_STATE * g:GROUP_W + SSM_STATE * (g + 1)]
        c_g = act[:, GROUP_W + SSM_GROUPS * SSM_STATE + SSM_STATE * g:
                  GROUP_W + SSM_GROUPS * SSM_STATE + SSM_STATE * (g + 1)]
        cb = _dot_nt(c_g, b_g)
        for hh in range(heads_per_group):
            h = g * heads_per_group + hh
            x = head(sx, h)
            dt = sp[:, G_DT + h:G_DT + h + 1]
            b_col, b_row, b_last = decays(G_DT + h)
            decay = jnp.where(causal, jnp.exp(b_col - b_row), 0.0)
            v = x * dt
            S = n_ssm[0, h]
            o = _dot(cb * decay, v) + _dot(c_g * jnp.exp(b_col), S)
            k_end = b_g * jnp.exp(b_last - b_col)
            n_ssm[0, h] = S * jnp.exp(b_last) + _dot(k_end.T, v)
            outs.append(o + head(sd, h) * x)
    yv = jnp.concatenate(outs, axis=1) * _silu(sz)
    gw = GROUP_W // SSM_GROUPS
    yn = jnp.concatenate([rmsnorm(yv[:, gw * g:gw * (g + 1)]) for g in range(SSM_GROUPS)], axis=1)
    y_ref[:, 3 * GROUP_W:4 * GROUP_W] = (yn * norms[3:4, :]).astype(y_ref.dtype)


def _mixer_call(proj, row0, n_seq, n_chunks, L, tables, lp, states):
    cos_t, sin_t = tables
    blk0 = row0 // L
    tmap = lambda b, c: (blk0 + b * n_chunks + c, 0)
    ymap = lambda b, c: (b * n_chunks + c, 0)
    const2 = lambda b, c: (0, 0)
    st4 = lambda b, c: (b, 0, 0, 0)
    st3 = lambda b, c: (b, 0, 0)
    state_specs = [
        pl.BlockSpec((1, N_HEADS, HEAD_DIM, HEAD_DIM), st4),
        pl.BlockSpec((1, N_HEADS, HEAD_DIM, LANES), st4),
        pl.BlockSpec((1, 1, LANES), st3),
        pl.BlockSpec((1, N_HEADS, HEAD_DIM, HEAD_DIM), st4),
        pl.BlockSpec((1, SUBLANES, CONV_CH), st3),
        pl.BlockSpec((1, N_HEADS, SSM_STATE, HEAD_DIM), st4),
        pl.BlockSpec((1, SUBLANES, CONV_CH), st3),
    ]
    in_specs = [
        pl.BlockSpec((L, D_PROJ), tmap),
        pl.BlockSpec((L, GROUP_W), lambda b, c: (c, 0)),
        pl.BlockSpec((L, GROUP_W), lambda b, c: (c, 0)),
        pl.BlockSpec((SUBLANES, LANES), const2),
        pl.BlockSpec((N_HEADS, GROUP_W), const2),
        pl.BlockSpec((CONV_W, CONV_CH), const2),
        pl.BlockSpec((CONV_W, CONV_CH), const2),
        pl.BlockSpec((1, CONV_CH), const2),
        pl.BlockSpec((1, GROUP_W), const2),
    ] + state_specs
    out_specs = [pl.BlockSpec((L, D_MODEL), ymap)] + state_specs
    out_shape = [jax.ShapeDtypeStruct((n_seq * n_chunks * L, D_MODEL), BF16)] + [
        jax.ShapeDtypeStruct(s.shape, F32) for s in states]
    res = pl.pallas_call(
        functools.partial(_mixer_kernel, L),
        grid=(n_seq, n_chunks),
        in_specs=in_specs,
        out_specs=out_specs,
        out_shape=out_shape,
        scratch_shapes=[pltpu.VMEM((L + SUBLANES, CONV_CH), F32),
                        pltpu.VMEM((L + SUBLANES, CONV_CH), F32)],
        compiler_params=pltpu.CompilerParams(
            dimension_semantics=("arbitrary", "arbitrary"), vmem_limit_bytes=VMEM_LIMIT),
        name=f"mixer_L{L}",
    )(proj, cos_t, sin_t, lp["gp"], lp["norms"], lp["gcw"], lp["scw"], lp["scb"], lp["sd"],
      *states)
    return res[0], res[1:]


def _split3(x):
    hi = x.astype(BF16)
    r1 = x - hi.astype(F32)
    mid = r1.astype(BF16)
    lo = (r1 - mid.astype(F32)).astype(BF16)
    return hi, mid, lo


def _dot_tn(a, b):
    return lax.dot_general(a, b, (((0,), (0,)), ((), ())), preferred_element_type=F32)


def _prompt_mixer_kernel(nb, *refs):
    L = PROMPT_CHUNK
    proj_refs = refs[:nb]
    (cos_ref, sin_ref, gb_ref, norms_ref, gcw_ref, scw_ref, scb_ref,
     mbd_b_ref, mbd_f_ref, m2_b_ref, m2_f_ref, pm_ref, lm_ref, tril3_ref, rt_ref, rs_ref,
     y_ref, s_ret, s_mc, s_mn, s_mm, s_gdn, s_gconv, s_ssm, s_sconv,
     xg_ref, xs_ref) = refs[nb:]
    c = pl.program_id(1)

    @pl.when(c == 0)
    def _init():
        for r in (s_ret, s_mc, s_mn, s_mm, s_gdn, s_ssm, xg_ref, xs_ref):
            r[...] = jnp.zeros_like(r)

    mbd_b = mbd_b_ref[...]
    mbd_f = mbd_f_ref[...]
    m2_b = m2_b_ref[...]
    m2_f = m2_f_ref[...]
    causal_p = pm_ref[0]
    strict_p = pm_ref[1]
    eye_p = pm_ref[2]
    ut_p = pm_ref[3]
    ninf_p = pm_ref[4]
    tril3 = tril3_ref[...]
    gb = gb_ref[...]
    norms = norms_ref[...]
    cosv = cos_ref[...]
    sinv = sin_ref[...]
    lane_w = lax.broadcasted_iota(jnp.int32, (L, GROUP_W), 1)
    first_half = (lane_w % HEAD_DIM) < (HEAD_DIM // 2)
    ones_b = jnp.ones((L, GROUP_W), BF16)
    neg_a_gdn = -jnp.exp(gb[4:5, :])
    neg_a_ssm = -jnp.exp(gb[5:6, :])
    inv_hd = 1.0 / HEAD_DIM

    def bf(x):
        return x.astype(BF16)

    def bd(xb, mask):
        return jnp.concatenate([xb] * (GROUP_W // L), axis=0) * mask

    def mm(a, b):
        return jnp.dot(a, b, preferred_element_type=F32)

    def mm_nt(a, b):
        return lax.dot_general(a, b, (((1,), (1,)), ((), ())), preferred_element_type=F32)

    def cumsum(g):
        return mm(tril3, jnp.concatenate(_split3(g), axis=0))

    def sum0(x):
        return jnp.sum(x, axis=0, keepdims=True)

    def segmax(x):
        parts = []
        for h in range(N_HEADS):
            mh = jnp.max(x[:, HEAD_DIM * h:HEAD_DIM * (h + 1)], axis=1, keepdims=True)
            parts.append(jnp.broadcast_to(mh, (L, HEAD_DIM)))
        return jnp.concatenate(parts, axis=1)

    def rope(x):
        swapped = jnp.where(first_half, pltpu.roll(x, GROUP_W - HEAD_DIM // 2, 1),
                            pltpu.roll(x, HEAD_DIM // 2, 1))
        return x * cosv + swapped * sinv

    def conv4(x_ref, j, w):
        acc = w[0:1, :] * x_ref[j, pl.ds(SUBLANES - 3, L), :]
        for k in range(1, CONV_W):
            acc = acc + w[k:k + 1, :] * x_ref[j, pl.ds(SUBLANES - 3 + k, L), :]
        return acc


    def ret_chain(j):
        P = proj_refs[j]
        rq = rope(P[:, C_RET:C_RET + GROUP_W])
        rk = rope(P[:, C_RET + GROUP_W:C_RET + 2 * GROUP_W]) * HEAD_DIM ** -0.5
        rvb = bf(P[:, C_RET + 2 * GROUP_W:C_RET + 3 * GROUP_W])
        S = s_ret[j]
        scores = mm_nt(bf(rq), bd(bf(rk), mbd_b)) * rt_ref[0]
        o_inter = mm(bf(rq * rt_ref[1]), bf(S))
        yield
        s_ret[j] = S * rs_ref[0:1, :] + mbd_f * _dot_tn(bf(rk * rt_ref[2]), rvb)
        o = mm(bf(scores), bd(rvb, mbd_b)) + o_inter
        yield
        mean = mm(bf(o), mbd_b) * inv_hd
        yield
        xc = o - mean
        var = mm(bf(xc * xc), mbd_b) * inv_hd
        yield
        ro = xc * lax.rsqrt(var + EPS) * norms[0:1, :]
        rg = P[:, C_RET + 3 * GROUP_W:C_RET + 4 * GROUP_W]
        y_ref[j, :, 0:GROUP_W] = (_silu(rg) * ro).astype(y_ref.dtype)

    def mlstm_chain(j):
        P = proj_refs[j]
        mq = bf(P[:, C_ML:C_ML + GROUP_W])
        mk = P[:, C_ML + GROUP_W:C_ML + 2 * GROUP_W] * HEAD_DIM ** -0.5
        mvb = bf(P[:, C_ML + 2 * GROUP_W:C_ML + 3 * GROUP_W])
        i_x = P[:, C_XMI:C_XMI + GROUP_W] + gb[0:1, :]
        g_f = -_softplus(-(P[:, C_XMF:C_XMF + GROUP_W] + gb[1:2, :]))
        b_col = cumsum(g_f)
        qk = mm_nt(mq, bd(bf(mk), mbd_b))
        C = s_mc[j]
        N = s_mn[j]
        q_c = mm(mq, bf(C))
        q_n = mm(mq, bf(N))
        yield
        b_row = sum0(g_f * ut_p)
        i_row = sum0(i_x * eye_p)
        b_last = b_col[L - 1:L, :]
        m_prev = s_mm[j, 0:1, :]
        dd = b_col - b_row + i_row
        inter = b_col + m_prev
        m_row = jnp.maximum(segmax(dd + ninf_p), inter)
        w_intra = jnp.exp(jnp.minimum(dd - m_row, 0.0)) * causal_p
        w_state = jnp.exp(inter - m_row)
        sb = bf(qk * w_intra)
        num = mm(sb, bd(mvb, mbd_b)) + w_state * q_c
        den = mm(sb, mbd_b) + w_state * q_n
        g_end = b_last - b_col + i_x
        m_new = jnp.maximum(b_last + m_prev, jnp.max(g_end, axis=0, keepdims=True))
        w_c = jnp.exp(b_last + m_prev - m_new)
        kwb = bf(mk * jnp.exp(g_end - m_new))
        s_mc[j] = w_c * C + mbd_f * _dot_tn(kwb, mvb)
        s_mn[j] = w_c * N + mbd_f * _dot_tn(kwb, ones_b)
        s_mm[j] = jnp.broadcast_to(m_new, (SUBLANES, GROUP_W))
        yield
        hh = num / jnp.maximum(jnp.abs(den), jnp.exp(-m_row))
        mean = mm(bf(hh), mbd_b) * inv_hd
        yield
        xc = hh - mean
        var = mm(bf(xc * xc), mbd_b) * inv_hd
        yield
        mh = xc * lax.rsqrt(var + EPS) * norms[1:2, :]
        mo = P[:, C_ML + 3 * GROUP_W:C_ML + 4 * GROUP_W]
        y_ref[j, :, GROUP_W:2 * GROUP_W] = (_sigmoid(mo) * mh).astype(y_ref.dtype)

    def gdn_chain(j):
        P = proj_refs[j]
        xg_ref[j, SUBLANES:SUBLANES + L, :] = P[:, C_GDN:C_GDN + CONV_CH]
        conv = conv4(xg_ref, j, gcw_ref[...])
        hist = xg_ref[j, L:L + SUBLANES, :]
        s_gconv[j] = hist
        xg_ref[j, 0:SUBLANES, :] = hist
        act = _silu(conv)
        gq, gk, gv = act[:, 0:GROUP_W], act[:, GROUP_W:2 * GROUP_W], act[:, 2 * GROUP_W:]
        q_ss = mm(bf(gq * gq), mbd_b)
        k_ss = mm(bf(gk * gk), mbd_b)
        g_g = neg_a_gdn * _softplus(P[:, C_XGA:C_XGA + GROUP_W] + gb[2:3, :])
        b_col = cumsum(g_g)
        yield
        qn = gq * lax.rsqrt(q_ss + EPS) * HEAD_DIM ** -0.5
        kn = gk * lax.rsqrt(k_ss + EPS)
        beta = _sigmoid(P[:, C_XGB:C_XGB + GROUP_W])
        b_row = sum0(g_g * ut_p)
        b_last = b_col[L - 1:L, :]
        e_col = jnp.exp(b_col)
        decay = jnp.exp(jnp.minimum(b_col - b_row, 0.0)) * causal_p
        kb = kn * beta
        bd_k = bd(bf(kn), mbd_b)
        kk = mm_nt(bf(kb), bd_k)
        qk = mm_nt(bf(qn), bd_k)
        S = s_gdn[j]
        sbf = bf(S)
        q_s = mm(bf(qn), sbf)
        yield
        a_low = kk * (decay * strict_p)
        attn = qk * decay
        t_inv = eye_p - a_low * lm_ref[0]
        for lev in range(1, N_LEVELS):
            m1 = mm(bf(a_low * lm_ref[lev]), bd(bf(t_inv), mbd_b))
            yield
            t_inv = t_inv - mm(bf(t_inv), bd(bf(m1), mbd_b))
            yield
        tb = bf(t_inv)
        u = mm(tb, bd(bf(gv * beta), mbd_b))
        w = mm(tb, bd(bf(kb * e_col), mbd_b))
        yield
        v_new = u - mm(bf(w), sbf)
        yield
        vnb = bf(v_new)
        o = e_col * q_s + mm(bf(attn), bd(vnb, mbd_b))
        k_end = kn * jnp.exp(b_last - b_col)
        s_gdn[j] = S * jnp.exp(b_last) + mbd_f * _dot_tn(bf(k_end), vnb)
        yield
        o_ss = mm(bf(o * o), mbd_b)
        yield
        go = o * lax.rsqrt(o_ss * inv_hd + EPS) * norms[2:3, :]
        gz = P[:, C_GDN + CONV_CH:C_GDN + CONV_CH + GROUP_W]
        y_ref[j, :, 2 * GROUP_W:3 * GROUP_W] = (go * _silu(gz)).astype(y_ref.dtype)

    def ssd_chain(j):
        P = proj_refs[j]
        xs_ref[j, SUBLANES:SUBLANES + L, :] = P[:, C_SSM + GROUP_W:C_SSM + GROUP_W + CONV_CH]
        conv = conv4(xs_ref, j, scw_ref[...]) + scb_ref[...]
        hist = xs_ref[j, L:L + SUBLANES, :]
        s_sconv[j] = hist
        xs_ref[j, 0:SUBLANES, :] = hist
        act = _silu(conv)
        sx = act[:, 0:GROUP_W]
        s_b = bf(act[:, GROUP_W:2 * GROUP_W])
        s_c = bf(act[:, 2 * GROUP_W:])
        dt = _softplus(P[:, C_XDT:C_XDT + GROUP_W] + gb[3:4, :])
        g_s = dt * neg_a_ssm
        b_col = cumsum(g_s)
        cb = mm_nt(s_c, bd(s_b, m2_b))
        S = s_ssm[j]
        c_s = mm(s_c, bf(S))
        yield
        b_row = sum0(g_s * ut_p)
        b_last = b_col[L - 1:L, :]
        decay = jnp.exp(jnp.minimum(b_col - b_row, 0.0)) * causal_p
        v = sx * dt
        o = mm(bf(cb * decay), bd(bf(v), mbd_b)) + jnp.exp(b_col) * c_s
        v_end = bf(v * jnp.exp(b_last - b_col))
        s_ssm[j] = S * jnp.exp(b_last) + m2_f * _dot_tn(s_b, v_end)
        yield
        sz = P[:, C_SSM:C_SSM + GROUP_W]
        yv = (o + gb[6:7, :] * sx) * _silu(sz)
        ms = mm(bf(yv * yv), m2_b) * (1.0 / (GROUP_W // SSM_GROUPS))
        yield
        y_ref[j, :, 3 * GROUP_W:] = (yv * lax.rsqrt(ms + EPS) * norms[3:4, :]).astype(y_ref.dtype)

    chains = [chain(j) for chain in (gdn_chain, mlstm_chain, ret_chain, ssd_chain)
              for j in range(nb)]
    while chains:
        alive = []
        for ch in chains:
            try:
                next(ch)
                alive.append(ch)
            except StopIteration:
                pass
        chains = alive


def _prompt_mixer_call(proj, n_seq, n_chunks, tables, lp, consts):
    nb = PROMPT_SEQS
    L = PROMPT_CHUNK
    cos_t, sin_t = tables
    const2 = lambda g, c: (0, 0)
    const3 = lambda g, c: (0, 0, 0)
    st3 = lambda g, c: (g, 0, 0)
    proj_specs = [pl.BlockSpec((L, D_PROJ), functools.partial(
        lambda g, c, j: ((g * nb + j) * n_chunks + c, 0), j=j)) for j in range(nb)]
    sq = pl.BlockSpec((GROUP_W, GROUP_W), const2)
    in_specs = proj_specs + [
        pl.BlockSpec((L, GROUP_W), lambda g, c: (c, 0)),
        pl.BlockSpec((L, GROUP_W), lambda g, c: (c, 0)),
        pl.BlockSpec((SUBLANES, GROUP_W), const2),
        pl.BlockSpec((N_HEADS, GROUP_W), const2),
        pl.BlockSpec((CONV_W, CONV_CH), const2),
        pl.BlockSpec((CONV_W, CONV_CH), const2),
        pl.BlockSpec((1, CONV_CH), const2),
        sq, sq, sq, sq,
        pl.BlockSpec((5, L, GROUP_W), const3),
        pl.BlockSpec((N_LEVELS, L, GROUP_W), const3),
        pl.BlockSpec((L, 3 * L), const2),
        pl.BlockSpec((3, L, GROUP_W), const3),
        pl.BlockSpec((SUBLANES, GROUP_W), const2),
    ]
    big = pl.BlockSpec((nb, GROUP_W, GROUP_W), st3)
    hist = pl.BlockSpec((nb, SUBLANES, CONV_CH), st3)
    out_specs = [pl.BlockSpec((nb, L, D_MODEL), lambda g, c: (g, c, 0)),
                 big, big, big, pl.BlockSpec((nb, SUBLANES, GROUP_W), st3), big, hist, big, hist]
    sq_shape = jax.ShapeDtypeStruct((n_seq, GROUP_W, GROUP_W), F32)
    hist_shape = jax.ShapeDtypeStruct((n_seq, SUBLANES, CONV_CH), F32)
    out_shape = [jax.ShapeDtypeStruct((n_seq, n_chunks * L, D_MODEL), BF16),
                 sq_shape, sq_shape, sq_shape,
                 jax.ShapeDtypeStruct((n_seq, SUBLANES, GROUP_W), F32),
                 sq_shape, hist_shape, sq_shape, hist_shape]
    res = pl.pallas_call(
        functools.partial(_prompt_mixer_kernel, nb),
        grid=(n_seq // nb, n_chunks),
        in_specs=in_specs,
        out_specs=out_specs,
        out_shape=out_shape,
        scratch_shapes=[pltpu.VMEM((nb, L + SUBLANES, CONV_CH), F32),
                        pltpu.VMEM((nb, L + SUBLANES, CONV_CH), F32)],
        compiler_params=pltpu.CompilerParams(
            dimension_semantics=("arbitrary", "arbitrary"), vmem_limit_bytes=VMEM_LIMIT),
        name="mixer_prompt",
    )(*([proj] * nb), cos_t, sin_t, lp["gbx"], lp["norms"], lp["gcw"], lp["scw"], lp["scb"],
      consts["mbd_b"], consts["mbd_f"], consts["m2_b"], consts["m2_f"], consts["pm"],
      consts["lm"], consts["tril3"], lp["ret_tab"], lp["ret_sdec"])
    return res[0], res[1:]


def _prompt_consts():
    L = PROMPT_CHUNK
    r = jnp.arange(GROUP_W)
    mbd = (r[:, None] // HEAD_DIM == r[None, :] // HEAD_DIM)
    half = GROUP_W // SSM_GROUPS
    m2 = (r[:, None] // half == r[None, :] // half)
    l = jnp.arange(L)[:, None]
    s = (r % L)[None, :]
    causal = l >= s
    pm = jnp.stack([causal.astype(F32), (l > s).astype(F32), (l == s).astype(F32),
                    (l <= s).astype(F32), jnp.where(causal, 0.0, -jnp.inf).astype(F32)])
    lm = jnp.stack([((l >> (j + 1)) == (s >> (j + 1))) & (((l >> j) & 1) == 1) & (((s >> j) & 1) == 0)
                    for j in range(N_LEVELS)]).astype(F32)
    tril = (jnp.arange(L)[:, None] >= jnp.arange(L)[None, :]).astype(BF16)
    return dict(mbd_b=mbd.astype(BF16), mbd_f=mbd.astype(F32), m2_b=m2.astype(BF16),
                m2_f=m2.astype(F32), pm=pm, lm=lm, tril3=jnp.concatenate([tril] * 3, axis=1))


def _ret_tables():
    L = PROMPT_CHUNK
    log_gamma = jnp.log1p(-jnp.exp2(-5.0 - jnp.arange(N_HEADS, dtype=F32)))
    b = jnp.cumsum(jnp.broadcast_to(log_gamma[:, None], (N_HEADS, L)), axis=1)
    diff = b[:, :, None] - b[:, None, :]
    causal = jnp.arange(L)[:, None] >= jnp.arange(L)[None, :]
    dec = jnp.exp(jnp.where(causal[None], diff, -jnp.inf))
    dec_p = dec.transpose(1, 0, 2).reshape(L, GROUP_W)
    expand = lambda x: jnp.repeat(x.T, HEAD_DIM, axis=1)
    q_scale = expand(jnp.exp(b))
    k_scale = expand(jnp.exp(b[:, -1:] - b))
    s_dec = jnp.repeat(jnp.exp(b[:, -1]), HEAD_DIM)[None, :]
    return (jnp.stack([dec_p, q_scale, k_scale]),
            jnp.broadcast_to(s_dec, (SUBLANES, GROUP_W)))


def _ple(h3, p_ref, pn_ref, pg_ref, pp_ref):
    r = _rms(h3, pn_ref[...])
    gate = _sigmoid(jnp.dot(r.astype(BF16), pg_ref[...], preferred_element_type=F32))
    proj = jnp.dot(p_ref[...].astype(BF16), pp_ref[...], preferred_element_type=F32)
    return h3 + gate * proj


def _dense_kernel(h_ref, y_ref, wo_ref, nf_ref, wg_ref, wu_ref, wd_ref, p_ref, pn_ref, pg_ref,
                  pp_ref, o_ref):
    h2 = h_ref[...] + jnp.dot(y_ref[...], wo_ref[...], preferred_element_type=F32)
    c = _rms(h2, nf_ref[...]).astype(BF16)
    fw = FFN_DENSE // FFN_SPLIT
    h3 = h2
    for s in range(FFN_SPLIT):
        g = jnp.dot(c, wg_ref[:, fw * s:fw * (s + 1)], preferred_element_type=F32)
        u = jnp.dot(c, wu_ref[:, fw * s:fw * (s + 1)], preferred_element_type=F32)
        a = (_silu(g) * u).astype(BF16)
        h3 = h3 + jnp.dot(a, wd_ref[fw * s:fw * (s + 1), :], preferred_element_type=F32)
    o_ref[...] = _ple(h3, p_ref, pn_ref, pg_ref, pp_ref)


def _resident(shape):
    return pl.BlockSpec(shape, lambda i: (0,) * len(shape), pipeline_mode=pl.Buffered(1))


def _dense_call(h, y, p_all, layer, lw):
    t = h.shape[0]
    rows = lambda w: pl.BlockSpec((TM, w), lambda i: (i, 0))
    return pl.pallas_call(
        _dense_kernel,
        grid=(t // TM,),
        in_specs=[
            rows(D_MODEL), rows(D_MODEL),
            _resident((D_MODEL, D_MODEL)), _resident((1, D_MODEL)),
            _resident((D_MODEL, FFN_DENSE)), _resident((D_MODEL, FFN_DENSE)),
            _resident((FFN_DENSE, D_MODEL)),
            pl.BlockSpec((None, TM, PLE_DIM), lambda i: (layer, i, 0)),
            _resident((1, D_MODEL)), _resident((D_MODEL, D_MODEL)), _resident((PLE_DIM, D_MODEL)),
        ],
        out_specs=rows(D_MODEL),
        out_shape=jax.ShapeDtypeStruct((t, D_MODEL), F32),
        compiler_params=pltpu.CompilerParams(
            dimension_semantics=("arbitrary",), vmem_limit_bytes=VMEM_LIMIT),
        name="dense_ffn",
    )(h, y, lw["w_out"], lw["norm_ffn"], lw["wg"], lw["wu"], lw["wd"], p_all,
      lw["ple_norm"], lw["ple_w_gate"], lw["ple_w_proj"])


R_I0, R_I1, R_W0, R_W1 = 0, 1, 2, 3


def _moe_pre_kernel(h_ref, y_ref, wo_ref, nf_ref, rt_ref, h2_ref, c_ref, r_ref):
    h2 = h_ref[...] + jnp.dot(y_ref[...], wo_ref[...], preferred_element_type=F32)
    h2_ref[...] = h2
    c = _rms(h2, nf_ref[...])
    c_ref[...] = c.astype(BF16)
    lane = lax.broadcasted_iota(jnp.int32, (TM, LANES), 1)
    logits = jnp.where(lane < N_EXPERTS, _dot_hi(c, rt_ref[...]), -jnp.inf)
    m1 = jnp.max(logits, axis=1, keepdims=True)
    i1 = jnp.min(jnp.where(logits == m1, lane, LANES), axis=1, keepdims=True)
    rest = jnp.where(lane == i1, -jnp.inf, logits)
    m2 = jnp.max(rest, axis=1, keepdims=True)
    i2 = jnp.min(jnp.where(rest == m2, lane, LANES), axis=1, keepdims=True)
    e2 = jnp.exp(m2 - m1)
    den = 1.0 + e2
    r_ref[...] = jnp.where(lane == R_I0, i1.astype(F32),
                           jnp.where(lane == R_I1, i2.astype(F32),
                                     jnp.where(lane == R_W0, 1.0 / den,
                                               jnp.where(lane == R_W1, e2 / den, 0.0))))


def _moe_pre_call(h, y, lw):
    t = h.shape[0]
    rows = lambda w: pl.BlockSpec((TM, w), lambda i: (i, 0))
    return pl.pallas_call(
        _moe_pre_kernel,
        grid=(t // TM,),
        in_specs=[rows(D_MODEL), rows(D_MODEL), _resident((D_MODEL, D_MODEL)),
                  _resident((1, D_MODEL)), _resident((D_MODEL, LANES))],
        out_specs=[rows(D_MODEL), rows(D_MODEL), rows(LANES)],
        out_shape=[jax.ShapeDtypeStruct((t, D_MODEL), F32),
                   jax.ShapeDtypeStruct((t, D_MODEL), BF16),
                   jax.ShapeDtypeStruct((t, LANES), F32)],
        compiler_params=pltpu.CompilerParams(
            dimension_semantics=("arbitrary",), vmem_limit_bytes=VMEM_LIMIT),
        name="moe_pre",
    )(h, y, lw["w_out"], lw["norm_ffn"], lw["router"])


def _moe_ffn_kernel(be_ref, nb_ref, x_ref, wg_ref, wu_ref, wd_ref, o_ref):
    del be_ref
    i = pl.program_id(0)

    @pl.when(i < nb_ref[0])
    def _compute():
        x = x_ref[...]
        fw = FFN_EXPERT // FFN_SPLIT
        acc = jnp.zeros((TMM, D_MODEL), F32)
        for s in range(FFN_SPLIT):
            g = jnp.dot(x, wg_ref[0, :, fw * s:fw * (s + 1)], preferred_element_type=F32)
            u = jnp.dot(x, wu_ref[0, :, fw * s:fw * (s + 1)], preferred_element_type=F32)
            a = (_silu(g) * u).astype(BF16)
            acc = acc + jnp.dot(a, wd_ref[0, fw * s:fw * (s + 1), :], preferred_element_type=F32)
        o_ref[...] = acc.astype(o_ref.dtype)

    @pl.when(i >= nb_ref[0])
    def _skip():
        o_ref[...] = jnp.zeros_like(o_ref)


def _moe_ffn_call(xs, blk_e, nblk, lw):
    n_rows = xs.shape[0]
    wspec = lambda shape: pl.BlockSpec(shape, lambda i, be, nb: (be[i], 0, 0),
                                       pipeline_mode=pl.Buffered(1))
    grid_spec = pltpu.PrefetchScalarGridSpec(
        num_scalar_prefetch=2,
        grid=(n_rows // TMM,),
        in_specs=[
            pl.BlockSpec((TMM, D_MODEL), lambda i, be, nb: (i, 0)),
            wspec((1, D_MODEL, FFN_EXPERT)), wspec((1, D_MODEL, FFN_EXPERT)),
            wspec((1, FFN_EXPERT, D_MODEL)),
        ],
        out_specs=pl.BlockSpec((TMM, D_MODEL), lambda i, be, nb: (i, 0)),
    )
    return pl.pallas_call(
        _moe_ffn_kernel,
        grid_spec=grid_spec,
        out_shape=jax.ShapeDtypeStruct((n_rows, D_MODEL), BF16),
        compiler_params=pltpu.CompilerParams(
            dimension_semantics=("arbitrary",), vmem_limit_bytes=VMEM_LIMIT),
        name="moe_ffn",
    )(blk_e, nblk, xs, lw["wg"], lw["wu"], lw["wd"])


def _moe_post_kernel(final, h2_ref, y0_ref, y1_ref, r_ref, p_ref, pn_ref, pg_ref, pp_ref, nfin_ref,
                     o_ref):
    r = r_ref[...]
    w0 = r[:, R_W0:R_W0 + 1]
    w1 = r[:, R_W1:R_W1 + 1]
    h3 = h2_ref[...] + (w0 * y0_ref[...].astype(F32) + w1 * y1_ref[...].astype(F32))
    h4 = _ple(h3, p_ref, pn_ref, pg_ref, pp_ref)
    o_ref[...] = _rms(h4, nfin_ref[...]) if final else h4


def _moe_post_call(h2, y0, y1, route, p_all, layer, lw, norm_final, final):
    t = h2.shape[0]
    rows = lambda w: pl.BlockSpec((TM, w), lambda i: (i, 0))
    return pl.pallas_call(
        functools.partial(_moe_post_kernel, final),
        grid=(t // TM,),
        in_specs=[rows(D_MODEL), rows(D_MODEL), rows(D_MODEL), rows(LANES),
                  pl.BlockSpec((None, TM, PLE_DIM), lambda i: (layer, i, 0)),
                  _resident((1, D_MODEL)), _resident((D_MODEL, D_MODEL)),
                  _resident((PLE_DIM, D_MODEL)), _resident((1, D_MODEL))],
        out_specs=rows(D_MODEL),
        out_shape=jax.ShapeDtypeStruct((t, D_MODEL), F32),
        compiler_params=pltpu.CompilerParams(
            dimension_semantics=("arbitrary",), vmem_limit_bytes=VMEM_LIMIT),
        name="moe_post",
    )(h2, y0, y1, route, p_all, lw["ple_norm"], lw["ple_w_gate"], lw["ple_w_proj"], norm_final)


def _route_plan(route, n_rows):
    t = route.shape[0]
    e_flat = route[:, R_I0:R_I1 + 1].astype(jnp.int32).reshape(-1)
    onehot = (e_flat[:, None] == jnp.arange(N_EXPERTS, dtype=jnp.int32)[None, :]).astype(jnp.int32)
    csum = jnp.cumsum(onehot, axis=0)
    rank = jnp.sum((csum - onehot) * onehot, axis=1)
    counts = csum[-1]
    pcounts = ((counts + TMM - 1) // TMM) * TMM
    ends = jnp.cumsum(pcounts)
    starts = ends - pcounts
    pos = starts[e_flat] + rank
    src = jnp.zeros((n_rows,), jnp.int32).at[pos].set(jnp.arange(2 * t, dtype=jnp.int32) // 2)
    blk_start = jnp.arange(n_rows // TMM, dtype=jnp.int32) * TMM
    blk_e = jnp.minimum(jnp.sum((blk_start[:, None] >= ends[None, :]).astype(jnp.int32), axis=1),
                        N_EXPERTS - 1).astype(jnp.int32)
    nblk = (ends[-1:] // TMM).astype(jnp.int32)
    return src, pos.reshape(t, 2), blk_e, nblk


def _rope_tables(pos):
    half = HEAD_DIM // 2
    inv_freq = ROPE_BASE ** (-jnp.arange(half, dtype=F32) / half)
    ang = pos[:, None] * inv_freq[None, :]
    cos, sin = jnp.cos(ang), jnp.sin(ang)
    cos_h = jnp.concatenate([cos, cos], axis=-1)
    sin_h = jnp.concatenate([-sin, sin], axis=-1)
    return jnp.tile(cos_h, (1, N_HEADS)), jnp.tile(sin_h, (1, N_HEADS))


def _pad_lanes(x, width=LANES):
    return jnp.pad(x, [(0, 0)] * (x.ndim - 1) + [(0, width - x.shape[-1])])


def kernel(x_prompt, x_sample, state_ret, state_mlstm_c, state_mlstm_n, state_mlstm_m, state_gdn, state_gdn_conv, state_ssm, state_ssm_conv, p_prompt, p_sample, norm_mix, w_in, w_out, ret_norm, mlstm_i_bias, mlstm_f_bias, mlstm_norm, gdn_conv_w, gdn_a_log, gdn_dt_bias, gdn_norm, ssm_conv_w, ssm_conv_b, ssm_dt_bias, ssm_a_log, ssm_d, ssm_norm, norm_ffn, ffn_w_gate, ffn_w_up, ffn_w_down, moe_router, moe_w_gate, moe_w_up, moe_w_down, ple_w_proj, ple_norm, ple_w_gate, norm_final):
    bp, tp = x_prompt.shape[:2]
    bs, ts = x_sample.shape[:2]
    n_prompt = bp * tp
    n_all = n_prompt + bs * ts
    assert tp % PROMPT_CHUNK == 0 and n_all % TM == 0 and ts == SUBLANES and DEPTH % 2 == 0

    h = jnp.concatenate([x_prompt.reshape(n_prompt, D_MODEL), x_sample.reshape(bs * ts, D_MODEL)])
    h = h.astype(F32)
    p_all = jnp.concatenate([p_prompt.reshape(DEPTH, n_prompt, PLE_DIM),
                             p_sample.reshape(DEPTH, bs * ts, PLE_DIM)], axis=1)

    o_ml = 4 * GROUP_W
    o_gdn = o_ml + 4 * GROUP_W + 2 * N_HEADS
    o_ssm = o_gdn + 4 * GROUP_W + 2 * N_HEADS
    gate_cols = jnp.concatenate([
        w_in[:, :, o_ml + 4 * GROUP_W:o_gdn],
        w_in[:, :, o_gdn + 4 * GROUP_W:o_ssm],
        w_in[:, :, o_ssm + 4 * GROUP_W:],
    ], axis=-1)
    w_in_r = jnp.concatenate([
        w_in[:, :, 0:o_ml + 4 * GROUP_W],
        w_in[:, :, o_gdn:o_gdn + 4 * GROUP_W],
        w_in[:, :, o_ssm:o_ssm + 4 * GROUP_W],
        _pad_lanes(gate_cols),
        jnp.repeat(gate_cols, HEAD_DIM, axis=-1),
    ], axis=-1).astype(BF16)
    assert w_in_r.shape[-1] == D_PROJ

    zeros4 = jnp.zeros((DEPTH, N_HEADS), F32)
    gate_bias = jnp.concatenate([mlstm_i_bias, mlstm_f_bias, zeros4, gdn_dt_bias, ssm_dt_bias], axis=-1)
    gate_alog = jnp.concatenate([zeros4, zeros4, zeros4, gdn_a_log, ssm_a_log], axis=-1)
    log_gamma = jnp.log1p(-jnp.exp2(-5.0 - jnp.arange(N_HEADS, dtype=F32)))
    gp = jnp.stack([_pad_lanes(gate_bias.astype(F32)), _pad_lanes(gate_alog.astype(F32)),
                    jnp.broadcast_to(_pad_lanes(log_gamma)[None], (DEPTH, LANES))], axis=1)
    gp = jnp.pad(gp, ((0, 0), (0, SUBLANES - gp.shape[1]), (0, 0)))
    norms = jnp.stack([ret_norm, mlstm_norm, gdn_norm, ssm_norm], axis=1).astype(F32)
    ssm_d_w = jnp.repeat(ssm_d.astype(F32), HEAD_DIM, axis=-1)[:, None, :]
    gbx = jnp.repeat(jnp.stack([mlstm_i_bias, mlstm_f_bias, gdn_dt_bias, ssm_dt_bias, gdn_a_log,
                                ssm_a_log, ssm_d, zeros4], axis=1).astype(F32), HEAD_DIM, axis=-1)
    consts = _prompt_consts()
    ret_tab, ret_sdec = _ret_tables()

    tab_prompt = _rope_tables(jnp.arange(tp, dtype=F32))
    tab_sample = _rope_tables(PAST_LEN + jnp.arange(ts, dtype=F32))

    def conv_in(s):
        return jnp.pad(s.astype(F32), ((0, 0), (SUBLANES - (CONV_W - 1), 0), (0, 0)))

    def states_in(i, n_seq, given):
        if given:
            mc = jnp.concatenate([state_mlstm_c[i], state_mlstm_n[i][..., None]], axis=-1)
            return (state_ret[i].astype(F32), _pad_lanes(mc.astype(F32)),
                    _pad_lanes(state_mlstm_m[i].astype(F32))[:, None, :], state_gdn[i].astype(F32),
                    conv_in(state_gdn_conv[i]), state_ssm[i].astype(F32), conv_in(state_ssm_conv[i]))
        z = lambda *s: jnp.zeros((n_seq,) + s, F32)
        return (z(N_HEADS, HEAD_DIM, HEAD_DIM), z(N_HEADS, HEAD_DIM, LANES), z(1, LANES),
                z(N_HEADS, HEAD_DIM, HEAD_DIM), z(SUBLANES, CONV_CH),
                z(N_HEADS, SSM_STATE, HEAD_DIM), z(SUBLANES, CONV_CH))

    def states_out(st):
        n_ret, n_mc, n_mm, n_gdn, n_gconv, n_ssm, n_sconv = st
        keep = SUBLANES - (CONV_W - 1)
        return (n_ret, n_mc[..., :HEAD_DIM], n_mc[..., HEAD_DIM], n_mm[:, 0, :N_HEADS], n_gdn,
                n_gconv[:, keep:], n_ssm, n_sconv[:, keep:])

    def prompt_states_out(st):
        s_ret, s_mc, s_mn, s_mm, s_gdn, s_gconv, s_ssm, s_sconv = st
        keep = SUBLANES - (CONV_W - 1)
        hd = HEAD_DIM

        def diag(s):
            return jnp.stack([s[:, hd * hh:hd * (hh + 1), hd * hh:hd * (hh + 1)]
                              for hh in range(N_HEADS)], axis=1)

        n_vec = jnp.stack([s_mn[:, hd * hh:hd * (hh + 1), hd * hh] for hh in range(N_HEADS)], axis=1)
        per_group = N_HEADS // SSM_GROUPS
        ssm = jnp.stack([s_ssm[:, SSM_STATE * (hh // per_group):SSM_STATE * (hh // per_group + 1),
                               hd * hh:hd * (hh + 1)] for hh in range(N_HEADS)], axis=1)
        return (diag(s_ret), diag(s_mc), n_vec, s_mm[:, 0, ::hd], diag(s_gdn),
                s_gconv[:, keep:], ssm, s_sconv[:, keep:])

    n_moe_rows = 2 * n_all + N_EXPERTS * TMM
    new_prompt, new_sample = [], []
    for i in range(DEPTH):
        lp = dict(gp=gp[i], norms=norms[i], gcw=gdn_conv_w[i].astype(F32),
                  scw=ssm_conv_w[i].astype(F32), scb=ssm_conv_b[i].astype(F32)[None, :],
                  sd=ssm_d_w[i], gbx=gbx[i], ret_tab=ret_tab, ret_sdec=ret_sdec)
        proj = _proj_call(h, norm_mix[i].astype(F32)[None, :], w_in_r[i])
        y_p, st_p = _prompt_mixer_call(proj, bp, tp // PROMPT_CHUNK, tab_prompt, lp, consts)
        y_s, st_s = _mixer_call(proj, n_prompt, bs, 1, ts, tab_sample, lp, states_in(i, bs, True))
        y_all = jnp.concatenate([y_p.reshape(n_prompt, D_MODEL), y_s])
        new_prompt.append(prompt_states_out(st_p))
        new_sample.append(states_out(st_s))

        j = i // 2
        lw = dict(w_out=w_out[i].astype(BF16), norm_ffn=norm_ffn[i].astype(F32)[None, :],
                  ple_norm=ple_norm[i].astype(F32)[None, :], ple_w_gate=ple_w_gate[i].astype(BF16),
                  ple_w_proj=ple_w_proj[i].astype(BF16))
        if i % 2 == 0:
            lw.update(wg=ffn_w_gate[j].astype(BF16), wu=ffn_w_up[j].astype(BF16),
                      wd=ffn_w_down[j].astype(BF16))
            h = _dense_call(h, y_all, p_all, i, lw)
        else:
            lw.update(router=_pad_lanes(moe_router[j].astype(F32)), wg=moe_w_gate[j].astype(BF16),
                      wu=moe_w_up[j].astype(BF16), wd=moe_w_down[j].astype(BF16))
            h2, c, route = _moe_pre_call(h, y_all, lw)
            src, pos, blk_e, nblk = _route_plan(route, n_moe_rows)
            ys = _moe_ffn_call(jnp.take(c, src, axis=0), blk_e, nblk, lw)
            y0 = jnp.take(ys, pos[:, 0], axis=0)
            y1 = jnp.take(ys, pos[:, 1], axis=0)
            h = _moe_post_call(h2, y0, y1, route, p_all, i, lw, norm_final.astype(F32)[None, :],
                               final=(i == DEPTH - 1))

    y = h.astype(x_prompt.dtype)
    y_prompt = y[:n_prompt].reshape(bp, tp, D_MODEL)
    y_sample = y[n_prompt:].reshape(bs, ts, D_MODEL)
    stack = lambda lst: tuple(jnp.stack([l[k] for l in lst]) for k in range(8))
    return (y_prompt, y_sample) + stack(new_prompt) + stack(new_sample)
```

```python
import functools

import jax
import jax.numpy as jnp
from jax import lax
from jax.experimental import pallas as pl
from jax.experimental.pallas import tpu as pltpu

F32 = jnp.float32
BF16 = jnp.bfloat16
HIGHEST = lax.Precision.HIGHEST

D_MODEL = 1024
DEPTH = 4
PAST_LEN = 16384
N_HEADS = 4
HEAD_DIM = 64
GROUP_W = N_HEADS * HEAD_DIM
SSM_STATE = 128
SSM_GROUPS = 2
CONV_W = 4
CONV_CH = 3 * GROUP_W
ROPE_BASE = 10000.0
FFN_DENSE = 2816
N_EXPERTS = 8
FFN_EXPERT = 3584
PLE_DIM = 256
EPS = 1e-6

LANES = 128
SUBLANES = 8
VMEM_LIMIT = 56 * 1024 * 1024

C_RET = 0
C_ML = 1024
C_GDN = 2048
C_SSM = 3072
C_XMI = 4096
C_XMF = C_XMI + GROUP_W
C_XGB = C_XMF + GROUP_W
C_XGA = C_XGB + GROUP_W
C_XDT = C_XGA + GROUP_W
D_PROJ = C_XDT + GROUP_W

MIX_ROWS = 64
PROMPT_SEQS = 4
TM = 512
TMM = 512
FFN_SPLIT = 2


def _dot_hi(a, b):
    return jnp.dot(a, b, precision=HIGHEST, preferred_element_type=F32)


def _sigmoid(x):
    return 1.0 / (1.0 + jnp.exp(-x))


def _silu(x):
    return x * _sigmoid(x)


def _softplus(x):
    return jnp.maximum(x, 0.0) + jnp.log1p(jnp.exp(-jnp.abs(x)))


def _rms(x, g):
    return x * lax.rsqrt(jnp.mean(x * x, -1, keepdims=True) + EPS) * g


def _proj_kernel(h_ref, g_ref, w_ref, o_ref):
    a = _rms(h_ref[...], g_ref[...])
    o_ref[...] = jnp.dot(a.astype(BF16), w_ref[...], preferred_element_type=F32)


def _proj_call(h, g, w):
    t = h.shape[0]
    return pl.pallas_call(
        _proj_kernel,
        grid=(t // TM,),
        in_specs=[
            pl.BlockSpec((TM, D_MODEL), lambda i: (i, 0)),
            pl.BlockSpec((1, D_MODEL), lambda i: (0, 0)),
            pl.BlockSpec((D_MODEL, D_PROJ), lambda i: (0, 0), pipeline_mode=pl.Buffered(1)),
        ],
        out_specs=pl.BlockSpec((TM, D_PROJ), lambda i: (i, 0)),
        out_shape=jax.ShapeDtypeStruct((t, D_PROJ), F32),
        compiler_params=pltpu.CompilerParams(
            dimension_semantics=("arbitrary",), vmem_limit_bytes=VMEM_LIMIT),
        name="proj",
    )(h, g, w)


def _split3(x):
    hi = x.astype(BF16)
    r1 = x - hi.astype(F32)
    mid = r1.astype(BF16)
    lo = (r1 - mid.astype(F32)).astype(BF16)
    return hi, mid, lo


def _dot_tn(a, b):
    return lax.dot_general(a, b, (((0,), (0,)), ((), ())), preferred_element_type=F32)


def _mixer_kernel(nb, G, T, carry, *refs):
    R = G * T
    hd = HEAD_DIM
    n_levels = T.bit_length() - 1
    per_group = N_HEADS // SSM_GROUPS
    proj_refs = refs[:nb]
    (cos_ref, sin_ref, gb_ref, norms_ref, gcw_ref, scw_ref, scb_ref, mbd_b_ref, mbd_f_ref,
     m2_b_ref, m2_f_ref, pm_ref, lm_ref, tril3_ref, rt_ref, rs_ref) = refs[nb:nb + 16]
    rest = refs[nb + 16:]
    if carry:
        (y_ref, o_ret, o_mc, o_mn, o_mm, o_gdn, o_gconv, o_ssm, o_sconv, xg_ref, xs_ref) = rest
    else:
        (i_ret, i_mc, i_mn, i_mm, i_gdn, i_gconv, i_ssm, i_sconv,
         y_ref, o_ret, o_mc, o_mn, o_mm, o_gdn, o_gconv, o_ssm, o_sconv, xg_ref, xs_ref) = rest

    if carry:
        @pl.when(pl.program_id(1) == 0)
        def _init():
            for r in (o_ret, o_mc, o_mn, o_mm, o_gdn, o_ssm, xg_ref, xs_ref):
                r[...] = jnp.zeros_like(r)

    def to_bd(s4):
        rows = []
        for h in range(N_HEADS):
            parts = [jnp.zeros((hd, hd * h), F32)] if h else []
            parts.append(s4[h])
            if h < N_HEADS - 1:
                parts.append(jnp.zeros((hd, hd * (N_HEADS - 1 - h)), F32))
            rows.append(jnp.concatenate(parts, axis=1))
        return jnp.concatenate(rows, axis=0)

    def to_st(s4):
        rows = []
        for g in range(SSM_GROUPS):
            parts = [jnp.zeros((SSM_STATE, SSM_STATE * g), F32)] if g else []
            parts += [s4[per_group * g + k] for k in range(per_group)]
            if g < SSM_GROUPS - 1:
                parts.append(jnp.zeros((SSM_STATE, SSM_STATE * (SSM_GROUPS - 1 - g)), F32))
            rows.append(jnp.concatenate(parts, axis=1))
        return jnp.concatenate(rows, axis=0)

    out_refs = dict(ret=o_ret, mc=o_mc, mn=o_mn, gdn=o_gdn, ssm=o_ssm)
    if carry:
        def get(kind, j, i):
            return out_refs[kind][j]

        def put(kind, j, i, v):
            out_refs[kind][j] = v

        def get_m(j, i):
            return o_mm[j, 0:1, :]

        def put_m(j, i, row):
            o_mm[j] = jnp.broadcast_to(row, (SUBLANES, GROUP_W))
    else:
        in_refs = dict(ret=i_ret, mc=i_mc, gdn=i_gdn)

        def get(kind, j, i):
            if kind == "mn":
                return i_mn[i]
            if kind == "ssm":
                return to_st(i_ssm[i])
            return to_bd(in_refs[kind][i])

        def put(kind, j, i, v):
            if kind == "mn":
                o_mn[i] = v
            elif kind == "ssm":
                for h in range(N_HEADS):
                    g = h // per_group
                    o_ssm[i, h] = v[SSM_STATE * g:SSM_STATE * (g + 1), hd * h:hd * (h + 1)]
            else:
                for h in range(N_HEADS):
                    out_refs[kind][i, h] = v[hd * h:hd * (h + 1), hd * h:hd * (h + 1)]

        def get_m(j, i):
            return i_mm[i, 0:1, :]

        def put_m(j, i, row):
            o_mm[i] = jnp.broadcast_to(row, (SUBLANES, GROUP_W))

    def seq_rows(x, i):
        return x if G == 1 else x[T * i:T * (i + 1)]

    def seq_row(x, i):
        return x if x.shape[0] == 1 else x[T * i:T * i + 1]

    def seg_last(x):
        if G == 1:
            return x[R - 1:R]
        last = x.reshape(G, T, x.shape[1])[:, T - 1:T, :]
        return jnp.broadcast_to(last, (G, T, x.shape[1])).reshape(R, x.shape[1])

    def seg_max0(x):
        if G == 1:
            return jnp.max(x, axis=0, keepdims=True)
        mx = jnp.max(x.reshape(G, T, x.shape[1]), axis=1, keepdims=True)
        return jnp.broadcast_to(mx, (G, T, x.shape[1])).reshape(R, x.shape[1])

    def m_rows(j):
        if G == 1:
            return get_m(j, 0)
        return jnp.concatenate([jnp.broadcast_to(get_m(j, i), (T, GROUP_W)) for i in range(G)],
                               axis=0)

    mbd_b = mbd_b_ref[...]
    mbd_f = mbd_f_ref[...]
    m2_b = m2_b_ref[...]
    m2_f = m2_f_ref[...]
    eye_p = pm_ref[0]
    ut_p = pm_ref[1]
    ninf_p = pm_ref[2]
    tril3 = tril3_ref[...]
    gb = gb_ref[...]
    norms = norms_ref[...]
    cosv = cos_ref[...]
    sinv = sin_ref[...]
    lane_w = lax.broadcasted_iota(jnp.int32, (R, GROUP_W), 1)
    first_half = (lane_w % hd) < (hd // 2)
    ones_b = jnp.ones((R, GROUP_W), BF16)
    neg_a_gdn = -jnp.exp(gb[4:5, :])
    neg_a_ssm = -jnp.exp(gb[5:6, :])
    inv_hd = 1.0 / hd

    def bf(x):
        return x.astype(BF16)

    def bd(xb, mask):
        return jnp.concatenate([xb] * (GROUP_W // R), axis=0) * mask

    def mm(a, b):
        return jnp.dot(a, b, preferred_element_type=F32)

    def mm_nt(a, b):
        return lax.dot_general(a, b, (((1,), (1,)), ((), ())), preferred_element_type=F32)

    def mm_state(a, states):
        outs = [mm(seq_rows(a, i), bf(states[i])) for i in range(G)]
        return outs[0] if G == 1 else jnp.concatenate(outs, axis=0)

    def delta(a, b, i):
        return _dot_tn(seq_rows(a, i), seq_rows(b, i))

    def cumsum(g):
        return mm(tril3, jnp.concatenate(_split3(g), axis=0))

    def sum0(x):
        return jnp.sum(x, axis=0, keepdims=True)

    def segmax(x):
        parts = []
        for h in range(N_HEADS):
            mh = jnp.max(x[:, hd * h:hd * (h + 1)], axis=1, keepdims=True)
            parts.append(jnp.broadcast_to(mh, (R, hd)))
        return jnp.concatenate(parts, axis=1)

    def rope(x):
        swapped = jnp.where(first_half, pltpu.roll(x, GROUP_W - hd // 2, 1),
                            pltpu.roll(x, hd // 2, 1))
        return x * cosv + swapped * sinv

    def conv4(x_ref, j, x, w, hist_in, hist_out):
        if not carry:
            x_ref[j, :, 0:SUBLANES, :] = hist_in[...]
        x_ref[j, :, SUBLANES:SUBLANES + T, :] = x.reshape(G, T, CONV_CH)
        acc = w[0:1, :] * x_ref[j, :, pl.ds(SUBLANES - 3, T), :]
        for k in range(1, CONV_W):
            acc = acc + w[k:k + 1, :] * x_ref[j, :, pl.ds(SUBLANES - 3 + k, T), :]
        hist = x_ref[j, :, T:T + SUBLANES, :]
        if carry:
            hist_out[j] = hist[0]
            x_ref[j, :, 0:SUBLANES, :] = hist
        else:
            hist_out[...] = hist
        return acc.reshape(R, CONV_CH)


    def ret_chain(j):
        P = proj_refs[j]
        rq = rope(P[:, C_RET:C_RET + GROUP_W])
        rk = rope(P[:, C_RET + GROUP_W:C_RET + 2 * GROUP_W]) * hd ** -0.5
        rvb = bf(P[:, C_RET + 2 * GROUP_W:C_RET + 3 * GROUP_W])
        S = [get("ret", j, i) for i in range(G)]
        scores = mm_nt(bf(rq), bd(bf(rk), mbd_b)) * rt_ref[0]
        o_inter = mm_state(bf(rq * rt_ref[1]), S)
        yield
        ke = bf(rk * rt_ref[2])
        for i in range(G):
            put("ret", j, i, S[i] * rs_ref[0:1, :] + mbd_f * delta(ke, rvb, i))
        o = mm(bf(scores), bd(rvb, mbd_b)) + o_inter
        yield
        mean = mm(bf(o), mbd_b) * inv_hd
        yield
        xc = o - mean
        var = mm(bf(xc * xc), mbd_b) * inv_hd
        yield
        ro = xc * lax.rsqrt(var + EPS) * norms[0:1, :]
        rg = P[:, C_RET + 3 * GROUP_W:C_RET + 4 * GROUP_W]
        y_ref[j, :, 0:GROUP_W] = (_silu(rg) * ro).astype(y_ref.dtype)

    def mlstm_chain(j):
        P = proj_refs[j]
        mq = bf(P[:, C_ML:C_ML + GROUP_W])
        mk = P[:, C_ML + GROUP_W:C_ML + 2 * GROUP_W] * hd ** -0.5
        mvb = bf(P[:, C_ML + 2 * GROUP_W:C_ML + 3 * GROUP_W])
        i_x = P[:, C_XMI:C_XMI + GROUP_W] + gb[0:1, :]
        g_f = -_softplus(-(P[:, C_XMF:C_XMF + GROUP_W] + gb[1:2, :]))
        b_col = cumsum(g_f)
        qk = mm_nt(mq, bd(bf(mk), mbd_b))
        C = [get("mc", j, i) for i in range(G)]
        N = [get("mn", j, i) for i in range(G)]
        q_c = mm_state(mq, C)
        q_n = mm_state(mq, N)
        yield
        b_row = sum0(g_f * ut_p)
        i_row = sum0(i_x * eye_p)
        b_last = seg_last(b_col)
        m_prev = m_rows(j)
        dd = b_col - b_row + i_row
        inter = b_col + m_prev
        m_row = jnp.maximum(segmax(dd + ninf_p), inter)
        w_intra = jnp.exp(jnp.minimum(dd - m_row, ninf_p))
        w_state = jnp.exp(inter - m_row)
        sb = bf(qk * w_intra)
        num = mm(sb, bd(mvb, mbd_b)) + w_state * q_c
        den = mm(sb, mbd_b) + w_state * q_n
        g_end = b_last - b_col + i_x
        m_new = jnp.maximum(b_last + m_prev, seg_max0(g_end))
        w_c = jnp.exp(b_last + m_prev - m_new)
        kwb = bf(mk * jnp.exp(g_end - m_new))
        for i in range(G):
            wc_i = seq_row(w_c, i)
            put("mc", j, i, wc_i * C[i] + mbd_f * delta(kwb, mvb, i))
            put("mn", j, i, wc_i * N[i] + mbd_f * delta(kwb, ones_b, i))
            put_m(j, i, seq_row(m_new, i))
        yield
        hh = num / jnp.maximum(jnp.abs(den), jnp.exp(-m_row))
        mean = mm(bf(hh), mbd_b) * inv_hd
        yield
        xc = hh - mean
        var = mm(bf(xc * xc), mbd_b) * inv_hd
        yield
        mh = xc * lax.rsqrt(var + EPS) * norms[1:2, :]
        mo = P[:, C_ML + 3 * GROUP_W:C_ML + 4 * GROUP_W]
        y_ref[j, :, GROUP_W:2 * GROUP_W] = (_sigmoid(mo) * mh).astype(y_ref.dtype)

    def gdn_chain(j):
        P = proj_refs[j]
        conv = conv4(xg_ref, j, P[:, C_GDN:C_GDN + CONV_CH], gcw_ref[...],
                     None if carry else i_gconv, o_gconv)
        act = _silu(conv)
        gq, gk, gv = act[:, 0:GROUP_W], act[:, GROUP_W:2 * GROUP_W], act[:, 2 * GROUP_W:]
        q_ss = mm(bf(gq * gq), mbd_b)
        k_ss = mm(bf(gk * gk), mbd_b)
        g_g = neg_a_gdn * _softplus(P[:, C_XGA:C_XGA + GROUP_W] + gb[2:3, :])
        b_col = cumsum(g_g)
        yield
        qn = gq * lax.rsqrt(q_ss + EPS) * hd ** -0.5
        kn = gk * lax.rsqrt(k_ss + EPS)
        beta = _sigmoid(P[:, C_XGB:C_XGB + GROUP_W])
        b_row = sum0(g_g * ut_p)
        b_last = seg_last(b_col)
        e_col = jnp.exp(b_col)
        decay = jnp.exp(jnp.minimum(b_col - b_row, ninf_p))
        kb = kn * beta
        bd_k = bd(bf(kn), mbd_b)
        kk = mm_nt(bf(kb), bd_k)
        qk = mm_nt(bf(qn), bd_k)
        S = [get("gdn", j, i) for i in range(G)]
        q_s = mm_state(bf(qn), S)
        yield
        a_low = kk * decay
        attn = qk * decay
        t_inv = eye_p - a_low * lm_ref[0]
        for lev in range(1, n_levels):
            m1 = mm(bf(a_low * lm_ref[lev]), bd(bf(t_inv), mbd_b))
            yield
            t_inv = t_inv - mm(bf(t_inv), bd(bf(m1), mbd_b))
            yield
        tb = bf(t_inv)
        u = mm(tb, bd(bf(gv * beta), mbd_b))
        w = mm(tb, bd(bf(kb * e_col), mbd_b))
        yield
        v_new = u - mm_state(bf(w), S)
        yield
        vnb = bf(v_new)
        o = e_col * q_s + mm(bf(attn), bd(vnb, mbd_b))
        keb = bf(kn * jnp.exp(b_last - b_col))
        s_dec = jnp.exp(b_last)
        for i in range(G):
            put("gdn", j, i, S[i] * seq_row(s_dec, i) + mbd_f * delta(keb, vnb, i))
        yield
        o_ss = mm(bf(o * o), mbd_b)
        yield
        go = o * lax.rsqrt(o_ss * inv_hd + EPS) * norms[2:3, :]
        gz = P[:, C_GDN + CONV_CH:C_GDN + CONV_CH + GROUP_W]
        y_ref[j, :, 2 * GROUP_W:3 * GROUP_W] = (go * _silu(gz)).astype(y_ref.dtype)

    def ssd_chain(j):
        P = proj_refs[j]
        conv = conv4(xs_ref, j, P[:, C_SSM + GROUP_W:C_SSM + GROUP_W + CONV_CH], scw_ref[...],
                     None if carry else i_sconv, o_sconv) + scb_ref[...]
        act = _silu(conv)
        sx = act[:, 0:GROUP_W]
        s_b = bf(act[:, GROUP_W:2 * GROUP_W])
        s_c = bf(act[:, 2 * GROUP_W:])
        dt = _softplus(P[:, C_XDT:C_XDT + GROUP_W] + gb[3:4, :])
        g_s = dt * neg_a_ssm
        b_col = cumsum(g_s)
        cb = mm_nt(s_c, bd(s_b, m2_b))
        S = [get("ssm", j, i) for i in range(G)]
        c_s = mm_state(s_c, S)
        yield
        b_row = sum0(g_s * ut_p)
        b_last = seg_last(b_col)
        decay = jnp.exp(jnp.minimum(b_col - b_row, ninf_p))
        v = sx * dt
        o = mm(bf(cb * decay), bd(bf(v), mbd_b)) + jnp.exp(b_col) * c_s
        v_end = bf(v * jnp.exp(b_last - b_col))
        s_dec = jnp.exp(b_last)
        for i in range(G):
            put("ssm", j, i, S[i] * seq_row(s_dec, i) + m2_f * delta(s_b, v_end, i))
        yield
        sz = P[:, C_SSM:C_SSM + GROUP_W]
        yv = (o + gb[6:7, :] * sx) * _silu(sz)
        ms = mm(bf(yv * yv), m2_b) * (1.0 / (GROUP_W // SSM_GROUPS))
        yield
        y_ref[j, :, 3 * GROUP_W:] = (yv * lax.rsqrt(ms + EPS) * norms[3:4, :]).astype(y_ref.dtype)

    chains = [chain(j) for chain in (gdn_chain, mlstm_chain, ret_chain, ssd_chain)
              for j in range(nb)]
    while chains:
        alive = []
        for ch in chains:
            try:
                next(ch)
                alive.append(ch)
            except StopIteration:
                pass
        chains = alive


def _mixer_consts(G, T):
    R = G * T
    n_levels = T.bit_length() - 1
    r = jnp.arange(GROUP_W)
    mbd = (r[:, None] // HEAD_DIM == r[None, :] // HEAD_DIM)
    half = GROUP_W // SSM_GROUPS
    m2 = (r[:, None] // half == r[None, :] // half)
    row = jnp.arange(R)[:, None]
    col = (r % R)[None, :]
    same = (row // T) == (col // T)
    tl, ts = row % T, col % T
    causal = same & (tl >= ts)
    pm = jnp.stack([(row == col).astype(F32), (same & (tl <= ts)).astype(F32),
                    jnp.where(causal, 0.0, -jnp.inf).astype(F32)])
    lm = jnp.stack([same & ((tl >> (j + 1)) == (ts >> (j + 1))) & (((tl >> j) & 1) == 1)
                    & (((ts >> j) & 1) == 0) for j in range(n_levels)]).astype(F32)
    tril = causal[:, :R].astype(BF16)
    return dict(mbd_b=mbd.astype(BF16), mbd_f=mbd.astype(F32), m2_b=m2.astype(BF16),
                m2_f=m2.astype(F32), pm=pm, lm=lm, tril3=jnp.concatenate([tril] * 3, axis=1))


def _ret_tables(G, T):
    R = G * T
    log_gamma = jnp.log1p(-jnp.exp2(-5.0 - jnp.arange(N_HEADS, dtype=F32)))
    b = jnp.cumsum(jnp.broadcast_to(log_gamma[:, None], (N_HEADS, T)), axis=1)
    b_r = jnp.tile(b, (1, G))
    pos = jnp.arange(R)
    causal = ((pos[:, None] // T) == (pos[None, :] // T)) & ((pos[:, None] % T) >= (pos[None, :] % T))
    diff = b_r[:, :, None] - b_r[:, None, :]
    dec = jnp.exp(jnp.where(causal[None], diff, -jnp.inf))
    dec_p = dec.transpose(1, 0, 2).reshape(R, GROUP_W)
    expand = lambda x: jnp.repeat(x.T, HEAD_DIM, axis=1)
    q_scale = expand(jnp.exp(b_r))
    k_scale = expand(jnp.exp(b[:, -1:] - b_r))
    s_dec = jnp.repeat(jnp.exp(b[:, -1]), HEAD_DIM)[None, :]
    return (jnp.stack([dec_p, q_scale, k_scale]),
            jnp.broadcast_to(s_dec, (SUBLANES, GROUP_W)))


def _mixer_call(proj, row0, n_seq, seq_len, nb, G, carry, tables, lp, consts, ret_tabs,
                states=()):
    R = MIX_ROWS
    T = R // G
    n_chunks = seq_len // T
    n_levels = T.bit_length() - 1
    blk0 = row0 // R
    cos_t, sin_t = tables
    const2 = lambda g, c: (0, 0)
    const3 = lambda g, c: (0, 0, 0)
    st3 = lambda g, c: (g, 0, 0)
    st4 = lambda g, c: (g, 0, 0, 0)
    if carry:
        proj_specs = [pl.BlockSpec((R, D_PROJ), functools.partial(
            lambda g, c, j: (blk0 + (g * nb + j) * n_chunks + c, 0), j=j)) for j in range(nb)]
        tab_map = lambda g, c: (c, 0)
        n_steps = n_seq // nb
    else:
        assert nb == 1 and n_chunks == 1
        proj_specs = [pl.BlockSpec((R, D_PROJ), lambda g, c: (blk0 + g, 0))]
        tab_map = const2
        n_steps = n_seq // G
    sq = pl.BlockSpec((GROUP_W, GROUP_W), const2)
    in_specs = proj_specs + [
        pl.BlockSpec((R, GROUP_W), tab_map),
        pl.BlockSpec((R, GROUP_W), tab_map),
        pl.BlockSpec((SUBLANES, GROUP_W), const2),
        pl.BlockSpec((N_HEADS, GROUP_W), const2),
        pl.BlockSpec((CONV_W, CONV_CH), const2),
        pl.BlockSpec((CONV_W, CONV_CH), const2),
        pl.BlockSpec((1, CONV_CH), const2),
        sq, sq, sq, sq,
        pl.BlockSpec((3, R, GROUP_W), const3),
        pl.BlockSpec((n_levels, R, GROUP_W), const3),
        pl.BlockSpec((R, 3 * R), const2),
        pl.BlockSpec((3, R, GROUP_W), const3),
        pl.BlockSpec((SUBLANES, GROUP_W), const2),
    ]
    ns = nb if carry else G
    big = pl.BlockSpec((ns, GROUP_W, GROUP_W), st3)
    hist = pl.BlockSpec((ns, SUBLANES, CONV_CH), st3)
    mrow = pl.BlockSpec((ns, SUBLANES, GROUP_W), st3)
    sq_shape = jax.ShapeDtypeStruct((n_seq, GROUP_W, GROUP_W), F32)
    hist_shape = jax.ShapeDtypeStruct((n_seq, SUBLANES, CONV_CH), F32)
    m_shape = jax.ShapeDtypeStruct((n_seq, SUBLANES, GROUP_W), F32)
    if carry:
        state_specs = [big, big, big, mrow, big, hist, big, hist]
        state_shapes = [sq_shape, sq_shape, sq_shape, m_shape, sq_shape, hist_shape, sq_shape,
                        hist_shape]
        y_spec = pl.BlockSpec((nb, R, D_MODEL), lambda g, c: (g, c, 0))
        y_shape = jax.ShapeDtypeStruct((n_seq, seq_len, D_MODEL), BF16)
        state_in_specs = []
    else:
        nat = pl.BlockSpec((G, N_HEADS, HEAD_DIM, HEAD_DIM), st4)
        nat_ssm = pl.BlockSpec((G, N_HEADS, SSM_STATE, HEAD_DIM), st4)
        nat_shape = jax.ShapeDtypeStruct((n_seq, N_HEADS, HEAD_DIM, HEAD_DIM), F32)
        ssm_shape = jax.ShapeDtypeStruct((n_seq, N_HEADS, SSM_STATE, HEAD_DIM), F32)
        state_specs = [nat, nat, big, mrow, nat, hist, nat_ssm, hist]
        state_shapes = [nat_shape, nat_shape, sq_shape, m_shape, nat_shape, hist_shape, ssm_shape,
                        hist_shape]
        y_spec = pl.BlockSpec((1, R, D_MODEL), lambda g, c: (g, 0, 0))
        y_shape = jax.ShapeDtypeStruct((n_seq // G, R, D_MODEL), BF16)
        state_in_specs = state_specs
    res = pl.pallas_call(
        functools.partial(_mixer_kernel, nb, G, T, carry),
        grid=(n_steps, n_chunks),
        in_specs=in_specs + state_in_specs,
        out_specs=[y_spec] + state_specs,
        out_shape=[y_shape] + state_shapes,
        scratch_shapes=[pltpu.VMEM((nb, G, T + SUBLANES, CONV_CH), F32),
                        pltpu.VMEM((nb, G, T + SUBLANES, CONV_CH), F32)],
        compiler_params=pltpu.CompilerParams(
            dimension_semantics=("arbitrary", "arbitrary"), vmem_limit_bytes=VMEM_LIMIT),
        name="mixer_prompt" if carry else "mixer_sample",
    )(*([proj] * nb), cos_t, sin_t, lp["gbx"], lp["norms"], lp["gcw"], lp["scw"], lp["scb"],
      consts["mbd_b"], consts["mbd_f"], consts["m2_b"], consts["m2_f"], consts["pm"],
      consts["lm"], consts["tril3"], ret_tabs[0], ret_tabs[1], *states)
    return res[0], res[1:]


def _ple(h3, p_ref, pn_ref, pg_ref, pp_ref):
    r = _rms(h3, pn_ref[...])
    gate = _sigmoid(jnp.dot(r.astype(BF16), pg_ref[...], preferred_element_type=F32))
    proj = jnp.dot(p_ref[...].astype(BF16), pp_ref[...], preferred_element_type=F32)
    return h3 + gate * proj


def _dense_kernel(h_ref, y_ref, wo_ref, nf_ref, wg_ref, wu_ref, wd_ref, p_ref, pn_ref, pg_ref,
                  pp_ref, o_ref):
    h2 = h_ref[...] + jnp.dot(y_ref[...], wo_ref[...], preferred_element_type=F32)
    c = _rms(h2, nf_ref[...]).astype(BF16)
    fw = FFN_DENSE // FFN_SPLIT
    h3 = h2
    for s in range(FFN_SPLIT):
        g = jnp.dot(c, wg_ref[:, fw * s:fw * (s + 1)], preferred_element_type=F32)
        u = jnp.dot(c, wu_ref[:, fw * s:fw * (s + 1)], preferred_element_type=F32)
        a = (_silu(g) * u).astype(BF16)
        h3 = h3 + jnp.dot(a, wd_ref[fw * s:fw * (s + 1), :], preferred_element_type=F32)
    o_ref[...] = _ple(h3, p_ref, pn_ref, pg_ref, pp_ref)


def _resident(shape):
    return pl.BlockSpec(shape, lambda i: (0,) * len(shape), pipeline_mode=pl.Buffered(1))


def _dense_call(h, y, p_all, layer, lw):
    t = h.shape[0]
    rows = lambda w: pl.BlockSpec((TM, w), lambda i: (i, 0))
    return pl.pallas_call(
        _dense_kernel,
        grid=(t // TM,),
        in_specs=[
            rows(D_MODEL), rows(D_MODEL),
            _resident((D_MODEL, D_MODEL)), _resident((1, D_MODEL)),
            _resident((D_MODEL, FFN_DENSE)), _resident((D_MODEL, FFN_DENSE)),
            _resident((FFN_DENSE, D_MODEL)),
            pl.BlockSpec((None, TM, PLE_DIM), lambda i: (layer, i, 0)),
            _resident((1, D_MODEL)), _resident((D_MODEL, D_MODEL)), _resident((PLE_DIM, D_MODEL)),
        ],
        out_specs=rows(D_MODEL),
        out_shape=jax.ShapeDtypeStruct((t, D_MODEL), F32),
        compiler_params=pltpu.CompilerParams(
            dimension_semantics=("arbitrary",), vmem_limit_bytes=VMEM_LIMIT),
        name="dense_ffn",
    )(h, y, lw["w_out"], lw["norm_ffn"], lw["wg"], lw["wu"], lw["wd"], p_all,
      lw["ple_norm"], lw["ple_w_gate"], lw["ple_w_proj"])


R_I0, R_I1, R_W0, R_W1 = 0, 1, 2, 3


def _moe_pre_kernel(h_ref, y_ref, wo_ref, nf_ref, rt_ref, h2_ref, c_ref, r_ref):
    h2 = h_ref[...] + jnp.dot(y_ref[...], wo_ref[...], preferred_element_type=F32)
    h2_ref[...] = h2
    c = _rms(h2, nf_ref[...])
    c_ref[...] = c.astype(BF16)
    lane = lax.broadcasted_iota(jnp.int32, (TM, LANES), 1)
    logits = jnp.where(lane < N_EXPERTS, _dot_hi(c, rt_ref[...]), -jnp.inf)
    m1 = jnp.max(logits, axis=1, keepdims=True)
    i1 = jnp.min(jnp.where(logits == m1, lane, LANES), axis=1, keepdims=True)
    rest = jnp.where(lane == i1, -jnp.inf, logits)
    m2 = jnp.max(rest, axis=1, keepdims=True)
    i2 = jnp.min(jnp.where(rest == m2, lane, LANES), axis=1, keepdims=True)
    e2 = jnp.exp(m2 - m1)
    den = 1.0 + e2
    r_ref[...] = jnp.where(lane == R_I0, i1.astype(F32),
                           jnp.where(lane == R_I1, i2.astype(F32),
                                     jnp.where(lane == R_W0, 1.0 / den,
                                               jnp.where(lane == R_W1, e2 / den, 0.0))))


def _moe_pre_call(h, y, lw):
    t = h.shape[0]
    rows = lambda w: pl.BlockSpec((TM, w), lambda i: (i, 0))
    return pl.pallas_call(
        _moe_pre_kernel,
        grid=(t // TM,),
        in_specs=[rows(D_MODEL), rows(D_MODEL), _resident((D_MODEL, D_MODEL)),
                  _resident((1, D_MODEL)), _resident((D_MODEL, LANES))],
        out_specs=[rows(D_MODEL), rows(D_MODEL), rows(LANES)],
        out_shape=[jax.ShapeDtypeStruct((t, D_MODEL), F32),
                   jax.ShapeDtypeStruct((t, D_MODEL), BF16),
                   jax.ShapeDtypeStruct((t, LANES), F32)],
        compiler_params=pltpu.CompilerParams(
            dimension_semantics=("arbitrary",), vmem_limit_bytes=VMEM_LIMIT),
        name="moe_pre",
    )(h, y, lw["w_out"], lw["norm_ffn"], lw["router"])


def _moe_ffn_kernel(be_ref, nb_ref, x_ref, wg_ref, wu_ref, wd_ref, o_ref):
    del be_ref
    i = pl.program_id(0)

    @pl.when(i < nb_ref[0])
    def _compute():
        x = x_ref[...]
        fw = FFN_EXPERT // FFN_SPLIT
        acc = jnp.zeros((TMM, D_MODEL), F32)
        for s in range(FFN_SPLIT):
            g = jnp.dot(x, wg_ref[0, :, fw * s:fw * (s + 1)], preferred_element_type=F32)
            u = jnp.dot(x, wu_ref[0, :, fw * s:fw * (s + 1)], preferred_element_type=F32)
            a = (_silu(g) * u).astype(BF16)
            acc = acc + jnp.dot(a, wd_ref[0, fw * s:fw * (s + 1), :], preferred_element_type=F32)
        o_ref[...] = acc.astype(o_ref.dtype)

    @pl.when(i >= nb_ref[0])
    def _skip():
        o_ref[...] = jnp.zeros_like(o_ref)


def _moe_ffn_call(xs, blk_e, nblk, lw):
    n_rows = xs.shape[0]
    wspec = lambda shape: pl.BlockSpec(shape, lambda i, be, nb: (be[i], 0, 0),
                                       pipeline_mode=pl.Buffered(1))
    grid_spec = pltpu.PrefetchScalarGridSpec(
        num_scalar_prefetch=2,
        grid=(n_rows // TMM,),
        in_specs=[
            pl.BlockSpec((TMM, D_MODEL), lambda i, be, nb: (i, 0)),
            wspec((1, D_MODEL, FFN_EXPERT)), wspec((1, D_MODEL, FFN_EXPERT)),
            wspec((1, FFN_EXPERT, D_MODEL)),
        ],
        out_specs=pl.BlockSpec((TMM, D_MODEL), lambda i, be, nb: (i, 0)),
    )
    return pl.pallas_call(
        _moe_ffn_kernel,
        grid_spec=grid_spec,
        out_shape=jax.ShapeDtypeStruct((n_rows, D_MODEL), BF16),
        compiler_params=pltpu.CompilerParams(
            dimension_semantics=("arbitrary",), vmem_limit_bytes=VMEM_LIMIT),
        name="moe_ffn",
    )(blk_e, nblk, xs, lw["wg"], lw["wu"], lw["wd"])


def _moe_post_kernel(final, h2_ref, y0_ref, y1_ref, r_ref, p_ref, pn_ref, pg_ref, pp_ref, nfin_ref,
                     o_ref):
    r = r_ref[...]
    w0 = r[:, R_W0:R_W0 + 1]
    w1 = r[:, R_W1:R_W1 + 1]
    h3 = h2_ref[...] + (w0 * y0_ref[...].astype(F32) + w1 * y1_ref[...].astype(F32))
    h4 = _ple(h3, p_ref, pn_ref, pg_ref, pp_ref)
    o_ref[...] = _rms(h4, nfin_ref[...]) if final else h4


def _moe_post_call(h2, y0, y1, route, p_all, layer, lw, norm_final, final):
    t = h2.shape[0]
    rows = lambda w: pl.BlockSpec((TM, w), lambda i: (i, 0))
    return pl.pallas_call(
        functools.partial(_moe_post_kernel, final),
        grid=(t // TM,),
        in_specs=[rows(D_MODEL), rows(D_MODEL), rows(D_MODEL), rows(LANES),
                  pl.BlockSpec((None, TM, PLE_DIM), lambda i: (layer, i, 0)),
                  _resident((1, D_MODEL)), _resident((D_MODEL, D_MODEL)),
                  _resident((PLE_DIM, D_MODEL)), _resident((1, D_MODEL))],
        out_specs=rows(D_MODEL),
        out_shape=jax.ShapeDtypeStruct((t, D_MODEL), F32),
        compiler_params=pltpu.CompilerParams(
            dimension_semantics=("arbitrary",), vmem_limit_bytes=VMEM_LIMIT),
        name="moe_post",
    )(h2, y0, y1, route, p_all, lw["ple_norm"], lw["ple_w_gate"], lw["ple_w_proj"], norm_final)


def _route_plan(route, n_rows):
    t = route.shape[0]
    e_flat = route[:, R_I0:R_I1 + 1].astype(jnp.int32).reshape(-1)
    onehot = (e_flat[:, None] == jnp.arange(N_EXPERTS, dtype=jnp.int32)[None, :]).astype(jnp.int32)
    csum = jnp.cumsum(onehot, axis=0)
    rank = jnp.sum((csum - onehot) * onehot, axis=1)
    counts = csum[-1]
    pcounts = ((counts + TMM - 1) // TMM) * TMM
    ends = jnp.cumsum(pcounts)
    starts = ends - pcounts
    pos = starts[e_flat] + rank
    src = (jnp.arange(n_rows, dtype=jnp.int32) % t).at[pos].set(
        jnp.arange(2 * t, dtype=jnp.int32) // 2)
    blk_start = jnp.arange(n_rows // TMM, dtype=jnp.int32) * TMM
    blk_e = jnp.minimum(jnp.sum((blk_start[:, None] >= ends[None, :]).astype(jnp.int32), axis=1),
                        N_EXPERTS - 1).astype(jnp.int32)
    nblk = (ends[-1:] // TMM).astype(jnp.int32)
    return src, pos.reshape(t, 2), blk_e, nblk


def _rope_tables(pos):
    half = HEAD_DIM // 2
    inv_freq = ROPE_BASE ** (-jnp.arange(half, dtype=F32) / half)
    ang = pos[:, None] * inv_freq[None, :]
    cos, sin = jnp.cos(ang), jnp.sin(ang)
    cos_h = jnp.concatenate([cos, cos], axis=-1)
    sin_h = jnp.concatenate([-sin, sin], axis=-1)
    return jnp.tile(cos_h, (1, N_HEADS)), jnp.tile(sin_h, (1, N_HEADS))


def _pad_lanes(x, width=LANES):
    return jnp.pad(x, [(0, 0)] * (x.ndim - 1) + [(0, width - x.shape[-1])])


def kernel(x_prompt, x_sample, state_ret, state_mlstm_c, state_mlstm_n, state_mlstm_m, state_gdn, state_gdn_conv, state_ssm, state_ssm_conv, p_prompt, p_sample, norm_mix, w_in, w_out, ret_norm, mlstm_i_bias, mlstm_f_bias, mlstm_norm, gdn_conv_w, gdn_a_log, gdn_dt_bias, gdn_norm, ssm_conv_w, ssm_conv_b, ssm_dt_bias, ssm_a_log, ssm_d, ssm_norm, norm_ffn, ffn_w_gate, ffn_w_up, ffn_w_down, moe_router, moe_w_gate, moe_w_up, moe_w_down, ple_w_proj, ple_norm, ple_w_gate, norm_final):
    bp, tp = x_prompt.shape[:2]
    bs, ts = x_sample.shape[:2]
    n_prompt = bp * tp
    n_all = n_prompt + bs * ts
    sample_seqs = MIX_ROWS // ts
    assert tp % MIX_ROWS == 0 and bp % PROMPT_SEQS == 0 and n_all % TM == 0
    assert MIX_ROWS % ts == 0 and bs % sample_seqs == 0 and ts >= CONV_W - 1 and DEPTH % 2 == 0
    hd = HEAD_DIM

    h = jnp.concatenate([x_prompt.reshape(n_prompt, D_MODEL), x_sample.reshape(bs * ts, D_MODEL)])
    h = h.astype(F32)
    p_all = jnp.concatenate([p_prompt.reshape(DEPTH, n_prompt, PLE_DIM),
                             p_sample.reshape(DEPTH, bs * ts, PLE_DIM)], axis=1)

    o_ml = 4 * GROUP_W
    o_gdn = o_ml + 4 * GROUP_W + 2 * N_HEADS
    o_ssm = o_gdn + 4 * GROUP_W + 2 * N_HEADS
    gate_cols = jnp.concatenate([
        w_in[:, :, o_ml + 4 * GROUP_W:o_gdn],
        w_in[:, :, o_gdn + 4 * GROUP_W:o_ssm],
        w_in[:, :, o_ssm + 4 * GROUP_W:],
    ], axis=-1)
    w_in_r = jnp.concatenate([
        w_in[:, :, 0:o_ml + 4 * GROUP_W],
        w_in[:, :, o_gdn:o_gdn + 4 * GROUP_W],
        w_in[:, :, o_ssm:o_ssm + 4 * GROUP_W],
        jnp.repeat(gate_cols, hd, axis=-1),
    ], axis=-1).astype(BF16)
    assert w_in_r.shape[-1] == D_PROJ

    zeros4 = jnp.zeros((DEPTH, N_HEADS), F32)
    norms = jnp.stack([ret_norm, mlstm_norm, gdn_norm, ssm_norm], axis=1).astype(F32)
    gbx = jnp.repeat(jnp.stack([mlstm_i_bias, mlstm_f_bias, gdn_dt_bias, ssm_dt_bias, gdn_a_log,
                                ssm_a_log, ssm_d, zeros4], axis=1).astype(F32), hd, axis=-1)

    consts_p = _mixer_consts(1, MIX_ROWS)
    consts_s = _mixer_consts(sample_seqs, ts)
    ret_tabs_p = _ret_tables(1, MIX_ROWS)
    ret_tabs_s = _ret_tables(sample_seqs, ts)
    tab_prompt = _rope_tables(jnp.arange(tp, dtype=F32))
    tab_sample = tuple(jnp.tile(t, (sample_seqs, 1))
                       for t in _rope_tables(PAST_LEN + jnp.arange(ts, dtype=F32)))

    keep = SUBLANES - (CONV_W - 1)
    eye_h = jnp.eye(N_HEADS, dtype=F32)

    def conv_in(s):
        return jnp.pad(s.astype(F32), ((0, 0), (keep, 0), (0, 0)))

    def sample_states_in(i):
        n_bd = (state_mlstm_n[i].astype(F32)[:, :, :, None, None]
                * eye_h[None, :, None, :, None])
        n_bd = jnp.broadcast_to(n_bd, (bs, N_HEADS, hd, N_HEADS, hd)).reshape(bs, GROUP_W, GROUP_W)
        m_x = jnp.repeat(state_mlstm_m[i].astype(F32), hd, axis=-1)[:, None, :]
        return (state_ret[i].astype(F32), state_mlstm_c[i].astype(F32), n_bd,
                jnp.broadcast_to(m_x, (bs, SUBLANES, GROUP_W)), state_gdn[i].astype(F32),
                conv_in(state_gdn_conv[i]), state_ssm[i].astype(F32), conv_in(state_ssm_conv[i]))

    def n_vec(s_mn):
        return jnp.stack([s_mn[:, hd * hh:hd * (hh + 1), hd * hh] for hh in range(N_HEADS)], axis=1)

    def sample_states_out(st):
        s_ret, s_mc, s_mn, s_mm, s_gdn, s_gconv, s_ssm, s_sconv = st
        return (s_ret, s_mc, n_vec(s_mn), s_mm[:, 0, ::hd], s_gdn, s_gconv[:, keep:], s_ssm,
                s_sconv[:, keep:])

    def prompt_states_out(st):
        s_ret, s_mc, s_mn, s_mm, s_gdn, s_gconv, s_ssm, s_sconv = st

        def diag(s):
            return jnp.stack([s[:, hd * hh:hd * (hh + 1), hd * hh:hd * (hh + 1)]
                              for hh in range(N_HEADS)], axis=1)

        per_group = N_HEADS // SSM_GROUPS
        ssm = jnp.stack([s_ssm[:, SSM_STATE * (hh // per_group):SSM_STATE * (hh // per_group + 1),
                               hd * hh:hd * (hh + 1)] for hh in range(N_HEADS)], axis=1)
        return (diag(s_ret), diag(s_mc), n_vec(s_mn), s_mm[:, 0, ::hd], diag(s_gdn),
                s_gconv[:, keep:], ssm, s_sconv[:, keep:])

    n_moe_rows = 2 * n_all + N_EXPERTS * TMM
    new_prompt, new_sample = [], []
    for i in range(DEPTH):
        lp = dict(norms=norms[i], gcw=gdn_conv_w[i].astype(F32), scw=ssm_conv_w[i].astype(F32),
                  scb=ssm_conv_b[i].astype(F32)[None, :], gbx=gbx[i])
        proj = _proj_call(h, norm_mix[i].astype(F32)[None, :], w_in_r[i])
        y_p, st_p = _mixer_call(proj, 0, bp, tp, PROMPT_SEQS, 1, True, tab_prompt, lp, consts_p,
                                ret_tabs_p)
        y_s, st_s = _mixer_call(proj, n_prompt, bs, ts, 1, sample_seqs, False, tab_sample, lp,
                                consts_s, ret_tabs_s, sample_states_in(i))
        y_all = jnp.concatenate([y_p.reshape(n_prompt, D_MODEL), y_s.reshape(bs * ts, D_MODEL)])
        new_prompt.append(prompt_states_out(st_p))
        new_sample.append(sample_states_out(st_s))

        j = i // 2
        lw = dict(w_out=w_out[i].astype(BF16), norm_ffn=norm_ffn[i].astype(F32)[None, :],
                  ple_norm=ple_norm[i].astype(F32)[None, :], ple_w_gate=ple_w_gate[i].astype(BF16),
                  ple_w_proj=ple_w_proj[i].astype(BF16))
        if i % 2 == 0:
            lw.update(wg=ffn_w_gate[j].astype(BF16), wu=ffn_w_up[j].astype(BF16),
                      wd=ffn_w_down[j].astype(BF16))
            h = _dense_call(h, y_all, p_all, i, lw)
        else:
            lw.update(router=_pad_lanes(moe_router[j].astype(F32)), wg=moe_w_gate[j].astype(BF16),
                      wu=moe_w_up[j].astype(BF16), wd=moe_w_down[j].astype(BF16))
            h2, c, route = _moe_pre_call(h, y_all, lw)
            src, pos, blk_e, nblk = _route_plan(route, n_moe_rows)
            ys = _moe_ffn_call(jnp.take(c, src, axis=0, mode="clip"), blk_e, nblk, lw)
            y0 = jnp.take(ys, pos[:, 0], axis=0, mode="clip")
            y1 = jnp.take(ys, pos[:, 1], axis=0, mode="clip")
            h = _moe_post_call(h2, y0, y1, route, p_all, i, lw, norm_final.astype(F32)[None, :],
                               final=(i == DEPTH - 1))

    y = h.astype(x_prompt.dtype)
    y_prompt = y[:n_prompt].reshape(bp, tp, D_MODEL)
    y_sample = y[n_prompt:].reshape(bs, ts, D_MODEL)
    stack = lambda lst: tuple(jnp.stack([l[k] for l in lst]) for k in range(8))
    return (y_prompt, y_sample) + stack(new_prompt) + stack(new_sample)
```

```python
import functools

import jax
import jax.numpy as jnp
from jax import lax
from jax.experimental import pallas as pl
from jax.experimental.pallas import tpu as pltpu

F32 = jnp.float32
BF16 = jnp.bfloat16
HIGHEST = lax.Precision.HIGHEST

D_MODEL = 1024
DEPTH = 4
PAST_LEN = 16384
N_HEADS = 4
HEAD_DIM = 64
GROUP_W = N_HEADS * HEAD_DIM
SSM_STATE = 128
SSM_GROUPS = 2
CONV_W = 4
CONV_CH = 3 * GROUP_W
ROPE_BASE = 10000.0
FFN_DENSE = 2816
N_EXPERTS = 8
FFN_EXPERT = 3584
PLE_DIM = 256
EPS = 1e-6

LANES = 128
SUBLANES = 8
VMEM_LIMIT = 56 * 1024 * 1024

C_RET = 0
C_ML = 1024
C_GDN = 2048
C_SSM = 3072
C_XMI = 4096
C_XMF = C_XMI + GROUP_W
C_XGB = C_XMF + GROUP_W
C_XGA = C_XGB + GROUP_W
C_XDT = C_XGA + GROUP_W
D_PROJ = C_XDT + GROUP_W

MIX_ROWS = 64
PROMPT_SEQS = 4
TM = 512
TMM = 512
FFN_SPLIT = 2


def _dot_hi(a, b):
    return jnp.dot(a, b, precision=HIGHEST, preferred_element_type=F32)


def _sigmoid(x):
    return 1.0 / (1.0 + jnp.exp(-x))


def _silu(x):
    return x * _sigmoid(x)


def _softplus(x):
    return jnp.maximum(x, 0.0) + jnp.log1p(jnp.exp(-jnp.abs(x)))


def _rms(x, g):
    return x * lax.rsqrt(jnp.mean(x * x, -1, keepdims=True) + EPS) * g


def _proj_kernel(h_ref, g_ref, w_ref, o_ref):
    a = _rms(h_ref[...], g_ref[...])
    o_ref[...] = jnp.dot(a.astype(BF16), w_ref[...], preferred_element_type=F32)


def _proj_call(h, g, w):
    t = h.shape[0]
    return pl.pallas_call(
        _proj_kernel,
        grid=(t // TM,),
        in_specs=[
            pl.BlockSpec((TM, D_MODEL), lambda i: (i, 0)),
            pl.BlockSpec((1, D_MODEL), lambda i: (0, 0)),
            pl.BlockSpec((D_MODEL, D_PROJ), lambda i: (0, 0), pipeline_mode=pl.Buffered(1)),
        ],
        out_specs=pl.BlockSpec((TM, D_PROJ), lambda i: (i, 0)),
        out_shape=jax.ShapeDtypeStruct((t, D_PROJ), F32),
        compiler_params=pltpu.CompilerParams(
            dimension_semantics=("arbitrary",), vmem_limit_bytes=VMEM_LIMIT),
        name="proj",
    )(h, g, w)


def _split3(x):
    hi = x.astype(BF16)
    r1 = x - hi.astype(F32)
    mid = r1.astype(BF16)
    lo = (r1 - mid.astype(F32)).astype(BF16)
    return hi, mid, lo


def _dot_tn(a, b):
    return lax.dot_general(a, b, (((0,), (0,)), ((), ())), preferred_element_type=F32)


def _mixer_kernel(nb, G, T, carry, *refs):
    R = G * T
    hd = HEAD_DIM
    n_levels = T.bit_length() - 1
    per_group = N_HEADS // SSM_GROUPS
    proj_refs = refs[:nb]
    (cos_ref, sin_ref, gb_ref, norms_ref, gcw_ref, scw_ref, scb_ref, mbd_b_ref, mbd_f_ref,
     m2_b_ref, m2_f_ref, pm_ref, lm_ref, tril3_ref, rt_ref, rs_ref) = refs[nb:nb + 16]
    rest = refs[nb + 16:]
    if carry:
        (y_ref, o_ret, o_mc, o_mn, o_mm, o_gdn, o_gconv, o_ssm, o_sconv, xg_ref, xs_ref) = rest
    else:
        (i_ret, i_mc, i_mn, i_mm, i_gdn, i_gconv, i_ssm, i_sconv,
         y_ref, o_ret, o_mc, o_mn, o_mm, o_gdn, o_gconv, o_ssm, o_sconv, xg_ref, xs_ref) = rest

    if carry:
        @pl.when(pl.program_id(1) == 0)
        def _init():
            for r in (o_ret, o_mc, o_mn, o_mm, o_gdn, o_ssm, xg_ref, xs_ref):
                r[...] = jnp.zeros_like(r)

    def to_bd(s4):
        rows = []
        for h in range(N_HEADS):
            parts = [jnp.zeros((hd, hd * h), F32)] if h else []
            parts.append(s4[h])
            if h < N_HEADS - 1:
                parts.append(jnp.zeros((hd, hd * (N_HEADS - 1 - h)), F32))
            rows.append(jnp.concatenate(parts, axis=1))
        return jnp.concatenate(rows, axis=0)

    def to_st(s4):
        rows = []
        for g in range(SSM_GROUPS):
            parts = [jnp.zeros((SSM_STATE, SSM_STATE * g), F32)] if g else []
            parts += [s4[per_group * g + k] for k in range(per_group)]
            if g < SSM_GROUPS - 1:
                parts.append(jnp.zeros((SSM_STATE, SSM_STATE * (SSM_GROUPS - 1 - g)), F32))
            rows.append(jnp.concatenate(parts, axis=1))
        return jnp.concatenate(rows, axis=0)

    out_refs = dict(ret=o_ret, mc=o_mc, mn=o_mn, gdn=o_gdn, ssm=o_ssm)

    def live_blocks(kind):
        if kind == "ssm":
            return [(SSM_STATE * g, SSM_STATE, SSM_STATE * g, SSM_STATE) for g in range(SSM_GROUPS)]
        return [(hd * h, hd, hd * h, hd) for h in range(N_HEADS)]

    if carry:
        def get(kind, j, i):
            return out_refs[kind][j]

        def update(kind, j, i, dec, dlt):
            ref = out_refs[kind]
            for r0, nr, c0, nc in live_blocks(kind):
                ref[j, r0:r0 + nr, c0:c0 + nc] = (ref[j, r0:r0 + nr, c0:c0 + nc] * dec[:, c0:c0 + nc]
                                                  + dlt[r0:r0 + nr, c0:c0 + nc])

        def get_m(j, i):
            return o_mm[j, 0:1, :]

        def put_m(j, i, row):
            o_mm[j] = jnp.broadcast_to(row, (SUBLANES, GROUP_W))
    else:
        in_refs = dict(ret=i_ret, mc=i_mc, gdn=i_gdn, ssm=i_ssm)

        def get(kind, j, i):
            if kind == "mn":
                return i_mn[i]
            if kind == "ssm":
                return to_st(i_ssm[i])
            return to_bd(in_refs[kind][i])

        def update(kind, j, i, dec, dlt):
            if kind == "mn":
                o_mn[i] = i_mn[i] * dec + mbd_f * dlt
                return
            for h in range(N_HEADS):
                r0, nr = (SSM_STATE * (h // per_group), SSM_STATE) if kind == "ssm" else (hd * h, hd)
                out_refs[kind][i, h] = (in_refs[kind][i, h] * dec[:, hd * h:hd * (h + 1)]
                                        + dlt[r0:r0 + nr, hd * h:hd * (h + 1)])

        def get_m(j, i):
            return i_mm[i, 0:1, :]

        def put_m(j, i, row):
            o_mm[i] = jnp.broadcast_to(row, (SUBLANES, GROUP_W))

    def seq_rows(x, i):
        return x if G == 1 else x[T * i:T * (i + 1)]

    def seq_row(x, i):
        return x if x.shape[0] == 1 else x[T * i:T * i + 1]

    def seg_last(x):
        if G == 1:
            return x[R - 1:R]
        last = x.reshape(G, T, x.shape[1])[:, T - 1:T, :]
        return jnp.broadcast_to(last, (G, T, x.shape[1])).reshape(R, x.shape[1])

    def seg_max0(x):
        if G == 1:
            return jnp.max(x, axis=0, keepdims=True)
        mx = jnp.max(x.reshape(G, T, x.shape[1]), axis=1, keepdims=True)
        return jnp.broadcast_to(mx, (G, T, x.shape[1])).reshape(R, x.shape[1])

    def m_rows(j):
        if G == 1:
            return get_m(j, 0)
        return jnp.concatenate([jnp.broadcast_to(get_m(j, i), (T, GROUP_W)) for i in range(G)],
                               axis=0)

    mbd_b = mbd_b_ref[...]
    mbd_f = mbd_f_ref[...]
    m2_b = m2_b_ref[...]
    m2_f = m2_f_ref[...]
    eye_p = pm_ref[0]
    ut_p = pm_ref[1]
    ninf_p = pm_ref[2]
    tril3 = tril3_ref[...]
    gb = gb_ref[...]
    norms = norms_ref[...]
    cosv = cos_ref[...]
    sinv = sin_ref[...]
    lane_w = lax.broadcasted_iota(jnp.int32, (R, GROUP_W), 1)
    first_half = (lane_w % hd) < (hd // 2)
    ones_b = jnp.ones((R, GROUP_W), BF16)
    neg_a_gdn = -jnp.exp(gb[4:5, :])
    neg_a_ssm = -jnp.exp(gb[5:6, :])
    inv_hd = 1.0 / hd

    def bf(x):
        return x.astype(BF16)

    def bd(xb, mask):
        return jnp.concatenate([xb] * (GROUP_W // R), axis=0) * mask

    def mm(a, b):
        return jnp.dot(a, b, preferred_element_type=F32)

    def mm_nt(a, b):
        return lax.dot_general(a, b, (((1,), (1,)), ((), ())), preferred_element_type=F32)

    def mm_state(a, states):
        outs = [mm(seq_rows(a, i), bf(states[i])) for i in range(G)]
        return outs[0] if G == 1 else jnp.concatenate(outs, axis=0)

    def delta(a, b, i):
        return _dot_tn(seq_rows(a, i), seq_rows(b, i))

    def cumsum(g):
        return mm(tril3, jnp.concatenate(_split3(g), axis=0))

    def sum0(x):
        return jnp.sum(x, axis=0, keepdims=True)

    def segmax(x):
        parts = []
        for h in range(N_HEADS):
            mh = jnp.max(x[:, hd * h:hd * (h + 1)], axis=1, keepdims=True)
            parts.append(jnp.broadcast_to(mh, (R, hd)))
        return jnp.concatenate(parts, axis=1)

    def rope(x):
        swapped = jnp.where(first_half, pltpu.roll(x, GROUP_W - hd // 2, 1),
                            pltpu.roll(x, hd // 2, 1))
        return x * cosv + swapped * sinv

    def conv4(x_ref, j, x, w, hist_in, hist_out):
        if not carry:
            x_ref[j, :, 0:SUBLANES, :] = hist_in[...]
        x_ref[j, :, SUBLANES:SUBLANES + T, :] = x.reshape(G, T, CONV_CH)
        acc = w[0:1, :] * x_ref[j, :, pl.ds(SUBLANES - 3, T), :]
        for k in range(1, CONV_W):
            acc = acc + w[k:k + 1, :] * x_ref[j, :, pl.ds(SUBLANES - 3 + k, T), :]
        hist = x_ref[j, :, T:T + SUBLANES, :]
        if carry:
            hist_out[j] = hist[0]
            x_ref[j, :, 0:SUBLANES, :] = hist
        else:
            hist_out[...] = hist
        return acc.reshape(R, CONV_CH)


    def ret_chain(j):
        P = proj_refs[j]
        rq = rope(P[:, C_RET:C_RET + GROUP_W])
        rk = rope(P[:, C_RET + GROUP_W:C_RET + 2 * GROUP_W])
        rvb = bf(P[:, C_RET + 2 * GROUP_W:C_RET + 3 * GROUP_W])
        S = [get("ret", j, i) for i in range(G)]
        scores = mm_nt(bf(rq), bd(bf(rk), mbd_b)) * rt_ref[0]
        o_inter = mm_state(bf(rq * rt_ref[1]), S)
        yield
        ke = bf(rk * rt_ref[2])
        for i in range(G):
            update("ret", j, i, rs_ref[0:1, :], delta(ke, rvb, i))
        o = mm(bf(scores), bd(rvb, mbd_b)) + o_inter
        yield
        mean = mm(bf(o), mbd_b) * inv_hd
        yield
        xc = o - mean
        var = mm(bf(xc * xc), mbd_b) * inv_hd
        yield
        ro = xc * lax.rsqrt(var + EPS) * norms[0:1, :]
        rg = P[:, C_RET + 3 * GROUP_W:C_RET + 4 * GROUP_W]
        y_ref[j, :, 0:GROUP_W] = (_silu(rg) * ro).astype(y_ref.dtype)

    def mlstm_chain(j):
        P = proj_refs[j]
        mq = bf(P[:, C_ML:C_ML + GROUP_W])
        mk = P[:, C_ML + GROUP_W:C_ML + 2 * GROUP_W]
        mvb = bf(P[:, C_ML + 2 * GROUP_W:C_ML + 3 * GROUP_W])
        i_x = P[:, C_XMI:C_XMI + GROUP_W] + gb[0:1, :]
        g_f = -_softplus(-(P[:, C_XMF:C_XMF + GROUP_W] + gb[1:2, :]))
        b_col = cumsum(g_f)
        qk = mm_nt(mq, bd(bf(mk), mbd_b))
        C = [get("mc", j, i) for i in range(G)]
        N = [get("mn", j, i) for i in range(G)]
        q_c = mm_state(mq, C)
        q_n = mm_state(mq, N)
        yield
        b_row = sum0(g_f * ut_p)
        i_row = sum0(i_x * eye_p)
        b_last = seg_last(b_col)
        m_prev = m_rows(j)
        dd = b_col - b_row + i_row
        inter = b_col + m_prev
        m_row = jnp.maximum(segmax(dd + ninf_p), inter)
        w_intra = jnp.exp(jnp.minimum(dd - m_row, ninf_p))
        w_state = jnp.exp(inter - m_row)
        sb = bf(qk * w_intra)
        num = mm(sb, bd(mvb, mbd_b)) + w_state * q_c
        den = mm(sb, mbd_b) + w_state * q_n
        g_end = b_last - b_col + i_x
        m_new = jnp.maximum(b_last + m_prev, seg_max0(g_end))
        w_c = jnp.exp(b_last + m_prev - m_new)
        kwb = bf(mk * jnp.exp(g_end - m_new))
        for i in range(G):
            wc_i = seq_row(w_c, i)
            update("mc", j, i, wc_i, delta(kwb, mvb, i))
            update("mn", j, i, wc_i, delta(kwb, ones_b, i))
            put_m(j, i, seq_row(m_new, i))
        yield
        hh = num / jnp.maximum(jnp.abs(den), jnp.exp(-m_row))
        mean = mm(bf(hh), mbd_b) * inv_hd
        yield
        xc = hh - mean
        var = mm(bf(xc * xc), mbd_b) * inv_hd
        yield
        mh = xc * lax.rsqrt(var + EPS) * norms[1:2, :]
        mo = P[:, C_ML + 3 * GROUP_W:C_ML + 4 * GROUP_W]
        y_ref[j, :, GROUP_W:2 * GROUP_W] = (_sigmoid(mo) * mh).astype(y_ref.dtype)

    def gdn_chain(j):
        P = proj_refs[j]
        conv = conv4(xg_ref, j, P[:, C_GDN:C_GDN + CONV_CH], gcw_ref[...],
                     None if carry else i_gconv, o_gconv)
        act = _silu(conv)
        gq, gk, gv = act[:, 0:GROUP_W], act[:, GROUP_W:2 * GROUP_W], act[:, 2 * GROUP_W:]
        q_ss = mm(bf(gq * gq), mbd_b)
        k_ss = mm(bf(gk * gk), mbd_b)
        g_g = neg_a_gdn * _softplus(P[:, C_XGA:C_XGA + GROUP_W] + gb[2:3, :])
        b_col = cumsum(g_g)
        yield
        qn = gq * lax.rsqrt(q_ss + EPS) * hd ** -0.5
        kn = gk * lax.rsqrt(k_ss + EPS)
        beta = _sigmoid(P[:, C_XGB:C_XGB + GROUP_W])
        b_row = sum0(g_g * ut_p)
        b_last = seg_last(b_col)
        e_col = jnp.exp(b_col)
        decay = jnp.exp(jnp.minimum(b_col - b_row, ninf_p))
        kb = kn * beta
        bd_k = bd(bf(kn), mbd_b)
        kk = mm_nt(bf(kb), bd_k)
        qk = mm_nt(bf(qn), bd_k)
        S = [get("gdn", j, i) for i in range(G)]
        q_s = mm_state(bf(qn), S)
        yield
        a_low = kk * decay
        attn = qk * decay
        t_inv = eye_p - a_low * lm_ref[0]
        for lev in range(1, n_levels):
            m1 = mm(bf(a_low * lm_ref[lev]), bd(bf(t_inv), mbd_b))
            yield
            t_inv = t_inv - mm(bf(t_inv), bd(bf(m1), mbd_b))
            yield
        tb = bf(t_inv)
        u = mm(tb, bd(bf(gv * beta), mbd_b))
        w = mm(tb, bd(bf(kb * e_col), mbd_b))
        yield
        v_new = u - mm_state(bf(w), S)
        yield
        vnb = bf(v_new)
        o = e_col * q_s + mm(bf(attn), bd(vnb, mbd_b))
        keb = bf(kn * jnp.exp(b_last - b_col))
        s_dec = jnp.exp(b_last)
        for i in range(G):
            update("gdn", j, i, seq_row(s_dec, i), delta(keb, vnb, i))
        yield
        o_ss = mm(bf(o * o), mbd_b)
        yield
        go = o * lax.rsqrt(o_ss * inv_hd + EPS) * norms[2:3, :]
        gz = P[:, C_GDN + CONV_CH:C_GDN + CONV_CH + GROUP_W]
        y_ref[j, :, 2 * GROUP_W:3 * GROUP_W] = (go * _silu(gz)).astype(y_ref.dtype)

    def ssd_chain(j):
        P = proj_refs[j]
        conv = conv4(xs_ref, j, P[:, C_SSM + GROUP_W:C_SSM + GROUP_W + CONV_CH], scw_ref[...],
                     None if carry else i_sconv, o_sconv) + scb_ref[...]
        act = _silu(conv)
        sx = act[:, 0:GROUP_W]
        s_b = bf(act[:, GROUP_W:2 * GROUP_W])
        s_c = bf(act[:, 2 * GROUP_W:])
        dt = _softplus(P[:, C_XDT:C_XDT + GROUP_W] + gb[3:4, :])
        g_s = dt * neg_a_ssm
        b_col = cumsum(g_s)
        cb = mm_nt(s_c, bd(s_b, m2_b))
        S = [get("ssm", j, i) for i in range(G)]
        c_s = mm_state(s_c, S)
        yield
        b_row = sum0(g_s * ut_p)
        b_last = seg_last(b_col)
        decay = jnp.exp(jnp.minimum(b_col - b_row, ninf_p))
        v = sx * dt
        o = mm(bf(cb * decay), bd(bf(v), mbd_b)) + jnp.exp(b_col) * c_s
        v_end = bf(v * jnp.exp(b_last - b_col))
        s_dec = jnp.exp(b_last)
        for i in range(G):
            update("ssm", j, i, seq_row(s_dec, i), delta(s_b, v_end, i))
        yield
        sz = P[:, C_SSM:C_SSM + GROUP_W]
        yv = (o + gb[6:7, :] * sx) * _silu(sz)
        ms = mm(bf(yv * yv), m2_b) * (1.0 / (GROUP_W // SSM_GROUPS))
        yield
        y_ref[j, :, 3 * GROUP_W:] = (yv * lax.rsqrt(ms + EPS) * norms[3:4, :]).astype(y_ref.dtype)

    chains = [chain(j) for chain in (gdn_chain, mlstm_chain, ret_chain, ssd_chain)
              for j in range(nb)]
    while chains:
        alive = []
        for ch in chains:
            try:
                next(ch)
                alive.append(ch)
            except StopIteration:
                pass
        chains = alive


def _mixer_consts(G, T):
    R = G * T
    n_levels = T.bit_length() - 1
    r = jnp.arange(GROUP_W)
    mbd = (r[:, None] // HEAD_DIM == r[None, :] // HEAD_DIM)
    half = GROUP_W // SSM_GROUPS
    m2 = (r[:, None] // half == r[None, :] // half)
    row = jnp.arange(R)[:, None]
    col = (r % R)[None, :]
    same = (row // T) == (col // T)
    tl, ts = row % T, col % T
    causal = same & (tl >= ts)
    pm = jnp.stack([(row == col).astype(F32), (same & (tl <= ts)).astype(F32),
                    jnp.where(causal, 0.0, -jnp.inf).astype(F32)])
    lm = jnp.stack([same & ((tl >> (j + 1)) == (ts >> (j + 1))) & (((tl >> j) & 1) == 1)
                    & (((ts >> j) & 1) == 0) for j in range(n_levels)]).astype(F32)
    tril = causal[:, :R].astype(BF16)
    return dict(mbd_b=mbd.astype(BF16), mbd_f=mbd.astype(F32), m2_b=m2.astype(BF16),
                m2_f=m2.astype(F32), pm=pm, lm=lm, tril3=jnp.concatenate([tril] * 3, axis=1))


def _ret_tables(G, T):
    R = G * T
    log_gamma = jnp.log1p(-jnp.exp2(-5.0 - jnp.arange(N_HEADS, dtype=F32)))
    b = jnp.cumsum(jnp.broadcast_to(log_gamma[:, None], (N_HEADS, T)), axis=1)
    b_r = jnp.tile(b, (1, G))
    pos = jnp.arange(R)
    causal = ((pos[:, None] // T) == (pos[None, :] // T)) & ((pos[:, None] % T) >= (pos[None, :] % T))
    diff = b_r[:, :, None] - b_r[:, None, :]
    dec = jnp.exp(jnp.where(causal[None], diff, -jnp.inf))
    dec_p = dec.transpose(1, 0, 2).reshape(R, GROUP_W)
    expand = lambda x: jnp.repeat(x.T, HEAD_DIM, axis=1)
    q_scale = expand(jnp.exp(b_r))
    k_scale = expand(jnp.exp(b[:, -1:] - b_r))
    s_dec = jnp.repeat(jnp.exp(b[:, -1]), HEAD_DIM)[None, :]
    return (jnp.stack([dec_p, q_scale, k_scale]),
            jnp.broadcast_to(s_dec, (SUBLANES, GROUP_W)))


def _mixer_call(proj, row0, n_seq, seq_len, nb, G, carry, tables, lp, consts, ret_tabs,
                states=()):
    R = MIX_ROWS
    T = R // G
    n_chunks = seq_len // T
    n_levels = T.bit_length() - 1
    blk0 = row0 // R
    cos_t, sin_t = tables
    const2 = lambda g, c: (0, 0)
    const3 = lambda g, c: (0, 0, 0)
    st3 = lambda g, c: (g, 0, 0)
    st4 = lambda g, c: (g, 0, 0, 0)
    if carry:
        proj_specs = [pl.BlockSpec((R, D_PROJ), functools.partial(
            lambda g, c, j: (blk0 + (g * nb + j) * n_chunks + c, 0), j=j)) for j in range(nb)]
        tab_map = lambda g, c: (c, 0)
        n_steps = n_seq // nb
    else:
        assert nb == 1 and n_chunks == 1
        proj_specs = [pl.BlockSpec((R, D_PROJ), lambda g, c: (blk0 + g, 0))]
        tab_map = const2
        n_steps = n_seq // G
    sq = pl.BlockSpec((GROUP_W, GROUP_W), const2)
    in_specs = proj_specs + [
        pl.BlockSpec((R, GROUP_W), tab_map),
        pl.BlockSpec((R, GROUP_W), tab_map),
        pl.BlockSpec((SUBLANES, GROUP_W), const2),
        pl.BlockSpec((N_HEADS, GROUP_W), const2),
        pl.BlockSpec((CONV_W, CONV_CH), const2),
        pl.BlockSpec((CONV_W, CONV_CH), const2),
        pl.BlockSpec((1, CONV_CH), const2),
        sq, sq, sq, sq,
        pl.BlockSpec((3, R, GROUP_W), const3),
        pl.BlockSpec((n_levels, R, GROUP_W), const3),
        pl.BlockSpec((R, 3 * R), const2),
        pl.BlockSpec((3, R, GROUP_W), const3),
        pl.BlockSpec((SUBLANES, GROUP_W), const2),
    ]
    ns = nb if carry else G
    big = pl.BlockSpec((ns, GROUP_W, GROUP_W), st3)
    hist = pl.BlockSpec((ns, SUBLANES, CONV_CH), st3)
    mrow = pl.BlockSpec((ns, SUBLANES, GROUP_W), st3)
    sq_shape = jax.ShapeDtypeStruct((n_seq, GROUP_W, GROUP_W), F32)
    hist_shape = jax.ShapeDtypeStruct((n_seq, SUBLANES, CONV_CH), F32)
    m_shape = jax.ShapeDtypeStruct((n_seq, SUBLANES, GROUP_W), F32)
    if carry:
        state_specs = [big, big, big, mrow, big, hist, big, hist]
        state_shapes = [sq_shape, sq_shape, sq_shape, m_shape, sq_shape, hist_shape, sq_shape,
                        hist_shape]
        y_spec = pl.BlockSpec((nb, R, D_MODEL), lambda g, c: (g, c, 0))
        y_shape = jax.ShapeDtypeStruct((n_seq, seq_len, D_MODEL), BF16)
        state_in_specs = []
    else:
        nat = pl.BlockSpec((G, N_HEADS, HEAD_DIM, HEAD_DIM), st4)
        nat_ssm = pl.BlockSpec((G, N_HEADS, SSM_STATE, HEAD_DIM), st4)
        nat_shape = jax.ShapeDtypeStruct((n_seq, N_HEADS, HEAD_DIM, HEAD_DIM), F32)
        ssm_shape = jax.ShapeDtypeStruct((n_seq, N_HEADS, SSM_STATE, HEAD_DIM), F32)
        state_specs = [nat, nat, big, mrow, nat, hist, nat_ssm, hist]
        state_shapes = [nat_shape, nat_shape, sq_shape, m_shape, nat_shape, hist_shape, ssm_shape,
                        hist_shape]
        y_spec = pl.BlockSpec((1, R, D_MODEL), lambda g, c: (g, 0, 0))
        y_shape = jax.ShapeDtypeStruct((n_seq // G, R, D_MODEL), BF16)
        state_in_specs = state_specs
    res = pl.pallas_call(
        functools.partial(_mixer_kernel, nb, G, T, carry),
        grid=(n_steps, n_chunks),
        in_specs=in_specs + state_in_specs,
        out_specs=[y_spec] + state_specs,
        out_shape=[y_shape] + state_shapes,
        scratch_shapes=[pltpu.VMEM((nb, G, T + SUBLANES, CONV_CH), F32),
                        pltpu.VMEM((nb, G, T + SUBLANES, CONV_CH), F32)],
        compiler_params=pltpu.CompilerParams(
            dimension_semantics=("arbitrary", "arbitrary"), vmem_limit_bytes=VMEM_LIMIT),
        name="mixer_prompt" if carry else "mixer_sample",
    )(*([proj] * nb), cos_t, sin_t, lp["gbx"], lp["norms"], lp["gcw"], lp["scw"], lp["scb"],
      consts["mbd_b"], consts["mbd_f"], consts["m2_b"], consts["m2_f"], consts["pm"],
      consts["lm"], consts["tril3"], ret_tabs[0], ret_tabs[1], *states)
    return res[0], res[1:]


def _ple(h3, p_ref, pn_ref, pg_ref, pp_ref):
    r = _rms(h3, pn_ref[...])
    gate = _sigmoid(jnp.dot(r.astype(BF16), pg_ref[...], preferred_element_type=F32))
    proj = jnp.dot(p_ref[...].astype(BF16), pp_ref[...], preferred_element_type=F32)
    return h3 + gate * proj


def _dense_kernel(h_ref, y_ref, wo_ref, nf_ref, wg_ref, wu_ref, wd_ref, p_ref, pn_ref, pg_ref,
                  pp_ref, o_ref):
    h2 = h_ref[...] + jnp.dot(y_ref[...], wo_ref[...], preferred_element_type=F32)
    c = _rms(h2, nf_ref[...]).astype(BF16)
    fw = FFN_DENSE // FFN_SPLIT
    h3 = h2
    for s in range(FFN_SPLIT):
        g = jnp.dot(c, wg_ref[:, fw * s:fw * (s + 1)], preferred_element_type=F32)
        u = jnp.dot(c, wu_ref[:, fw * s:fw * (s + 1)], preferred_element_type=F32)
        a = (_silu(g) * u).astype(BF16)
        h3 = h3 + jnp.dot(a, wd_ref[fw * s:fw * (s + 1), :], preferred_element_type=F32)
    o_ref[...] = _ple(h3, p_ref, pn_ref, pg_ref, pp_ref)


def _resident(shape):
    return pl.BlockSpec(shape, lambda i: (0,) * len(shape), pipeline_mode=pl.Buffered(1))


def _dense_call(h, y, p_all, layer, lw):
    t = h.shape[0]
    rows = lambda w: pl.BlockSpec((TM, w), lambda i: (i, 0))
    return pl.pallas_call(
        _dense_kernel,
        grid=(t // TM,),
        in_specs=[
            rows(D_MODEL), rows(D_MODEL),
            _resident((D_MODEL, D_MODEL)), _resident((1, D_MODEL)),
            _resident((D_MODEL, FFN_DENSE)), _resident((D_MODEL, FFN_DENSE)),
            _resident((FFN_DENSE, D_MODEL)),
            pl.BlockSpec((None, TM, PLE_DIM), lambda i: (layer, i, 0)),
            _resident((1, D_MODEL)), _resident((D_MODEL, D_MODEL)), _resident((PLE_DIM, D_MODEL)),
        ],
        out_specs=rows(D_MODEL),
        out_shape=jax.ShapeDtypeStruct((t, D_MODEL), F32),
        compiler_params=pltpu.CompilerParams(
            dimension_semantics=("arbitrary",), vmem_limit_bytes=VMEM_LIMIT),
        name="dense_ffn",
    )(h, y, lw["w_out"], lw["norm_ffn"], lw["wg"], lw["wu"], lw["wd"], p_all,
      lw["ple_norm"], lw["ple_w_gate"], lw["ple_w_proj"])


R_I0, R_I1, R_W0, R_W1 = 0, 1, 2, 3


def _moe_pre_kernel(h_ref, y_ref, wo_ref, nf_ref, rt_ref, h2_ref, c_ref, r_ref):
    h2 = h_ref[...] + jnp.dot(y_ref[...], wo_ref[...], preferred_element_type=F32)
    h2_ref[...] = h2
    c = _rms(h2, nf_ref[...])
    c_ref[...] = c.astype(BF16)
    lane = lax.broadcasted_iota(jnp.int32, (TM, LANES), 1)
    logits = jnp.where(lane < N_EXPERTS, _dot_hi(c, rt_ref[...]), -jnp.inf)
    m1 = jnp.max(logits, axis=1, keepdims=True)
    i1 = jnp.min(jnp.where(logits == m1, lane, LANES), axis=1, keepdims=True)
    rest = jnp.where(lane == i1, -jnp.inf, logits)
    m2 = jnp.max(rest, axis=1, keepdims=True)
    i2 = jnp.min(jnp.where(rest == m2, lane, LANES), axis=1, keepdims=True)
    e2 = jnp.exp(m2 - m1)
    den = 1.0 + e2
    r_ref[...] = jnp.where(lane == R_I0, i1.astype(F32),
                           jnp.where(lane == R_I1, i2.astype(F32),
                                     jnp.where(lane == R_W0, 1.0 / den,
                                               jnp.where(lane == R_W1, e2 / den, 0.0))))


def _moe_pre_call(h, y, lw):
    t = h.shape[0]
    rows = lambda w: pl.BlockSpec((TM, w), lambda i: (i, 0))
    return pl.pallas_call(
        _moe_pre_kernel,
        grid=(t // TM,),
        in_specs=[rows(D_MODEL), rows(D_MODEL), _resident((D_MODEL, D_MODEL)),
                  _resident((1, D_MODEL)), _resident((D_MODEL, LANES))],
        out_specs=[rows(D_MODEL), rows(D_MODEL), rows(LANES)],
        out_shape=[jax.ShapeDtypeStruct((t, D_MODEL), F32),
                   jax.ShapeDtypeStruct((t, D_MODEL), BF16),
                   jax.ShapeDtypeStruct((t, LANES), F32)],
        compiler_params=pltpu.CompilerParams(
            dimension_semantics=("arbitrary",), vmem_limit_bytes=VMEM_LIMIT),
        name="moe_pre",
    )(h, y, lw["w_out"], lw["norm_ffn"], lw["router"])


def _moe_ffn_kernel(be_ref, nb_ref, x_ref, wg_ref, wu_ref, wd_ref, o_ref):
    del be_ref
    i = pl.program_id(0)

    @pl.when(i < nb_ref[0])
    def _compute():
        x = x_ref[...]
        fw = FFN_EXPERT // FFN_SPLIT
        acc = jnp.zeros((TMM, D_MODEL), F32)
        for s in range(FFN_SPLIT):
            g = jnp.dot(x, wg_ref[0, :, fw * s:fw * (s + 1)], preferred_element_type=F32)
            u = jnp.dot(x, wu_ref[0, :, fw * s:fw * (s + 1)], preferred_element_type=F32)
            a = (_silu(g) * u).astype(BF16)
            acc = acc + jnp.dot(a, wd_ref[0, fw * s:fw * (s + 1), :], preferred_element_type=F32)
        o_ref[...] = acc.astype(o_ref.dtype)

    @pl.when(i >= nb_ref[0])
    def _skip():
        o_ref[...] = jnp.zeros_like(o_ref)


def _moe_ffn_call(xs, blk_e, nblk, lw):
    n_rows = xs.shape[0]
    wspec = lambda shape: pl.BlockSpec(shape, lambda i, be, nb: (be[i], 0, 0),
                                       pipeline_mode=pl.Buffered(1))
    grid_spec = pltpu.PrefetchScalarGridSpec(
        num_scalar_prefetch=2,
        grid=(n_rows // TMM,),
        in_specs=[
            pl.BlockSpec((TMM, D_MODEL), lambda i, be, nb: (i, 0)),
            wspec((1, D_MODEL, FFN_EXPERT)), wspec((1, D_MODEL, FFN_EXPERT)),
            wspec((1, FFN_EXPERT, D_MODEL)),
        ],
        out_specs=pl.BlockSpec((TMM, D_MODEL), lambda i, be, nb: (i, 0)),
    )
    return pl.pallas_call(
        _moe_ffn_kernel,
        grid_spec=grid_spec,
        out_shape=jax.ShapeDtypeStruct((n_rows, D_MODEL), BF16),
        compiler_params=pltpu.CompilerParams(
            dimension_semantics=("arbitrary",), vmem_limit_bytes=VMEM_LIMIT),
        name="moe_ffn",
    )(blk_e, nblk, xs, lw["wg"], lw["wu"], lw["wd"])


def _moe_post_kernel(final, h2_ref, y0_ref, y1_ref, r_ref, p_ref, pn_ref, pg_ref, pp_ref, nfin_ref,
                     o_ref):
    r = r_ref[...]
    w0 = r[:, R_W0:R_W0 + 1]
    w1 = r[:, R_W1:R_W1 + 1]
    h3 = h2_ref[...] + (w0 * y0_ref[...].astype(F32) + w1 * y1_ref[...].astype(F32))
    h4 = _ple(h3, p_ref, pn_ref, pg_ref, pp_ref)
    o_ref[...] = _rms(h4, nfin_ref[...]) if final else h4


def _moe_post_call(h2, y0, y1, route, p_all, layer, lw, norm_final, final):
    t = h2.shape[0]
    rows = lambda w: pl.BlockSpec((TM, w), lambda i: (i, 0))
    return pl.pallas_call(
        functools.partial(_moe_post_kernel, final),
        grid=(t // TM,),
        in_specs=[rows(D_MODEL), rows(D_MODEL), rows(D_MODEL), rows(LANES),
                  pl.BlockSpec((None, TM, PLE_DIM), lambda i: (layer, i, 0)),
                  _resident((1, D_MODEL)), _resident((D_MODEL, D_MODEL)),
                  _resident((PLE_DIM, D_MODEL)), _resident((1, D_MODEL))],
        out_specs=rows(D_MODEL),
        out_shape=jax.ShapeDtypeStruct((t, D_MODEL), F32),
        compiler_params=pltpu.CompilerParams(
            dimension_semantics=("arbitrary",), vmem_limit_bytes=VMEM_LIMIT),
        name="moe_post",
    )(h2, y0, y1, route, p_all, lw["ple_norm"], lw["ple_w_gate"], lw["ple_w_proj"], norm_final)


def _route_plan(route, n_rows):
    t = route.shape[0]
    e_flat = route[:, R_I0:R_I1 + 1].astype(jnp.int32).reshape(-1)
    onehot = (jnp.arange(N_EXPERTS, dtype=jnp.int32)[:, None] == e_flat[None, :]).astype(jnp.int32)
    csum = jnp.cumsum(onehot, axis=1)
    counts = csum[:, -1]
    pcounts = ((counts + TMM - 1) // TMM) * TMM
    ends = jnp.cumsum(pcounts)
    starts = ends - pcounts
    pos = jnp.sum(onehot * (csum - 1 + starts[:, None]), axis=0)
    src = (jnp.arange(n_rows, dtype=jnp.int32) % t).at[pos].set(
        jnp.arange(2 * t, dtype=jnp.int32) // 2)
    blk_start = jnp.arange(n_rows // TMM, dtype=jnp.int32) * TMM
    blk_e = jnp.minimum(jnp.sum((blk_start[:, None] >= ends[None, :]).astype(jnp.int32), axis=1),
                        N_EXPERTS - 1).astype(jnp.int32)
    nblk = (ends[-1:] // TMM).astype(jnp.int32)
    return src, pos.reshape(t, 2), blk_e, nblk


def _rope_tables(pos):
    half = HEAD_DIM // 2
    inv_freq = ROPE_BASE ** (-jnp.arange(half, dtype=F32) / half)
    ang = pos[:, None] * inv_freq[None, :]
    cos, sin = jnp.cos(ang), jnp.sin(ang)
    cos_h = jnp.concatenate([cos, cos], axis=-1)
    sin_h = jnp.concatenate([-sin, sin], axis=-1)
    return jnp.tile(cos_h, (1, N_HEADS)), jnp.tile(sin_h, (1, N_HEADS))


def _pad_lanes(x, width=LANES):
    return jnp.pad(x, [(0, 0)] * (x.ndim - 1) + [(0, width - x.shape[-1])])


def kernel(x_prompt, x_sample, state_ret, state_mlstm_c, state_mlstm_n, state_mlstm_m, state_gdn, state_gdn_conv, state_ssm, state_ssm_conv, p_prompt, p_sample, norm_mix, w_in, w_out, ret_norm, mlstm_i_bias, mlstm_f_bias, mlstm_norm, gdn_conv_w, gdn_a_log, gdn_dt_bias, gdn_norm, ssm_conv_w, ssm_conv_b, ssm_dt_bias, ssm_a_log, ssm_d, ssm_norm, norm_ffn, ffn_w_gate, ffn_w_up, ffn_w_down, moe_router, moe_w_gate, moe_w_up, moe_w_down, ple_w_proj, ple_norm, ple_w_gate, norm_final):
    bp, tp = x_prompt.shape[:2]
    bs, ts = x_sample.shape[:2]
    n_prompt = bp * tp
    n_all = n_prompt + bs * ts
    sample_seqs = MIX_ROWS // ts
    assert tp % MIX_ROWS == 0 and bp % PROMPT_SEQS == 0 and n_all % TM == 0
    assert MIX_ROWS % ts == 0 and bs % sample_seqs == 0 and ts >= CONV_W - 1 and DEPTH % 2 == 0
    hd = HEAD_DIM

    h = jnp.concatenate([x_prompt.reshape(n_prompt, D_MODEL), x_sample.reshape(bs * ts, D_MODEL)])
    h = h.astype(F32)
    p_all = jnp.concatenate([p_prompt.reshape(DEPTH, n_prompt, PLE_DIM),
                             p_sample.reshape(DEPTH, bs * ts, PLE_DIM)], axis=1)

    o_ml = 4 * GROUP_W
    o_gdn = o_ml + 4 * GROUP_W + 2 * N_HEADS
    o_ssm = o_gdn + 4 * GROUP_W + 2 * N_HEADS
    gate_cols = jnp.concatenate([
        w_in[:, :, o_ml + 4 * GROUP_W:o_gdn],
        w_in[:, :, o_gdn + 4 * GROUP_W:o_ssm],
        w_in[:, :, o_ssm + 4 * GROUP_W:],
    ], axis=-1)
    w_in_r = jnp.concatenate([
        w_in[:, :, 0:o_ml + 4 * GROUP_W],
        w_in[:, :, o_gdn:o_gdn + 4 * GROUP_W],
        w_in[:, :, o_ssm:o_ssm + 4 * GROUP_W],
        jnp.repeat(gate_cols, hd, axis=-1),
    ], axis=-1)
    assert w_in_r.shape[-1] == D_PROJ
    col = jnp.arange(D_PROJ)
    k_cols = ((col >= C_RET + GROUP_W) & (col < C_RET + 2 * GROUP_W)) | (
        (col >= C_ML + GROUP_W) & (col < C_ML + 2 * GROUP_W))
    w_in_r = (w_in_r * jnp.where(k_cols, hd ** -0.5, 1.0).astype(F32)).astype(BF16)

    zeros4 = jnp.zeros((DEPTH, N_HEADS), F32)
    norms = jnp.stack([ret_norm, mlstm_norm, gdn_norm, ssm_norm], axis=1).astype(F32)
    gbx = jnp.repeat(jnp.stack([mlstm_i_bias, mlstm_f_bias, gdn_dt_bias, ssm_dt_bias, gdn_a_log,
                                ssm_a_log, ssm_d, zeros4], axis=1).astype(F32), hd, axis=-1)

    consts_p = _mixer_consts(1, MIX_ROWS)
    consts_s = _mixer_consts(sample_seqs, ts)
    ret_tabs_p = _ret_tables(1, MIX_ROWS)
    ret_tabs_s = _ret_tables(sample_seqs, ts)
    tab_prompt = _rope_tables(jnp.arange(tp, dtype=F32))
    tab_sample = tuple(jnp.tile(t, (sample_seqs, 1))
                       for t in _rope_tables(PAST_LEN + jnp.arange(ts, dtype=F32)))

    keep = SUBLANES - (CONV_W - 1)
    eye_h = jnp.eye(N_HEADS, dtype=F32)

    def conv_in(s):
        return jnp.pad(s.astype(F32), ((0, 0), (keep, 0), (0, 0)))

    def sample_states_in(i):
        n_bd = (state_mlstm_n[i].astype(F32)[:, :, :, None, None]
                * eye_h[None, :, None, :, None])
        n_bd = jnp.broadcast_to(n_bd, (bs, N_HEADS, hd, N_HEADS, hd)).reshape(bs, GROUP_W, GROUP_W)
        m_x = jnp.repeat(state_mlstm_m[i].astype(F32), hd, axis=-1)[:, None, :]
        return (state_ret[i].astype(F32), state_mlstm_c[i].astype(F32), n_bd,
                jnp.broadcast_to(m_x, (bs, SUBLANES, GROUP_W)), state_gdn[i].astype(F32),
                conv_in(state_gdn_conv[i]), state_ssm[i].astype(F32), conv_in(state_ssm_conv[i]))

    def n_vec(s_mn):
        return jnp.stack([s_mn[:, hd * hh:hd * (hh + 1), hd * hh] for hh in range(N_HEADS)], axis=1)

    def sample_states_out(st):
        s_ret, s_mc, s_mn, s_mm, s_gdn, s_gconv, s_ssm, s_sconv = st
        return (s_ret, s_mc, n_vec(s_mn), s_mm[:, 0, ::hd], s_gdn, s_gconv[:, keep:], s_ssm,
                s_sconv[:, keep:])

    def prompt_states_out(st):
        s_ret, s_mc, s_mn, s_mm, s_gdn, s_gconv, s_ssm, s_sconv = st

        def diag(s):
            return jnp.stack([s[:, hd * hh:hd * (hh + 1), hd * hh:hd * (hh + 1)]
                              for hh in range(N_HEADS)], axis=1)

        per_group = N_HEADS // SSM_GROUPS
        ssm = jnp.stack([s_ssm[:, SSM_STATE * (hh // per_group):SSM_STATE * (hh // per_group + 1),
                               hd * hh:hd * (hh + 1)] for hh in range(N_HEADS)], axis=1)
        return (diag(s_ret), diag(s_mc), n_vec(s_mn), s_mm[:, 0, ::hd], diag(s_gdn),
                s_gconv[:, keep:], ssm, s_sconv[:, keep:])

    n_moe_rows = 2 * n_all + N_EXPERTS * TMM
    new_prompt, new_sample = [], []
    for i in range(DEPTH):
        lp = dict(norms=norms[i], gcw=gdn_conv_w[i].astype(F32), scw=ssm_conv_w[i].astype(F32),
                  scb=ssm_conv_b[i].astype(F32)[None, :], gbx=gbx[i])
        proj = _proj_call(h, norm_mix[i].astype(F32)[None, :], w_in_r[i])
        y_p, st_p = _mixer_call(proj, 0, bp, tp, PROMPT_SEQS, 1, True, tab_prompt, lp, consts_p,
                                ret_tabs_p)
        y_s, st_s = _mixer_call(proj, n_prompt, bs, ts, 1, sample_seqs, False, tab_sample, lp,
                                consts_s, ret_tabs_s, sample_states_in(i))
        y_all = jnp.concatenate([y_p.reshape(n_prompt, D_MODEL), y_s.reshape(bs * ts, D_MODEL)])
        new_prompt.append(prompt_states_out(st_p))
        new_sample.append(sample_states_out(st_s))

        j = i // 2
        lw = dict(w_out=w_out[i].astype(BF16), norm_ffn=norm_ffn[i].astype(F32)[None, :],
                  ple_norm=ple_norm[i].astype(F32)[None, :], ple_w_gate=ple_w_gate[i].astype(BF16),
                  ple_w_proj=ple_w_proj[i].astype(BF16))
        if i % 2 == 0:
            lw.update(wg=ffn_w_gate[j].astype(BF16), wu=ffn_w_up[j].astype(BF16),
                      wd=ffn_w_down[j].astype(BF16))
            h = _dense_call(h, y_all, p_all, i, lw)
        else:
            lw.update(router=_pad_lanes(moe_router[j].astype(F32)), wg=moe_w_gate[j].astype(BF16),
                      wu=moe_w_up[j].astype(BF16), wd=moe_w_down[j].astype(BF16))
            h2, c, route = _moe_pre_call(h, y_all, lw)
            src, pos, blk_e, nblk = _route_plan(route, n_moe_rows)
            ys = _moe_ffn_call(jnp.take(c, src, axis=0, mode="clip"), blk_e, nblk, lw)
            y0 = jnp.take(ys, pos[:, 0], axis=0, mode="clip")
            y1 = jnp.take(ys, pos[:, 1], axis=0, mode="clip")
            h = _moe_post_call(h2, y0, y1, route, p_all, i, lw, norm_final.astype(F32)[None, :],
                               final=(i == DEPTH - 1))

    y = h.astype(x_prompt.dtype)
    y_prompt = y[:n_prompt].reshape(bp, tp, D_MODEL)
    y_sample = y[n_prompt:].reshape(bs, ts, D_MODEL)
    stack = lambda lst: tuple(jnp.stack([l[k] for l in lst]) for k in range(8))
    return (y_prompt, y_sample) + stack(new_prompt) + stack(new_sample)
```

```python
import functools

import jax
import jax.numpy as jnp
from jax import lax
from jax.experimental import pallas as pl
from jax.experimental.pallas import tpu as pltpu

F32 = jnp.float32
BF16 = jnp.bfloat16
HIGHEST = lax.Precision.HIGHEST

D_MODEL = 1024
DEPTH = 4
PAST_LEN = 16384
N_HEADS = 4
HEAD_DIM = 64
GROUP_W = N_HEADS * HEAD_DIM
SSM_STATE = 128
SSM_GROUPS = 2
CONV_W = 4
CONV_CH = 3 * GROUP_W
ROPE_BASE = 10000.0
FFN_DENSE = 2816
N_EXPERTS = 8
FFN_EXPERT = 3584
PLE_DIM = 256
EPS = 1e-6

LANES = 128
SUBLANES = 8
VMEM_LIMIT = 56 * 1024 * 1024

C_RET = 0
C_ML = 1024
C_GDN = 2048
C_SSM = 3072
C_XMI = 4096
C_XMF = C_XMI + GROUP_W
C_XGB = C_XMF + GROUP_W
C_XGA = C_XGB + GROUP_W
C_XDT = C_XGA + GROUP_W
D_PROJ = C_XDT + GROUP_W

MIX_ROWS = 64
PROMPT_SEQS = 4
TM = 512
TMM = 512
FFN_SPLIT = 2


def _dot_hi(a, b):
    return jnp.dot(a, b, precision=HIGHEST, preferred_element_type=F32)


def _sigmoid(x):
    return 1.0 / (1.0 + jnp.exp(-x))


def _silu(x):
    return x * _sigmoid(x)


def _softplus(x):
    return jnp.maximum(x, 0.0) + jnp.log1p(jnp.exp(-jnp.abs(x)))


def _rms(x, g):
    return x * lax.rsqrt(jnp.mean(x * x, -1, keepdims=True) + EPS) * g


def _proj_kernel(h_ref, g_ref, w_ref, o_ref):
    a = _rms(h_ref[...], g_ref[...])
    o_ref[...] = jnp.dot(a.astype(BF16), w_ref[...], preferred_element_type=F32)


def _proj_call(h, g, w):
    t = h.shape[0]
    return pl.pallas_call(
        _proj_kernel,
        grid=(t // TM,),
        in_specs=[
            pl.BlockSpec((TM, D_MODEL), lambda i: (i, 0)),
            pl.BlockSpec((1, D_MODEL), lambda i: (0, 0)),
            pl.BlockSpec((D_MODEL, D_PROJ), lambda i: (0, 0), pipeline_mode=pl.Buffered(1)),
        ],
        out_specs=pl.BlockSpec((TM, D_PROJ), lambda i: (i, 0)),
        out_shape=jax.ShapeDtypeStruct((t, D_PROJ), F32),
        compiler_params=pltpu.CompilerParams(
            dimension_semantics=("arbitrary",), vmem_limit_bytes=VMEM_LIMIT),
        name="proj",
    )(h, g, w)


def _split3(x):
    hi = x.astype(BF16)
    r1 = x - hi.astype(F32)
    mid = r1.astype(BF16)
    lo = (r1 - mid.astype(F32)).astype(BF16)
    return hi, mid, lo


def _dot_tn(a, b):
    return lax.dot_general(a, b, (((0,), (0,)), ((), ())), preferred_element_type=F32)


def _mixer_kernel(nb, G, T, carry, *refs):
    R = G * T
    hd = HEAD_DIM
    n_levels = T.bit_length() - 1
    per_group = N_HEADS // SSM_GROUPS
    proj_refs = refs[:nb]
    (cos_ref, sin_ref, gb_ref, norms_ref, gcw_ref, scw_ref, scb_ref, mbd_b_ref, mbd_f_ref,
     m2_b_ref, m2_f_ref, pm_ref, lm_ref, tril3_ref, rt_ref, rs_ref) = refs[nb:nb + 16]
    rest = refs[nb + 16:]
    if carry:
        (y_ref, o_ret, o_mc, o_mn, o_mm, o_gdn, o_gconv, o_ssm, o_sconv, xg_ref, xs_ref) = rest
    else:
        (i_ret, i_mc, i_mn, i_mm, i_gdn, i_gconv, i_ssm, i_sconv,
         y_ref, o_ret, o_mc, o_mn, o_mm, o_gdn, o_gconv, o_ssm, o_sconv, xg_ref, xs_ref) = rest

    if carry:
        @pl.when(pl.program_id(1) == 0)
        def _init():
            for r in (o_ret, o_mc, o_mn, o_mm, o_gdn, o_ssm, xg_ref, xs_ref):
                r[...] = jnp.zeros_like(r)

    def to_bd(s4):
        rows = []
        for h in range(N_HEADS):
            parts = [jnp.zeros((hd, hd * h), F32)] if h else []
            parts.append(s4[h])
            if h < N_HEADS - 1:
                parts.append(jnp.zeros((hd, hd * (N_HEADS - 1 - h)), F32))
            rows.append(jnp.concatenate(parts, axis=1))
        return jnp.concatenate(rows, axis=0)

    def to_st(s4):
        rows = []
        for g in range(SSM_GROUPS):
            parts = [jnp.zeros((SSM_STATE, SSM_STATE * g), F32)] if g else []
            parts += [s4[per_group * g + k] for k in range(per_group)]
            if g < SSM_GROUPS - 1:
                parts.append(jnp.zeros((SSM_STATE, SSM_STATE * (SSM_GROUPS - 1 - g)), F32))
            rows.append(jnp.concatenate(parts, axis=1))
        return jnp.concatenate(rows, axis=0)

    out_refs = dict(ret=o_ret, mc=o_mc, mn=o_mn, gdn=o_gdn, ssm=o_ssm)

    def live_blocks(kind):
        if kind == "ssm":
            return [(SSM_STATE * g, SSM_STATE, SSM_STATE * g, SSM_STATE) for g in range(SSM_GROUPS)]
        return [(hd * h, hd, hd * h, hd) for h in range(N_HEADS)]

    if carry:
        def get(kind, j, i):
            return out_refs[kind][j]

        def update(kind, j, i, dec, dlt):
            ref = out_refs[kind]
            for r0, nr, c0, nc in live_blocks(kind):
                ref[j, r0:r0 + nr, c0:c0 + nc] = (ref[j, r0:r0 + nr, c0:c0 + nc] * dec[:, c0:c0 + nc]
                                                  + dlt[r0:r0 + nr, c0:c0 + nc])

        def get_m(j, i):
            return o_mm[j, 0:1, :]

        def put_m(j, i, row):
            o_mm[j] = jnp.broadcast_to(row, (SUBLANES, GROUP_W))
    else:
        in_refs = dict(ret=i_ret, mc=i_mc, gdn=i_gdn, ssm=i_ssm)

        def get(kind, j, i):
            if kind == "mn":
                return i_mn[i]
            if kind == "ssm":
                return to_st(i_ssm[i])
            return to_bd(in_refs[kind][i])

        def update(kind, j, i, dec, dlt):
            if kind == "mn":
                o_mn[i] = i_mn[i] * dec + mbd_f * dlt
                return
            for h in range(N_HEADS):
                r0, nr = (SSM_STATE * (h // per_group), SSM_STATE) if kind == "ssm" else (hd * h, hd)
                out_refs[kind][i, h] = (in_refs[kind][i, h] * dec[:, hd * h:hd * (h + 1)]
                                        + dlt[r0:r0 + nr, hd * h:hd * (h + 1)])

        def get_m(j, i):
            return i_mm[i, 0:1, :]

        def put_m(j, i, row):
            o_mm[i] = jnp.broadcast_to(row, (SUBLANES, GROUP_W))

    def seq_rows(x, i):
        return x if G == 1 else x[T * i:T * (i + 1)]

    def seq_row(x, i):
        return x if x.shape[0] == 1 else x[T * i:T * i + 1]

    def seg_last(x):
        if G == 1:
            return x[R - 1:R]
        last = x.reshape(G, T, x.shape[1])[:, T - 1:T, :]
        return jnp.broadcast_to(last, (G, T, x.shape[1])).reshape(R, x.shape[1])

    def seg_max0(x):
        if G == 1:
            return jnp.max(x, axis=0, keepdims=True)
        mx = jnp.max(x.reshape(G, T, x.shape[1]), axis=1, keepdims=True)
        return jnp.broadcast_to(mx, (G, T, x.shape[1])).reshape(R, x.shape[1])

    def m_rows(j):
        if G == 1:
            return get_m(j, 0)
        return jnp.concatenate([jnp.broadcast_to(get_m(j, i), (T, GROUP_W)) for i in range(G)],
                               axis=0)

    mbd_b = mbd_b_ref[...]
    mbd_f = mbd_f_ref[...]
    m2_b = m2_b_ref[...]
    m2_f = m2_f_ref[...]
    eye_p = pm_ref[0]
    ut_p = pm_ref[1]
    ninf_p = pm_ref[2]
    tril3 = tril3_ref[...]
    gb = gb_ref[...]
    norms = norms_ref[...]
    cosv = cos_ref[...]
    sinv = sin_ref[...]
    lane_w = lax.broadcasted_iota(jnp.int32, (R, GROUP_W), 1)
    first_half = (lane_w % hd) < (hd // 2)
    ones_b = jnp.ones((R, GROUP_W), BF16)
    neg_a_gdn = -jnp.exp(gb[4:5, :])
    neg_a_ssm = -jnp.exp(gb[5:6, :])
    inv_hd = 1.0 / hd

    def bf(x):
        return x.astype(BF16)

    def bd(xb, mask):
        return jnp.concatenate([xb] * (GROUP_W // R), axis=0) * mask

    def mm(a, b):
        return jnp.dot(a, b, preferred_element_type=F32)

    def mm_nt(a, b):
        return lax.dot_general(a, b, (((1,), (1,)), ((), ())), preferred_element_type=F32)

    def mm_state(a, states):
        outs = [mm(seq_rows(a, i), bf(states[i])) for i in range(G)]
        return outs[0] if G == 1 else jnp.concatenate(outs, axis=0)

    def delta(a, b, i):
        return _dot_tn(seq_rows(a, i), seq_rows(b, i))

    def cumsum(g):
        return mm(tril3, jnp.concatenate(_split3(g), axis=0))

    def sum0(x):
        return jnp.sum(x, axis=0, keepdims=True)

    def segmax(x):
        parts = []
        for h in range(N_HEADS):
            mh = jnp.max(x[:, hd * h:hd * (h + 1)], axis=1, keepdims=True)
            parts.append(jnp.broadcast_to(mh, (R, hd)))
        return jnp.concatenate(parts, axis=1)

    def rope(x):
        swapped = jnp.where(first_half, pltpu.roll(x, GROUP_W - hd // 2, 1),
                            pltpu.roll(x, hd // 2, 1))
        return x * cosv + swapped * sinv

    def conv4(x_ref, j, x, w, hist_in, hist_out):
        if not carry:
            x_ref[j, :, 0:SUBLANES, :] = hist_in[...]
        x_ref[j, :, SUBLANES:SUBLANES + T, :] = x.reshape(G, T, CONV_CH)
        acc = w[0:1, :] * x_ref[j, :, pl.ds(SUBLANES - 3, T), :]
        for k in range(1, CONV_W):
            acc = acc + w[k:k + 1, :] * x_ref[j, :, pl.ds(SUBLANES - 3 + k, T), :]
        hist = x_ref[j, :, T:T + SUBLANES, :]
        if carry:
            hist_out[j] = hist[0]
            x_ref[j, :, 0:SUBLANES, :] = hist
        else:
            hist_out[...] = hist
        return acc.reshape(R, CONV_CH)


    def ret_chain(j):
        P = proj_refs[j]
        rq = rope(P[:, C_RET:C_RET + GROUP_W])
        rk = rope(P[:, C_RET + GROUP_W:C_RET + 2 * GROUP_W])
        rvb = bf(P[:, C_RET + 2 * GROUP_W:C_RET + 3 * GROUP_W])
        S = [get("ret", j, i) for i in range(G)]
        scores = mm_nt(bf(rq), bd(bf(rk), mbd_b)) * rt_ref[0]
        o_inter = mm_state(bf(rq * rt_ref[1]), S)
        yield
        ke = bf(rk * rt_ref[2])
        for i in range(G):
            update("ret", j, i, rs_ref[0:1, :], delta(ke, rvb, i))
        o = mm(bf(scores), bd(rvb, mbd_b)) + o_inter
        yield
        mean = mm(bf(o), mbd_b) * inv_hd
        yield
        xc = o - mean
        var = mm(bf(xc * xc), mbd_b) * inv_hd
        yield
        ro = xc * lax.rsqrt(var + EPS) * norms[0:1, :]
        rg = P[:, C_RET + 3 * GROUP_W:C_RET + 4 * GROUP_W]
        y_ref[j, :, 0:GROUP_W] = (_silu(rg) * ro).astype(y_ref.dtype)

    def mlstm_chain(j):
        P = proj_refs[j]
        mq = bf(P[:, C_ML:C_ML + GROUP_W])
        mk = P[:, C_ML + GROUP_W:C_ML + 2 * GROUP_W]
        mvb = bf(P[:, C_ML + 2 * GROUP_W:C_ML + 3 * GROUP_W])
        i_x = P[:, C_XMI:C_XMI + GROUP_W] + gb[0:1, :]
        g_f = -_softplus(-(P[:, C_XMF:C_XMF + GROUP_W] + gb[1:2, :]))
        b_col = cumsum(g_f)
        qk = mm_nt(mq, bd(bf(mk), mbd_b))
        C = [get("mc", j, i) for i in range(G)]
        N = [get("mn", j, i) for i in range(G)]
        q_c = mm_state(mq, C)
        q_n = mm_state(mq, N)
        yield
        b_row = sum0(g_f * ut_p)
        i_row = sum0(i_x * eye_p)
        b_last = seg_last(b_col)
        m_prev = m_rows(j)
        dd = b_col - b_row + i_row
        inter = b_col + m_prev
        m_row = jnp.maximum(segmax(dd + ninf_p), inter)
        w_intra = jnp.exp(jnp.minimum(dd - m_row, ninf_p))
        w_state = jnp.exp(inter - m_row)
        sb = bf(qk * w_intra)
        num = mm(sb, bd(mvb, mbd_b)) + w_state * q_c
        den = mm(sb, mbd_b) + w_state * q_n
        g_end = b_last - b_col + i_x
        m_new = jnp.maximum(b_last + m_prev, seg_max0(g_end))
        w_c = jnp.exp(b_last + m_prev - m_new)
        kwb = bf(mk * jnp.exp(g_end - m_new))
        for i in range(G):
            wc_i = seq_row(w_c, i)
            update("mc", j, i, wc_i, delta(kwb, mvb, i))
            update("mn", j, i, wc_i, delta(kwb, ones_b, i))
            put_m(j, i, seq_row(m_new, i))
        yield
        hh = num / jnp.maximum(jnp.abs(den), jnp.exp(-m_row))
        mean = mm(bf(hh), mbd_b) * inv_hd
        yield
        xc = hh - mean
        var = mm(bf(xc * xc), mbd_b) * inv_hd
        yield
        mh = xc * lax.rsqrt(var + EPS) * norms[1:2, :]
        mo = P[:, C_ML + 3 * GROUP_W:C_ML + 4 * GROUP_W]
        y_ref[j, :, GROUP_W:2 * GROUP_W] = (_sigmoid(mo) * mh).astype(y_ref.dtype)

    def gdn_chain(j):
        P = proj_refs[j]
        conv = conv4(xg_ref, j, P[:, C_GDN:C_GDN + CONV_CH], gcw_ref[...],
                     None if carry else i_gconv, o_gconv)
        act = _silu(conv)
        gq, gk, gv = act[:, 0:GROUP_W], act[:, GROUP_W:2 * GROUP_W], act[:, 2 * GROUP_W:]
        q_ss = mm(bf(gq * gq), mbd_b)
        k_ss = mm(bf(gk * gk), mbd_b)
        g_g = neg_a_gdn * _softplus(P[:, C_XGA:C_XGA + GROUP_W] + gb[2:3, :])
        b_col = cumsum(g_g)
        yield
        qn = gq * lax.rsqrt(q_ss + EPS) * hd ** -0.5
        kn = gk * lax.rsqrt(k_ss + EPS)
        beta = _sigmoid(P[:, C_XGB:C_XGB + GROUP_W])
        b_row = sum0(g_g * ut_p)
        b_last = seg_last(b_col)
        e_col = jnp.exp(b_col)
        decay = jnp.exp(jnp.minimum(b_col - b_row, ninf_p))
        kb = kn * beta
        bd_k = bd(bf(kn), mbd_b)
        kk = mm_nt(bf(kb), bd_k)
        qk = mm_nt(bf(qn), bd_k)
        S = [get("gdn", j, i) for i in range(G)]
        q_s = mm_state(bf(qn), S)
        yield
        a_low = kk * decay
        attn = qk * decay
        t_inv = eye_p - a_low * lm_ref[0]
        for lev in range(1, n_levels):
            m1 = mm(bf(a_low * lm_ref[lev]), bd(bf(t_inv), mbd_b))
            yield
            t_inv = t_inv - mm(bf(t_inv), bd(bf(m1), mbd_b))
            yield
        tb = bf(t_inv)
        u = mm(tb, bd(bf(gv * beta), mbd_b))
        w = mm(tb, bd(bf(kb * e_col), mbd_b))
        yield
        v_new = u - mm_state(bf(w), S)
        yield
        vnb = bf(v_new)
        o = e_col * q_s + mm(bf(attn), bd(vnb, mbd_b))
        keb = bf(kn * jnp.exp(b_last - b_col))
        s_dec = jnp.exp(b_last)
        for i in range(G):
            update("gdn", j, i, seq_row(s_dec, i), delta(keb, vnb, i))
        yield
        o_ss = mm(bf(o * o), mbd_b)
        yield
        go = o * lax.rsqrt(o_ss * inv_hd + EPS) * norms[2:3, :]
        gz = P[:, C_GDN + CONV_CH:C_GDN + CONV_CH + GROUP_W]
        y_ref[j, :, 2 * GROUP_W:3 * GROUP_W] = (go * _silu(gz)).astype(y_ref.dtype)

    def ssd_chain(j):
        P = proj_refs[j]
        conv = conv4(xs_ref, j, P[:, C_SSM + GROUP_W:C_SSM + GROUP_W + CONV_CH], scw_ref[...],
                     None if carry else i_sconv, o_sconv) + scb_ref[...]
        act = _silu(conv)
        sx = act[:, 0:GROUP_W]
        s_b = bf(act[:, GROUP_W:2 * GROUP_W])
        s_c = bf(act[:, 2 * GROUP_W:])
        dt = _softplus(P[:, C_XDT:C_XDT + GROUP_W] + gb[3:4, :])
        g_s = dt * neg_a_ssm
        b_col = cumsum(g_s)
        cb = mm_nt(s_c, bd(s_b, m2_b))
        S = [get("ssm", j, i) for i in range(G)]
        c_s = mm_state(s_c, S)
        yield
        b_row = sum0(g_s * ut_p)
        b_last = seg_last(b_col)
        decay = jnp.exp(jnp.minimum(b_col - b_row, ninf_p))
        v = sx * dt
        o = mm(bf(cb * decay), bd(bf(v), mbd_b)) + jnp.exp(b_col) * c_s
        v_end = bf(v * jnp.exp(b_last - b_col))
        s_dec = jnp.exp(b_last)
        for i in range(G):
            update("ssm", j, i, seq_row(s_dec, i), delta(s_b, v_end, i))
        yield
        sz = P[:, C_SSM:C_SSM + GROUP_W]
        yv = (o + gb[6:7, :] * sx) * _silu(sz)
        ms = mm(bf(yv * yv), m2_b) * (1.0 / (GROUP_W // SSM_GROUPS))
        yield
        y_ref[j, :, 3 * GROUP_W:] = (yv * lax.rsqrt(ms + EPS) * norms[3:4, :]).astype(y_ref.dtype)

    chains = [chain(j) for chain in (gdn_chain, mlstm_chain, ret_chain, ssd_chain)
              for j in range(nb)]
    while chains:
        alive = []
        for ch in chains:
            try:
                next(ch)
                alive.append(ch)
            except StopIteration:
                pass
        chains = alive


def _mixer_consts(G, T):
    R = G * T
    n_levels = T.bit_length() - 1
    r = jnp.arange(GROUP_W)
    mbd = (r[:, None] // HEAD_DIM == r[None, :] // HEAD_DIM)
    half = GROUP_W // SSM_GROUPS
    m2 = (r[:, None] // half == r[None, :] // half)
    row = jnp.arange(R)[:, None]
    col = (r % R)[None, :]
    same = (row // T) == (col // T)
    tl, ts = row % T, col % T
    causal = same & (tl >= ts)
    pm = jnp.stack([(row == col).astype(F32), (same & (tl <= ts)).astype(F32),
                    jnp.where(causal, 0.0, -jnp.inf).astype(F32)])
    lm = jnp.stack([same & ((tl >> (j + 1)) == (ts >> (j + 1))) & (((tl >> j) & 1) == 1)
                    & (((ts >> j) & 1) == 0) for j in range(n_levels)]).astype(F32)
    tril = causal[:, :R].astype(BF16)
    return dict(mbd_b=mbd.astype(BF16), mbd_f=mbd.astype(F32), m2_b=m2.astype(BF16),
                m2_f=m2.astype(F32), pm=pm, lm=lm, tril3=jnp.concatenate([tril] * 3, axis=1))


def _ret_tables(G, T):
    R = G * T
    log_gamma = jnp.log1p(-jnp.exp2(-5.0 - jnp.arange(N_HEADS, dtype=F32)))
    b = jnp.cumsum(jnp.broadcast_to(log_gamma[:, None], (N_HEADS, T)), axis=1)
    b_r = jnp.tile(b, (1, G))
    pos = jnp.arange(R)
    causal = ((pos[:, None] // T) == (pos[None, :] // T)) & ((pos[:, None] % T) >= (pos[None, :] % T))
    diff = b_r[:, :, None] - b_r[:, None, :]
    dec = jnp.exp(jnp.where(causal[None], diff, -jnp.inf))
    dec_p = dec.transpose(1, 0, 2).reshape(R, GROUP_W)
    expand = lambda x: jnp.repeat(x.T, HEAD_DIM, axis=1)
    q_scale = expand(jnp.exp(b_r))
    k_scale = expand(jnp.exp(b[:, -1:] - b_r))
    s_dec = jnp.repeat(jnp.exp(b[:, -1]), HEAD_DIM)[None, :]
    return (jnp.stack([dec_p, q_scale, k_scale]),
            jnp.broadcast_to(s_dec, (SUBLANES, GROUP_W)))


def _mixer_call(proj, row0, n_seq, seq_len, nb, G, carry, tables, lp, consts, ret_tabs,
                states=(), layer=0):
    R = MIX_ROWS
    T = R // G
    n_chunks = seq_len // T
    n_levels = T.bit_length() - 1
    blk0 = row0 // R
    cos_t, sin_t = tables
    const2 = lambda g, c: (0, 0)
    const3 = lambda g, c: (0, 0, 0)
    st3 = lambda g, c: (g, 0, 0)
    st4 = lambda g, c: (g, 0, 0, 0)
    if carry:
        proj_specs = [pl.BlockSpec((R, D_PROJ), functools.partial(
            lambda g, c, j: (blk0 + (g * nb + j) * n_chunks + c, 0), j=j)) for j in range(nb)]
        tab_map = lambda g, c: (c, 0)
        n_steps = n_seq // nb
    else:
        assert nb == 1 and n_chunks == 1
        proj_specs = [pl.BlockSpec((R, D_PROJ), lambda g, c: (blk0 + g, 0))]
        tab_map = const2
        n_steps = n_seq // G
    sq = pl.BlockSpec((GROUP_W, GROUP_W), const2)
    in_specs = proj_specs + [
        pl.BlockSpec((R, GROUP_W), tab_map),
        pl.BlockSpec((R, GROUP_W), tab_map),
        pl.BlockSpec((SUBLANES, GROUP_W), const2),
        pl.BlockSpec((N_HEADS, GROUP_W), const2),
        pl.BlockSpec((CONV_W, CONV_CH), const2),
        pl.BlockSpec((CONV_W, CONV_CH), const2),
        pl.BlockSpec((1, CONV_CH), const2),
        sq, sq, sq, sq,
        pl.BlockSpec((3, R, GROUP_W), const3),
        pl.BlockSpec((n_levels, R, GROUP_W), const3),
        pl.BlockSpec((R, 3 * R), const2),
        pl.BlockSpec((3, R, GROUP_W), const3),
        pl.BlockSpec((SUBLANES, GROUP_W), const2),
    ]
    ns = nb if carry else G
    big = pl.BlockSpec((ns, GROUP_W, GROUP_W), st3)
    hist = pl.BlockSpec((ns, SUBLANES, CONV_CH), st3)
    mrow = pl.BlockSpec((ns, SUBLANES, GROUP_W), st3)
    sq_shape = jax.ShapeDtypeStruct((n_seq, GROUP_W, GROUP_W), F32)
    hist_shape = jax.ShapeDtypeStruct((n_seq, SUBLANES, CONV_CH), F32)
    m_shape = jax.ShapeDtypeStruct((n_seq, SUBLANES, GROUP_W), F32)
    if carry:
        state_specs = [big, big, big, mrow, big, hist, big, hist]
        state_shapes = [sq_shape, sq_shape, sq_shape, m_shape, sq_shape, hist_shape, sq_shape,
                        hist_shape]
        y_spec = pl.BlockSpec((nb, R, D_MODEL), lambda g, c: (g, c, 0))
        y_shape = jax.ShapeDtypeStruct((n_seq, seq_len, D_MODEL), BF16)
        state_in_specs = []
        aliases = {}
    else:
        st5 = lambda g, c: (layer, g, 0, 0, 0)
        nat = pl.BlockSpec((None, G, N_HEADS, HEAD_DIM, HEAD_DIM), st5)
        nat_ssm = pl.BlockSpec((None, G, N_HEADS, SSM_STATE, HEAD_DIM), st5)
        nat_shape = jax.ShapeDtypeStruct((DEPTH, n_seq, N_HEADS, HEAD_DIM, HEAD_DIM), F32)
        ssm_shape = jax.ShapeDtypeStruct((DEPTH, n_seq, N_HEADS, SSM_STATE, HEAD_DIM), F32)
        state_specs = [nat, nat, big, mrow, nat, hist, nat_ssm, hist]
        state_shapes = [nat_shape, nat_shape, sq_shape, m_shape, nat_shape, hist_shape, ssm_shape,
                        hist_shape]
        y_spec = pl.BlockSpec((1, R, D_MODEL), lambda g, c: (g, 0, 0))
        y_shape = jax.ShapeDtypeStruct((n_seq // G, R, D_MODEL), BF16)
        state_in_specs = state_specs
        n_fixed = len(in_specs)
        aliases = {n_fixed + k: 1 + k for k in (0, 1, 4, 6)}
    res = pl.pallas_call(
        functools.partial(_mixer_kernel, nb, G, T, carry),
        grid=(n_steps, n_chunks),
        in_specs=in_specs + state_in_specs,
        out_specs=[y_spec] + state_specs,
        out_shape=[y_shape] + state_shapes,
        scratch_shapes=[pltpu.VMEM((nb, G, T + SUBLANES, CONV_CH), F32),
                        pltpu.VMEM((nb, G, T + SUBLANES, CONV_CH), F32)],
        input_output_aliases=aliases,
        compiler_params=pltpu.CompilerParams(
            dimension_semantics=("arbitrary", "arbitrary"), vmem_limit_bytes=VMEM_LIMIT),
        name="mixer_prompt" if carry else "mixer_sample",
    )(*([proj] * nb), cos_t, sin_t, lp["gbx"], lp["norms"], lp["gcw"], lp["scw"], lp["scb"],
      consts["mbd_b"], consts["mbd_f"], consts["m2_b"], consts["m2_f"], consts["pm"],
      consts["lm"], consts["tril3"], ret_tabs[0], ret_tabs[1], *states)
    return res[0], res[1:]


def _ple(h3, p_ref, pn_ref, pg_ref, pp_ref):
    r = _rms(h3, pn_ref[...])
    gate = _sigmoid(jnp.dot(r.astype(BF16), pg_ref[...], preferred_element_type=F32))
    proj = jnp.dot(p_ref[...].astype(BF16), pp_ref[...], preferred_element_type=F32)
    return h3 + gate * proj


def _dense_kernel(h_ref, y_ref, wo_ref, nf_ref, wg_ref, wu_ref, wd_ref, p_ref, pn_ref, pg_ref,
                  pp_ref, o_ref):
    h2 = h_ref[...] + jnp.dot(y_ref[...], wo_ref[...], preferred_element_type=F32)
    c = _rms(h2, nf_ref[...]).astype(BF16)
    fw = FFN_DENSE // FFN_SPLIT
    h3 = h2
    for s in range(FFN_SPLIT):
        g = jnp.dot(c, wg_ref[:, fw * s:fw * (s + 1)], preferred_element_type=F32)
        u = jnp.dot(c, wu_ref[:, fw * s:fw * (s + 1)], preferred_element_type=F32)
        a = (_silu(g) * u).astype(BF16)
        h3 = h3 + jnp.dot(a, wd_ref[fw * s:fw * (s + 1), :], preferred_element_type=F32)
    o_ref[...] = _ple(h3, p_ref, pn_ref, pg_ref, pp_ref)


def _resident(shape):
    return pl.BlockSpec(shape, lambda i: (0,) * len(shape), pipeline_mode=pl.Buffered(1))


def _dense_call(h, y, p_all, layer, lw):
    t = h.shape[0]
    rows = lambda w: pl.BlockSpec((TM, w), lambda i: (i, 0))
    return pl.pallas_call(
        _dense_kernel,
        grid=(t // TM,),
        in_specs=[
            rows(D_MODEL), rows(D_MODEL),
            _resident((D_MODEL, D_MODEL)), _resident((1, D_MODEL)),
            _resident((D_MODEL, FFN_DENSE)), _resident((D_MODEL, FFN_DENSE)),
            _resident((FFN_DENSE, D_MODEL)),
            pl.BlockSpec((None, TM, PLE_DIM), lambda i: (layer, i, 0)),
            _resident((1, D_MODEL)), _resident((D_MODEL, D_MODEL)), _resident((PLE_DIM, D_MODEL)),
        ],
        out_specs=rows(D_MODEL),
        out_shape=jax.ShapeDtypeStruct((t, D_MODEL), F32),
        compiler_params=pltpu.CompilerParams(
            dimension_semantics=("arbitrary",), vmem_limit_bytes=VMEM_LIMIT),
        name="dense_ffn",
    )(h, y, lw["w_out"], lw["norm_ffn"], lw["wg"], lw["wu"], lw["wd"], p_all,
      lw["ple_norm"], lw["ple_w_gate"], lw["ple_w_proj"])


R_I0, R_I1, R_W0, R_W1 = 0, 1, 2, 3
C_COPIES = 2


def _moe_pre_kernel(h_ref, y_ref, wo_ref, nf_ref, rt_ref, h2_ref, c_ref, r_ref):
    h2 = h_ref[...] + jnp.dot(y_ref[...], wo_ref[...], preferred_element_type=F32)
    h2_ref[...] = h2
    c = _rms(h2, nf_ref[...])
    cb = c.astype(BF16)
    for k in range(C_COPIES):
        c_ref[k] = cb
    lane = lax.broadcasted_iota(jnp.int32, (TM, LANES), 1)
    logits = jnp.where(lane < N_EXPERTS, _dot_hi(c, rt_ref[...]), -jnp.inf)
    m1 = jnp.max(logits, axis=1, keepdims=True)
    i1 = jnp.min(jnp.where(logits == m1, lane, LANES), axis=1, keepdims=True)
    rest = jnp.where(lane == i1, -jnp.inf, logits)
    m2 = jnp.max(rest, axis=1, keepdims=True)
    i2 = jnp.min(jnp.where(rest == m2, lane, LANES), axis=1, keepdims=True)
    e2 = jnp.exp(m2 - m1)
    den = 1.0 + e2
    r_ref[...] = jnp.where(lane == R_I0, i1.astype(F32),
                           jnp.where(lane == R_I1, i2.astype(F32),
                                     jnp.where(lane == R_W0, 1.0 / den,
                                               jnp.where(lane == R_W1, e2 / den, 0.0))))


def _moe_pre_call(h, y, lw):
    t = h.shape[0]
    rows = lambda w: pl.BlockSpec((TM, w), lambda i: (i, 0))
    return pl.pallas_call(
        _moe_pre_kernel,
        grid=(t // TM,),
        in_specs=[rows(D_MODEL), rows(D_MODEL), _resident((D_MODEL, D_MODEL)),
                  _resident((1, D_MODEL)), _resident((D_MODEL, LANES))],
        out_specs=[rows(D_MODEL),
                   pl.BlockSpec((C_COPIES, TM, D_MODEL), lambda i: (0, i, 0)),
                   rows(LANES)],
        out_shape=[jax.ShapeDtypeStruct((t, D_MODEL), F32),
                   jax.ShapeDtypeStruct((C_COPIES, t, D_MODEL), BF16),
                   jax.ShapeDtypeStruct((t, LANES), F32)],
        compiler_params=pltpu.CompilerParams(
            dimension_semantics=("arbitrary",), vmem_limit_bytes=VMEM_LIMIT),
        name="moe_pre",
    )(h, y, lw["w_out"], lw["norm_ffn"], lw["router"])


def _moe_ffn_kernel(be_ref, nb_ref, x_ref, wg_ref, wu_ref, wd_ref, o_ref):
    del be_ref
    i = pl.program_id(0)

    @pl.when(i < nb_ref[0])
    def _compute():
        x = x_ref[...]
        fw = FFN_EXPERT // FFN_SPLIT
        acc = jnp.zeros((TMM, D_MODEL), F32)
        for s in range(FFN_SPLIT):
            g = jnp.dot(x, wg_ref[0, :, fw * s:fw * (s + 1)], preferred_element_type=F32)
            u = jnp.dot(x, wu_ref[0, :, fw * s:fw * (s + 1)], preferred_element_type=F32)
            a = (_silu(g) * u).astype(BF16)
            acc = acc + jnp.dot(a, wd_ref[0, fw * s:fw * (s + 1), :], preferred_element_type=F32)
        o_ref[...] = acc.astype(o_ref.dtype)

    @pl.when(i >= nb_ref[0])
    def _skip():
        o_ref[...] = jnp.zeros_like(o_ref)


def _moe_ffn_call(xs, blk_e, nblk, lw, layer):
    n_rows = xs.shape[0]
    wspec = lambda shape: pl.BlockSpec(shape, lambda i, be, nb: (layer, be[i], 0, 0),
                                       pipeline_mode=pl.Buffered(1))
    grid_spec = pltpu.PrefetchScalarGridSpec(
        num_scalar_prefetch=2,
        grid=(n_rows // TMM,),
        in_specs=[
            pl.BlockSpec((TMM, D_MODEL), lambda i, be, nb: (i, 0)),
            wspec((None, 1, D_MODEL, FFN_EXPERT)), wspec((None, 1, D_MODEL, FFN_EXPERT)),
            wspec((None, 1, FFN_EXPERT, D_MODEL)),
        ],
        out_specs=pl.BlockSpec((TMM, D_MODEL), lambda i, be, nb: (i, 0)),
    )
    return pl.pallas_call(
        _moe_ffn_kernel,
        grid_spec=grid_spec,
        out_shape=jax.ShapeDtypeStruct((n_rows, D_MODEL), BF16),
        compiler_params=pltpu.CompilerParams(
            dimension_semantics=("arbitrary",), vmem_limit_bytes=VMEM_LIMIT),
        name="moe_ffn",
    )(blk_e, nblk, xs, lw["wg"], lw["wu"], lw["wd"])


def _moe_post_kernel(final, h2_ref, y0_ref, y1_ref, r_ref, p_ref, pn_ref, pg_ref, pp_ref, nfin_ref,
                     o_ref):
    r = r_ref[...]
    w0 = r[:, R_W0:R_W0 + 1]
    w1 = r[:, R_W1:R_W1 + 1]
    h3 = h2_ref[...] + (w0 * y0_ref[...].astype(F32) + w1 * y1_ref[...].astype(F32))
    h4 = _ple(h3, p_ref, pn_ref, pg_ref, pp_ref)
    o_ref[...] = _rms(h4, nfin_ref[...]) if final else h4


def _moe_post_call(h2, y0, y1, route, p_all, layer, lw, norm_final, final):
    t = h2.shape[0]
    rows = lambda w: pl.BlockSpec((TM, w), lambda i: (i, 0))
    return pl.pallas_call(
        functools.partial(_moe_post_kernel, final),
        grid=(t // TM,),
        in_specs=[rows(D_MODEL), rows(D_MODEL), rows(D_MODEL), rows(LANES),
                  pl.BlockSpec((None, TM, PLE_DIM), lambda i: (layer, i, 0)),
                  _resident((1, D_MODEL)), _resident((D_MODEL, D_MODEL)),
                  _resident((PLE_DIM, D_MODEL)), _resident((1, D_MODEL))],
        out_specs=rows(D_MODEL),
        out_shape=jax.ShapeDtypeStruct((t, D_MODEL), F32),
        compiler_params=pltpu.CompilerParams(
            dimension_semantics=("arbitrary",), vmem_limit_bytes=VMEM_LIMIT),
        name="moe_post",
    )(h2, y0, y1, route, p_all, lw["ple_norm"], lw["ple_w_gate"], lw["ple_w_proj"], norm_final)


def _route_plan(route, n_rows):
    t = route.shape[0]
    e_flat = route[:, R_I0:R_I1 + 1].astype(jnp.int32).reshape(-1)
    onehot = (jnp.arange(N_EXPERTS, dtype=jnp.int32)[:, None] == e_flat[None, :]).astype(jnp.int32)
    csum = jnp.cumsum(onehot, axis=1)
    counts = csum[:, -1]
    pcounts = ((counts + TMM - 1) // TMM) * TMM
    ends = jnp.cumsum(pcounts)
    starts = ends - pcounts
    pos = jnp.sum(onehot * (csum - 1 + starts[:, None]), axis=0)
    src = (jnp.arange(n_rows, dtype=jnp.int32) % t).at[pos].set(
        jnp.arange(2 * t, dtype=jnp.int32) // 2)
    blk_start = jnp.arange(n_rows // TMM, dtype=jnp.int32) * TMM
    blk_e = jnp.minimum(jnp.sum((blk_start[:, None] >= ends[None, :]).astype(jnp.int32), axis=1),
                        N_EXPERTS - 1).astype(jnp.int32)
    nblk = (ends[-1:] // TMM).astype(jnp.int32)
    return src, pos.reshape(t, 2), blk_e, nblk


def _rope_tables(pos):
    half = HEAD_DIM // 2
    inv_freq = ROPE_BASE ** (-jnp.arange(half, dtype=F32) / half)
    ang = pos[:, None] * inv_freq[None, :]
    cos, sin = jnp.cos(ang), jnp.sin(ang)
    cos_h = jnp.concatenate([cos, cos], axis=-1)
    sin_h = jnp.concatenate([-sin, sin], axis=-1)
    return jnp.tile(cos_h, (1, N_HEADS)), jnp.tile(sin_h, (1, N_HEADS))


def _pad_lanes(x, width=LANES):
    return jnp.pad(x, [(0, 0)] * (x.ndim - 1) + [(0, width - x.shape[-1])])


def kernel(x_prompt, x_sample, state_ret, state_mlstm_c, state_mlstm_n, state_mlstm_m, state_gdn, state_gdn_conv, state_ssm, state_ssm_conv, p_prompt, p_sample, norm_mix, w_in, w_out, ret_norm, mlstm_i_bias, mlstm_f_bias, mlstm_norm, gdn_conv_w, gdn_a_log, gdn_dt_bias, gdn_norm, ssm_conv_w, ssm_conv_b, ssm_dt_bias, ssm_a_log, ssm_d, ssm_norm, norm_ffn, ffn_w_gate, ffn_w_up, ffn_w_down, moe_router, moe_w_gate, moe_w_up, moe_w_down, ple_w_proj, ple_norm, ple_w_gate, norm_final):
    bp, tp = x_prompt.shape[:2]
    bs, ts = x_sample.shape[:2]
    n_prompt = bp * tp
    n_all = n_prompt + bs * ts
    sample_seqs = MIX_ROWS // ts
    assert tp % MIX_ROWS == 0 and bp % PROMPT_SEQS == 0 and n_all % TM == 0
    assert MIX_ROWS % ts == 0 and bs % sample_seqs == 0 and ts >= CONV_W - 1 and DEPTH % 2 == 0
    hd = HEAD_DIM

    h = jnp.concatenate([x_prompt.reshape(n_prompt, D_MODEL), x_sample.reshape(bs * ts, D_MODEL)])
    h = h.astype(F32)
    p_all = jnp.concatenate([p_prompt.reshape(DEPTH, n_prompt, PLE_DIM),
                             p_sample.reshape(DEPTH, bs * ts, PLE_DIM)], axis=1)

    o_ml = 4 * GROUP_W
    o_gdn = o_ml + 4 * GROUP_W + 2 * N_HEADS
    o_ssm = o_gdn + 4 * GROUP_W + 2 * N_HEADS
    gate_cols = jnp.concatenate([
        w_in[:, :, o_ml + 4 * GROUP_W:o_gdn],
        w_in[:, :, o_gdn + 4 * GROUP_W:o_ssm],
        w_in[:, :, o_ssm + 4 * GROUP_W:],
    ], axis=-1)
    w_in_r = jnp.concatenate([
        w_in[:, :, 0:o_ml + 4 * GROUP_W],
        w_in[:, :, o_gdn:o_gdn + 4 * GROUP_W],
        w_in[:, :, o_ssm:o_ssm + 4 * GROUP_W],
        jnp.repeat(gate_cols, hd, axis=-1),
    ], axis=-1)
    assert w_in_r.shape[-1] == D_PROJ
    col = jnp.arange(D_PROJ)
    k_cols = ((col >= C_RET + GROUP_W) & (col < C_RET + 2 * GROUP_W)) | (
        (col >= C_ML + GROUP_W) & (col < C_ML + 2 * GROUP_W))
    w_in_r = (w_in_r * jnp.where(k_cols, hd ** -0.5, 1.0).astype(F32)).astype(BF16)

    zeros4 = jnp.zeros((DEPTH, N_HEADS), F32)
    norms = jnp.stack([ret_norm, mlstm_norm, gdn_norm, ssm_norm], axis=1).astype(F32)
    gbx = jnp.repeat(jnp.stack([mlstm_i_bias, mlstm_f_bias, gdn_dt_bias, ssm_dt_bias, gdn_a_log,
                                ssm_a_log, ssm_d, zeros4], axis=1).astype(F32), hd, axis=-1)

    consts_p = _mixer_consts(1, MIX_ROWS)
    consts_s = _mixer_consts(sample_seqs, ts)
    ret_tabs_p = _ret_tables(1, MIX_ROWS)
    ret_tabs_s = _ret_tables(sample_seqs, ts)
    tab_prompt = _rope_tables(jnp.arange(tp, dtype=F32))
    tab_sample = tuple(jnp.tile(t, (sample_seqs, 1))
                       for t in _rope_tables(PAST_LEN + jnp.arange(ts, dtype=F32)))

    keep = SUBLANES - (CONV_W - 1)
    eye_h = jnp.eye(N_HEADS, dtype=F32)

    def conv_in(s):
        return jnp.pad(s.astype(F32), ((0, 0), (keep, 0), (0, 0)))

    big_states = dict(ret=state_ret.astype(F32), mc=state_mlstm_c.astype(F32),
                      gdn=state_gdn.astype(F32), ssm=state_ssm.astype(F32))

    def sample_states_in(i):
        n_bd = (state_mlstm_n[i].astype(F32)[:, :, :, None, None]
                * eye_h[None, :, None, :, None])
        n_bd = jnp.broadcast_to(n_bd, (bs, N_HEADS, hd, N_HEADS, hd)).reshape(bs, GROUP_W, GROUP_W)
        m_x = jnp.repeat(state_mlstm_m[i].astype(F32), hd, axis=-1)[:, None, :]
        return (big_states["ret"], big_states["mc"], n_bd,
                jnp.broadcast_to(m_x, (bs, SUBLANES, GROUP_W)), big_states["gdn"],
                conv_in(state_gdn_conv[i]), big_states["ssm"], conv_in(state_ssm_conv[i]))

    def n_vec(s_mn):
        return jnp.stack([s_mn[:, hd * hh:hd * (hh + 1), hd * hh] for hh in range(N_HEADS)], axis=1)

    def sample_states_out(st):
        s_ret, s_mc, s_mn, s_mm, s_gdn, s_gconv, s_ssm, s_sconv = st
        big_states.update(ret=s_ret, mc=s_mc, gdn=s_gdn, ssm=s_ssm)
        return (n_vec(s_mn), s_mm[:, 0, ::hd], s_gconv[:, keep:], s_sconv[:, keep:])

    def prompt_states_out(st):
        s_ret, s_mc, s_mn, s_mm, s_gdn, s_gconv, s_ssm, s_sconv = st

        def diag(s):
            return jnp.stack([s[:, hd * hh:hd * (hh + 1), hd * hh:hd * (hh + 1)]
                              for hh in range(N_HEADS)], axis=1)

        per_group = N_HEADS // SSM_GROUPS
        ssm = jnp.stack([s_ssm[:, SSM_STATE * (hh // per_group):SSM_STATE * (hh // per_group + 1),
                               hd * hh:hd * (hh + 1)] for hh in range(N_HEADS)], axis=1)
        return (diag(s_ret), diag(s_mc), n_vec(s_mn), s_mm[:, 0, ::hd], diag(s_gdn),
                s_gconv[:, keep:], ssm, s_sconv[:, keep:])

    n_moe_rows = 2 * n_all + N_EXPERTS * TMM
    moe_wg, moe_wu, moe_wd = (w.astype(BF16) for w in (moe_w_gate, moe_w_up, moe_w_down))
    new_prompt, new_sample = [], []
    for i in range(DEPTH):
        lp = dict(norms=norms[i], gcw=gdn_conv_w[i].astype(F32), scw=ssm_conv_w[i].astype(F32),
                  scb=ssm_conv_b[i].astype(F32)[None, :], gbx=gbx[i])
        proj = _proj_call(h, norm_mix[i].astype(F32)[None, :], w_in_r[i])
        y_p, st_p = _mixer_call(proj, 0, bp, tp, PROMPT_SEQS, 1, True, tab_prompt, lp, consts_p,
                                ret_tabs_p)
        y_s, st_s = _mixer_call(proj, n_prompt, bs, ts, 1, sample_seqs, False, tab_sample, lp,
                                consts_s, ret_tabs_s, sample_states_in(i), layer=i)
        y_all = jnp.concatenate([y_p.reshape(n_prompt, D_MODEL), y_s.reshape(bs * ts, D_MODEL)])
        new_prompt.append(prompt_states_out(st_p))
        new_sample.append(sample_states_out(st_s))

        j = i // 2
        lw = dict(w_out=w_out[i].astype(BF16), norm_ffn=norm_ffn[i].astype(F32)[None, :],
                  ple_norm=ple_norm[i].astype(F32)[None, :], ple_w_gate=ple_w_gate[i].astype(BF16),
                  ple_w_proj=ple_w_proj[i].astype(BF16))
        if i % 2 == 0:
            lw.update(wg=ffn_w_gate[j].astype(BF16), wu=ffn_w_up[j].astype(BF16),
                      wd=ffn_w_down[j].astype(BF16))
            h = _dense_call(h, y_all, p_all, i, lw)
        else:
            lw.update(router=_pad_lanes(moe_router[j].astype(F32)), wg=moe_wg, wu=moe_wu, wd=moe_wd)
            h2, c, route = _moe_pre_call(h, y_all, lw)
            src, pos, blk_e, nblk = _route_plan(route, n_moe_rows)
            src = src + n_all * (jnp.arange(n_moe_rows, dtype=jnp.int32) % C_COPIES)
            xs = jnp.take(c.reshape(C_COPIES * n_all, D_MODEL), src, axis=0, mode="clip")
            ys = _moe_ffn_call(xs, blk_e, nblk, lw, j)
            y0 = jnp.take(ys, pos[:, 0], axis=0, mode="clip")
            y1 = jnp.take(ys, pos[:, 1], axis=0, mode="clip")
            h = _moe_post_call(h2, y0, y1, route, p_all, i, lw, norm_final.astype(F32)[None, :],
                               final=(i == DEPTH - 1))

    y = h.astype(x_prompt.dtype)
    y_prompt = y[:n_prompt].reshape(bp, tp, D_MODEL)
    y_sample = y[n_prompt:].reshape(bs, ts, D_MODEL)
    stack = lambda lst: tuple(jnp.stack([l[k] for l in lst]) for k in range(len(lst[0])))
    sa_n, sa_m, sa_gconv, sa_sconv = stack(new_sample)
    return (y_prompt, y_sample) + stack(new_prompt) + (
        big_states["ret"], big_states["mc"], sa_n, sa_m, big_states["gdn"], sa_gconv,
        big_states["ssm"], sa_sconv)
```

```python
import functools

import jax
import jax.numpy as jnp
from jax import lax
from jax.experimental import pallas as pl
from jax.experimental.pallas import tpu as pltpu

F32 = jnp.float32
BF16 = jnp.bfloat16
HIGHEST = lax.Precision.HIGHEST

D_MODEL = 1024
DEPTH = 4
PAST_LEN = 16384
N_HEADS = 4
HEAD_DIM = 64
GROUP_W = N_HEADS * HEAD_DIM
SSM_STATE = 128
SSM_GROUPS = 2
CONV_W = 4
CONV_CH = 3 * GROUP_W
ROPE_BASE = 10000.0
FFN_DENSE = 2816
N_EXPERTS = 8
FFN_EXPERT = 3584
PLE_DIM = 256
EPS = 1e-6

LANES = 128
SUBLANES = 8
VMEM_LIMIT = 56 * 1024 * 1024

C_RET = 0
C_ML = 1024
C_GDN = 2048
C_SSM = 3072
C_XMI = 4096
C_XMF = C_XMI + GROUP_W
C_XGB = C_XMF + GROUP_W
C_XGA = C_XGB + GROUP_W
C_XDT = C_XGA + GROUP_W
D_PROJ = C_XDT + GROUP_W

MIX_ROWS = 64
PROMPT_SEQS = 8
TM = 512
TMM = 512
FFN_SPLIT = 2


def _dot_hi(a, b):
    return jnp.dot(a, b, precision=HIGHEST, preferred_element_type=F32)


def _sigmoid(x):
    return 1.0 / (1.0 + jnp.exp(-x))


def _silu(x):
    return x * _sigmoid(x)


def _softplus(x):
    return jnp.maximum(x, 0.0) + jnp.log(1.0 + jnp.exp(-jnp.abs(x)))


def _rms(x, g):
    return x * lax.rsqrt(jnp.mean(x * x, -1, keepdims=True) + EPS) * g


def _proj_kernel(h_ref, g_ref, w_ref, o_ref):
    a = _rms(h_ref[...], g_ref[...])
    o_ref[...] = jnp.dot(a.astype(BF16), w_ref[...], preferred_element_type=F32)


def _proj_call(h, g, w):
    t = h.shape[0]
    return pl.pallas_call(
        _proj_kernel,
        grid=(t // TM,),
        in_specs=[
            pl.BlockSpec((TM, D_MODEL), lambda i: (i, 0)),
            pl.BlockSpec((1, D_MODEL), lambda i: (0, 0)),
            pl.BlockSpec((D_MODEL, D_PROJ), lambda i: (0, 0), pipeline_mode=pl.Buffered(1)),
        ],
        out_specs=pl.BlockSpec((TM, D_PROJ), lambda i: (i, 0)),
        out_shape=jax.ShapeDtypeStruct((t, D_PROJ), F32),
        compiler_params=pltpu.CompilerParams(
            dimension_semantics=("arbitrary",), vmem_limit_bytes=VMEM_LIMIT),
        name="proj",
    )(h, g, w)


def _split3(x):
    hi = x.astype(BF16)
    r1 = x - hi.astype(F32)
    mid = r1.astype(BF16)
    lo = (r1 - mid.astype(F32)).astype(BF16)
    return hi, mid, lo


def _dot_tn(a, b):
    return lax.dot_general(a, b, (((0,), (0,)), ((), ())), preferred_element_type=F32)


def _mixer_kernel(nb, G, T, carry, *refs):
    R = G * T
    hd = HEAD_DIM
    n_levels = T.bit_length() - 1
    per_group = N_HEADS // SSM_GROUPS
    proj_refs = refs[:nb]
    (cos_ref, sin_ref, gb_ref, norms_ref, gcw_ref, scw_ref, scb_ref, mbd_b_ref, mbd_f_ref,
     m2_b_ref, m2_f_ref, pm_ref, lm_ref, tril3_ref, rt_ref, rs_ref) = refs[nb:nb + 16]
    rest = refs[nb + 16:]
    if carry:
        (y_ref, o_ret, o_mc, o_mn, o_mm, o_gdn, o_gconv, o_ssm, o_sconv, xg_ref, xs_ref) = rest
    else:
        (i_ret, i_mc, i_mn, i_mm, i_gdn, i_gconv, i_ssm, i_sconv) = rest[:8]
        (y_ref, o_ret, o_mc, o_mn, o_mm, o_gdn, o_gconv, o_ssm, o_sconv, xg_ref, xs_ref) = rest[12:]

    if carry:
        @pl.when(pl.program_id(1) == 0)
        def _init():
            for r in (o_ret, o_mc, o_mn, o_mm, o_gdn, o_ssm, xg_ref, xs_ref):
                r[...] = jnp.zeros_like(r)

    def to_bd(s4):
        rows = []
        for h in range(N_HEADS):
            parts = [jnp.zeros((hd, hd * h), F32)] if h else []
            parts.append(s4[h])
            if h < N_HEADS - 1:
                parts.append(jnp.zeros((hd, hd * (N_HEADS - 1 - h)), F32))
            rows.append(jnp.concatenate(parts, axis=1))
        return jnp.concatenate(rows, axis=0)

    def to_st(s4):
        rows = []
        for g in range(SSM_GROUPS):
            parts = [jnp.zeros((SSM_STATE, SSM_STATE * g), F32)] if g else []
            parts += [s4[per_group * g + k] for k in range(per_group)]
            if g < SSM_GROUPS - 1:
                parts.append(jnp.zeros((SSM_STATE, SSM_STATE * (SSM_GROUPS - 1 - g)), F32))
            rows.append(jnp.concatenate(parts, axis=1))
        return jnp.concatenate(rows, axis=0)

    out_refs = dict(ret=o_ret, mc=o_mc, mn=o_mn, gdn=o_gdn, ssm=o_ssm)

    def live_blocks(kind):
        if kind == "ssm":
            return [(SSM_STATE * g, SSM_STATE, SSM_STATE * g, SSM_STATE) for g in range(SSM_GROUPS)]
        return [(hd * h, hd, hd * h, hd) for h in range(N_HEADS)]

    if carry:
        def get(kind, j, i):
            return out_refs[kind][j]

        def update(kind, j, i, dec, dlt):
            ref = out_refs[kind]
            for r0, nr, c0, nc in live_blocks(kind):
                ref[j, r0:r0 + nr, c0:c0 + nc] = (ref[j, r0:r0 + nr, c0:c0 + nc] * dec[:, c0:c0 + nc]
                                                  + dlt[r0:r0 + nr, c0:c0 + nc])

        def get_m(j, i):
            return o_mm[j, 0:1, :]

        def put_m(j, i, row):
            o_mm[j] = jnp.broadcast_to(row, (SUBLANES, GROUP_W))
    else:
        in_refs = dict(ret=i_ret, mc=i_mc, gdn=i_gdn, ssm=i_ssm)

        def get(kind, j, i):
            if kind == "mn":
                return i_mn[i]
            if kind == "ssm":
                return to_st(i_ssm[i])
            return to_bd(in_refs[kind][i])

        def update(kind, j, i, dec, dlt):
            if kind == "mn":
                o_mn[i] = i_mn[i] * dec + mbd_f * dlt
                return
            for h in range(N_HEADS):
                r0, nr = (SSM_STATE * (h // per_group), SSM_STATE) if kind == "ssm" else (hd * h, hd)
                out_refs[kind][i, h] = (in_refs[kind][i, h] * dec[:, hd * h:hd * (h + 1)]
                                        + dlt[r0:r0 + nr, hd * h:hd * (h + 1)])

        def get_m(j, i):
            return i_mm[i, 0:1, :]

        def put_m(j, i, row):
            o_mm[i] = jnp.broadcast_to(row, (SUBLANES, GROUP_W))

    def seq_rows(x, i):
        return x if G == 1 else x[T * i:T * (i + 1)]

    def seq_row(x, i):
        return x if x.shape[0] == 1 else x[T * i:T * i + 1]

    def seg_last(x):
        if G == 1:
            return x[R - 1:R]
        last = x.reshape(G, T, x.shape[1])[:, T - 1:T, :]
        return jnp.broadcast_to(last, (G, T, x.shape[1])).reshape(R, x.shape[1])

    def seg_max0(x):
        if G == 1:
            return jnp.max(x, axis=0, keepdims=True)
        mx = jnp.max(x.reshape(G, T, x.shape[1]), axis=1, keepdims=True)
        return jnp.broadcast_to(mx, (G, T, x.shape[1])).reshape(R, x.shape[1])

    def m_rows(j):
        if G == 1:
            return get_m(j, 0)
        return jnp.concatenate([jnp.broadcast_to(get_m(j, i), (T, GROUP_W)) for i in range(G)],
                               axis=0)

    mbd_b = mbd_b_ref[...]
    mbd_f = mbd_f_ref[...]
    m2_b = m2_b_ref[...]
    m2_f = m2_f_ref[...]
    eye_p = pm_ref[0]
    ut_p = pm_ref[1]
    ninf_p = pm_ref[2]
    tril3 = tril3_ref[...]
    gb = gb_ref[...]
    norms = norms_ref[...]
    cosv = cos_ref[...]
    sinv = sin_ref[...]
    lane_w = lax.broadcasted_iota(jnp.int32, (R, GROUP_W), 1)
    first_half = (lane_w % hd) < (hd // 2)
    ones_b = jnp.ones((R, GROUP_W), BF16)
    neg_a_gdn = -jnp.exp(gb[4:5, :])
    neg_a_ssm = -jnp.exp(gb[5:6, :])
    inv_hd = 1.0 / hd

    def bf(x):
        return x.astype(BF16)

    def bd(xb, mask):
        return jnp.concatenate([xb] * (GROUP_W // R), axis=0) * mask

    def mm(a, b):
        return jnp.dot(a, b, preferred_element_type=F32)

    def mm_nt(a, b):
        return lax.dot_general(a, b, (((1,), (1,)), ((), ())), preferred_element_type=F32)

    def mm_state(a, states):
        outs = [mm(seq_rows(a, i), bf(states[i])) for i in range(G)]
        return outs[0] if G == 1 else jnp.concatenate(outs, axis=0)

    def delta(a, b, i):
        return _dot_tn(seq_rows(a, i), seq_rows(b, i))

    def cumsum(g):
        return mm(tril3, jnp.concatenate(_split3(g), axis=0))

    def sum0(x):
        return jnp.sum(x, axis=0, keepdims=True)

    def segmax(x):
        parts = []
        for h in range(N_HEADS):
            mh = jnp.max(x[:, hd * h:hd * (h + 1)], axis=1, keepdims=True)
            parts.append(jnp.broadcast_to(mh, (R, hd)))
        return jnp.concatenate(parts, axis=1)

    def rope(x):
        swapped = jnp.where(first_half, pltpu.roll(x, GROUP_W - hd // 2, 1),
                            pltpu.roll(x, hd // 2, 1))
        return x * cosv + swapped * sinv

    def conv4(x_ref, j, x, w, hist_in, hist_out):
        if not carry:
            x_ref[j, :, 0:SUBLANES, :] = hist_in[...]
        x_ref[j, :, SUBLANES:SUBLANES + T, :] = x.reshape(G, T, CONV_CH)
        acc = w[0:1, :] * x_ref[j, :, pl.ds(SUBLANES - 3, T), :]
        for k in range(1, CONV_W):
            acc = acc + w[k:k + 1, :] * x_ref[j, :, pl.ds(SUBLANES - 3 + k, T), :]
        hist = x_ref[j, :, T:T + SUBLANES, :]
        if carry:
            hist_out[j] = hist[0]
            x_ref[j, :, 0:SUBLANES, :] = hist
        else:
            hist_out[...] = hist
        return acc.reshape(R, CONV_CH)


    def ret_chain(j):
        P = proj_refs[j]
        rq = rope(P[:, C_RET:C_RET + GROUP_W])
        rk = rope(P[:, C_RET + GROUP_W:C_RET + 2 * GROUP_W])
        rvb = bf(P[:, C_RET + 2 * GROUP_W:C_RET + 3 * GROUP_W])
        S = [get("ret", j, i) for i in range(G)]
        scores = mm_nt(bf(rq), bd(bf(rk), mbd_b)) * rt_ref[0]
        o_inter = mm_state(bf(rq * rt_ref[1]), S)
        yield
        ke = bf(rk * rt_ref[2])
        for i in range(G):
            update("ret", j, i, rs_ref[0:1, :], delta(ke, rvb, i))
        o = mm(bf(scores), bd(rvb, mbd_b)) + o_inter
        yield
        mean = mm(bf(o), mbd_b) * inv_hd
        yield
        xc = o - mean
        var = mm(bf(xc * xc), mbd_b) * inv_hd
        yield
        ro = xc * lax.rsqrt(var + EPS) * norms[0:1, :]
        rg = P[:, C_RET + 3 * GROUP_W:C_RET + 4 * GROUP_W]
        y_ref[j, :, 0:GROUP_W] = (_silu(rg) * ro).astype(y_ref.dtype)

    def mlstm_chain(j):
        P = proj_refs[j]
        mq = bf(P[:, C_ML:C_ML + GROUP_W])
        mk = P[:, C_ML + GROUP_W:C_ML + 2 * GROUP_W]
        mvb = bf(P[:, C_ML + 2 * GROUP_W:C_ML + 3 * GROUP_W])
        i_x = P[:, C_XMI:C_XMI + GROUP_W] + gb[0:1, :]
        g_f = -_softplus(-(P[:, C_XMF:C_XMF + GROUP_W] + gb[1:2, :]))
        b_col = cumsum(g_f)
        qk = mm_nt(mq, bd(bf(mk), mbd_b))
        C = [get("mc", j, i) for i in range(G)]
        N = [get("mn", j, i) for i in range(G)]
        q_c = mm_state(mq, C)
        q_n = mm_state(mq, N)
        yield
        b_row = sum0(g_f * ut_p)
        i_row = sum0(i_x * eye_p)
        b_last = seg_last(b_col)
        m_prev = m_rows(j)
        dd = b_col - b_row + i_row
        inter = b_col + m_prev
        m_row = jnp.maximum(segmax(dd + ninf_p), inter)
        w_intra = jnp.exp(jnp.minimum(dd - m_row, ninf_p))
        w_state = jnp.exp(inter - m_row)
        sb = bf(qk * w_intra)
        num = mm(sb, bd(mvb, mbd_b)) + w_state * q_c
        den = mm(sb, mbd_b) + w_state * q_n
        g_end = b_last - b_col + i_x
        m_new = jnp.maximum(b_last + m_prev, seg_max0(g_end))
        w_c = jnp.exp(b_last + m_prev - m_new)
        kwb = bf(mk * jnp.exp(g_end - m_new))
        for i in range(G):
            wc_i = seq_row(w_c, i)
            update("mc", j, i, wc_i, delta(kwb, mvb, i))
            update("mn", j, i, wc_i, delta(kwb, ones_b, i))
            put_m(j, i, seq_row(m_new, i))
        yield
        hh = num / jnp.maximum(jnp.abs(den), jnp.exp(-m_row))
        mean = mm(bf(hh), mbd_b) * inv_hd
        yield
        xc = hh - mean
        var = mm(bf(xc * xc), mbd_b) * inv_hd
        yield
        mh = xc * lax.rsqrt(var + EPS) * norms[1:2, :]
        mo = P[:, C_ML + 3 * GROUP_W:C_ML + 4 * GROUP_W]
        y_ref[j, :, GROUP_W:2 * GROUP_W] = (_sigmoid(mo) * mh).astype(y_ref.dtype)

    def gdn_chain(j):
        P = proj_refs[j]
        conv = conv4(xg_ref, j, P[:, C_GDN:C_GDN + CONV_CH], gcw_ref[...],
                     None if carry else i_gconv, o_gconv)
        act = _silu(conv)
        gq, gk, gv = act[:, 0:GROUP_W], act[:, GROUP_W:2 * GROUP_W], act[:, 2 * GROUP_W:]
        q_ss = mm(bf(gq * gq), mbd_b)
        k_ss = mm(bf(gk * gk), mbd_b)
        g_g = neg_a_gdn * _softplus(P[:, C_XGA:C_XGA + GROUP_W] + gb[2:3, :])
        b_col = cumsum(g_g)
        yield
        qn = gq * lax.rsqrt(q_ss + EPS) * hd ** -0.5
        kn = gk * lax.rsqrt(k_ss + EPS)
        beta = _sigmoid(P[:, C_XGB:C_XGB + GROUP_W])
        b_row = sum0(g_g * ut_p)
        b_last = seg_last(b_col)
        e_col = jnp.exp(b_col)
        decay = jnp.exp(jnp.minimum(b_col - b_row, ninf_p))
        kb = kn * beta
        bd_k = bd(bf(kn), mbd_b)
        kk = mm_nt(bf(kb), bd_k)
        qk = mm_nt(bf(qn), bd_k)
        S = [get("gdn", j, i) for i in range(G)]
        q_s = mm_state(bf(qn), S)
        yield
        a_low = kk * decay
        attn = qk * decay
        t_inv = eye_p - a_low * lm_ref[0]
        for lev in range(1, n_levels):
            m1 = mm(bf(a_low * lm_ref[lev]), bd(bf(t_inv), mbd_b))
            yield
            t_inv = t_inv - mm(bf(t_inv), bd(bf(m1), mbd_b))
            yield
        tb = bf(t_inv)
        u = mm(tb, bd(bf(gv * beta), mbd_b))
        w = mm(tb, bd(bf(kb * e_col), mbd_b))
        yield
        v_new = u - mm_state(bf(w), S)
        yield
        vnb = bf(v_new)
        o = e_col * q_s + mm(bf(attn), bd(vnb, mbd_b))
        keb = bf(kn * jnp.exp(b_last - b_col))
        s_dec = jnp.exp(b_last)
        for i in range(G):
            update("gdn", j, i, seq_row(s_dec, i), delta(keb, vnb, i))
        yield
        o_ss = mm(bf(o * o), mbd_b)
        yield
        go = o * lax.rsqrt(o_ss * inv_hd + EPS) * norms[2:3, :]
        gz = P[:, C_GDN + CONV_CH:C_GDN + CONV_CH + GROUP_W]
        y_ref[j, :, 2 * GROUP_W:3 * GROUP_W] = (go * _silu(gz)).astype(y_ref.dtype)

    def ssd_chain(j):
        P = proj_refs[j]
        conv = conv4(xs_ref, j, P[:, C_SSM + GROUP_W:C_SSM + GROUP_W + CONV_CH], scw_ref[...],
                     None if carry else i_sconv, o_sconv) + scb_ref[...]
        act = _silu(conv)
        sx = act[:, 0:GROUP_W]
        s_b = bf(act[:, GROUP_W:2 * GROUP_W])
        s_c = bf(act[:, 2 * GROUP_W:])
        dt = _softplus(P[:, C_XDT:C_XDT + GROUP_W] + gb[3:4, :])
        g_s = dt * neg_a_ssm
        b_col = cumsum(g_s)
        cb = mm_nt(s_c, bd(s_b, m2_b))
        S = [get("ssm", j, i) for i in range(G)]
        c_s = mm_state(s_c, S)
        yield
        b_row = sum0(g_s * ut_p)
        b_last = seg_last(b_col)
        decay = jnp.exp(jnp.minimum(b_col - b_row, ninf_p))
        v = sx * dt
        o = mm(bf(cb * decay), bd(bf(v), mbd_b)) + jnp.exp(b_col) * c_s
        v_end = bf(v * jnp.exp(b_last - b_col))
        s_dec = jnp.exp(b_last)
        for i in range(G):
            update("ssm", j, i, seq_row(s_dec, i), delta(s_b, v_end, i))
        yield
        sz = P[:, C_SSM:C_SSM + GROUP_W]
        yv = (o + gb[6:7, :] * sx) * _silu(sz)
        ms = mm(bf(yv * yv), m2_b) * (1.0 / (GROUP_W // SSM_GROUPS))
        yield
        y_ref[j, :, 3 * GROUP_W:] = (yv * lax.rsqrt(ms + EPS) * norms[3:4, :]).astype(y_ref.dtype)

    chains = [chain(j) for chain in (gdn_chain, mlstm_chain, ret_chain, ssd_chain)
              for j in range(nb)]
    while chains:
        alive = []
        for ch in chains:
            try:
                next(ch)
                alive.append(ch)
            except StopIteration:
                pass
        chains = alive


def _mixer_consts(G, T):
    R = G * T
    n_levels = T.bit_length() - 1
    r = jnp.arange(GROUP_W)
    mbd = (r[:, None] // HEAD_DIM == r[None, :] // HEAD_DIM)
    half = GROUP_W // SSM_GROUPS
    m2 = (r[:, None] // half == r[None, :] // half)
    row = jnp.arange(R)[:, None]
    col = (r % R)[None, :]
    same = (row // T) == (col // T)
    tl, ts = row % T, col % T
    causal = same & (tl >= ts)
    pm = jnp.stack([(row == col).astype(F32), (same & (tl <= ts)).astype(F32),
                    jnp.where(causal, 0.0, -jnp.inf).astype(F32)])
    lm = jnp.stack([same & ((tl >> (j + 1)) == (ts >> (j + 1))) & (((tl >> j) & 1) == 1)
                    & (((ts >> j) & 1) == 0) for j in range(n_levels)]).astype(F32)
    tril = causal[:, :R].astype(BF16)
    return dict(mbd_b=mbd.astype(BF16), mbd_f=mbd.astype(F32), m2_b=m2.astype(BF16),
                m2_f=m2.astype(F32), pm=pm, lm=lm, tril3=jnp.concatenate([tril] * 3, axis=1))


def _ret_tables(G, T):
    R = G * T
    log_gamma = jnp.log1p(-jnp.exp2(-5.0 - jnp.arange(N_HEADS, dtype=F32)))
    b = jnp.cumsum(jnp.broadcast_to(log_gamma[:, None], (N_HEADS, T)), axis=1)
    b_r = jnp.tile(b, (1, G))
    pos = jnp.arange(R)
    causal = ((pos[:, None] // T) == (pos[None, :] // T)) & ((pos[:, None] % T) >= (pos[None, :] % T))
    diff = b_r[:, :, None] - b_r[:, None, :]
    dec = jnp.exp(jnp.where(causal[None], diff, -jnp.inf))
    dec_p = dec.transpose(1, 0, 2).reshape(R, GROUP_W)
    expand = lambda x: jnp.repeat(x.T, HEAD_DIM, axis=1)
    q_scale = expand(jnp.exp(b_r))
    k_scale = expand(jnp.exp(b[:, -1:] - b_r))
    s_dec = jnp.repeat(jnp.exp(b[:, -1]), HEAD_DIM)[None, :]
    return (jnp.stack([dec_p, q_scale, k_scale]),
            jnp.broadcast_to(s_dec, (SUBLANES, GROUP_W)))


def _mixer_call(proj, row0, n_seq, seq_len, nb, G, carry, tables, lp, consts, ret_tabs,
                states=(), layer=0):
    R = MIX_ROWS
    T = R // G
    n_chunks = seq_len // T
    n_levels = T.bit_length() - 1
    blk0 = row0 // R
    cos_t, sin_t = tables
    const2 = lambda g, c: (0, 0)
    const3 = lambda g, c: (0, 0, 0)
    st3 = lambda g, c: (g, 0, 0)
    st4 = lambda g, c: (g, 0, 0, 0)
    if carry:
        proj_specs = [pl.BlockSpec((R, D_PROJ), functools.partial(
            lambda g, c, j: (blk0 + (g * nb + j) * n_chunks + c, 0), j=j)) for j in range(nb)]
        tab_map = lambda g, c: (c, 0)
        n_steps = n_seq // nb
    else:
        assert nb == 1 and n_chunks == 1
        proj_specs = [pl.BlockSpec((R, D_PROJ), lambda g, c: (blk0 + g, 0))]
        tab_map = const2
        n_steps = n_seq // G
    sq = pl.BlockSpec((GROUP_W, GROUP_W), const2)
    in_specs = proj_specs + [
        pl.BlockSpec((R, GROUP_W), tab_map),
        pl.BlockSpec((R, GROUP_W), tab_map),
        pl.BlockSpec((SUBLANES, GROUP_W), const2),
        pl.BlockSpec((N_HEADS, GROUP_W), const2),
        pl.BlockSpec((CONV_W, CONV_CH), const2),
        pl.BlockSpec((CONV_W, CONV_CH), const2),
        pl.BlockSpec((1, CONV_CH), const2),
        sq, sq, sq, sq,
        pl.BlockSpec((3, R, GROUP_W), const3),
        pl.BlockSpec((n_levels, R, GROUP_W), const3),
        pl.BlockSpec((R, 3 * R), const2),
        pl.BlockSpec((3, R, GROUP_W), const3),
        pl.BlockSpec((SUBLANES, GROUP_W), const2),
    ]
    ns = nb if carry else G
    big = pl.BlockSpec((ns, GROUP_W, GROUP_W), st3)
    hist = pl.BlockSpec((ns, SUBLANES, CONV_CH), st3)
    mrow = pl.BlockSpec((ns, SUBLANES, GROUP_W), st3)
    sq_shape = jax.ShapeDtypeStruct((n_seq, GROUP_W, GROUP_W), F32)
    hist_shape = jax.ShapeDtypeStruct((n_seq, SUBLANES, CONV_CH), F32)
    m_shape = jax.ShapeDtypeStruct((n_seq, SUBLANES, GROUP_W), F32)
    if carry:
        state_specs = [big, big, big, mrow, big, hist, big, hist]
        state_shapes = [sq_shape, sq_shape, sq_shape, m_shape, sq_shape, hist_shape, sq_shape,
                        hist_shape]
        y_spec = pl.BlockSpec((nb, R, D_MODEL), lambda g, c: (g, c, 0))
        y_shape = jax.ShapeDtypeStruct((n_seq, seq_len, D_MODEL), BF16)
        state_in_specs = []
        aliases = {}
    else:
        st5 = lambda g, c: (layer, g, 0, 0, 0)
        nat = pl.BlockSpec((None, G, N_HEADS, HEAD_DIM, HEAD_DIM), st5)
        nat_ssm = pl.BlockSpec((None, G, N_HEADS, SSM_STATE, HEAD_DIM), st5)
        nat_shape = jax.ShapeDtypeStruct((DEPTH, n_seq, N_HEADS, HEAD_DIM, HEAD_DIM), F32)
        ssm_shape = jax.ShapeDtypeStruct((DEPTH, n_seq, N_HEADS, SSM_STATE, HEAD_DIM), F32)
        state_specs = [nat, nat, big, mrow, nat, hist, nat_ssm, hist]
        state_shapes = [nat_shape, nat_shape, sq_shape, m_shape, nat_shape, hist_shape, ssm_shape,
                        hist_shape]
        y_spec = pl.BlockSpec((1, R, D_MODEL), lambda g, c: (g, 0, 0))
        y_shape = jax.ShapeDtypeStruct((n_seq // G, R, D_MODEL), BF16)
        state_in_specs = state_specs + [pl.BlockSpec(memory_space=pl.ANY)] * 4
        n_fixed = len(in_specs) + len(state_specs)
        aliases = {n_fixed + q: 1 + k for q, k in enumerate((0, 1, 4, 6))}
    res = pl.pallas_call(
        functools.partial(_mixer_kernel, nb, G, T, carry),
        grid=(n_steps, n_chunks),
        in_specs=in_specs + state_in_specs,
        out_specs=[y_spec] + state_specs,
        out_shape=[y_shape] + state_shapes,
        scratch_shapes=[pltpu.VMEM((nb, G, T + SUBLANES, CONV_CH), F32),
                        pltpu.VMEM((nb, G, T + SUBLANES, CONV_CH), F32)],
        input_output_aliases=aliases,
        compiler_params=pltpu.CompilerParams(
            dimension_semantics=("arbitrary", "arbitrary"), vmem_limit_bytes=VMEM_LIMIT),
        name="mixer_prompt" if carry else "mixer_sample",
    )(*([proj] * nb), cos_t, sin_t, lp["gbx"], lp["norms"], lp["gcw"], lp["scw"], lp["scb"],
      consts["mbd_b"], consts["mbd_f"], consts["m2_b"], consts["m2_f"], consts["pm"],
      consts["lm"], consts["tril3"], ret_tabs[0], ret_tabs[1], *states)
    return res[0], res[1:]


def _ple(h3, p_ref, pn_ref, pg_ref, pp_ref):
    r = _rms(h3, pn_ref[...])
    gate = _sigmoid(jnp.dot(r.astype(BF16), pg_ref[...], preferred_element_type=F32))
    proj = jnp.dot(p_ref[...].astype(BF16), pp_ref[...], preferred_element_type=F32)
    return h3 + gate * proj


def _dense_kernel(h_ref, y_ref, wo_ref, nf_ref, wg_ref, wu_ref, wd_ref, p_ref, pn_ref, pg_ref,
                  pp_ref, o_ref):
    h2 = h_ref[...] + jnp.dot(y_ref[...], wo_ref[...], preferred_element_type=F32)
    c = _rms(h2, nf_ref[...]).astype(BF16)
    fw = FFN_DENSE // FFN_SPLIT
    h3 = h2
    for s in range(FFN_SPLIT):
        g = jnp.dot(c, wg_ref[:, fw * s:fw * (s + 1)], preferred_element_type=F32)
        u = jnp.dot(c, wu_ref[:, fw * s:fw * (s + 1)], preferred_element_type=F32)
        a = (_silu(g) * u).astype(BF16)
        h3 = h3 + jnp.dot(a, wd_ref[fw * s:fw * (s + 1), :], preferred_element_type=F32)
    o_ref[...] = _ple(h3, p_ref, pn_ref, pg_ref, pp_ref)


def _resident(shape):
    return pl.BlockSpec(shape, lambda i: (0,) * len(shape), pipeline_mode=pl.Buffered(1))


def _dense_call(h, y, p_all, layer, lw):
    t = h.shape[0]
    rows = lambda w: pl.BlockSpec((TM, w), lambda i: (i, 0))
    return pl.pallas_call(
        _dense_kernel,
        grid=(t // TM,),
        in_specs=[
            rows(D_MODEL), rows(D_MODEL),
            _resident((D_MODEL, D_MODEL)), _resident((1, D_MODEL)),
            _resident((D_MODEL, FFN_DENSE)), _resident((D_MODEL, FFN_DENSE)),
            _resident((FFN_DENSE, D_MODEL)),
            pl.BlockSpec((None, TM, PLE_DIM), lambda i: (layer, i, 0)),
            _resident((1, D_MODEL)), _resident((D_MODEL, D_MODEL)), _resident((PLE_DIM, D_MODEL)),
        ],
        out_specs=rows(D_MODEL),
        out_shape=jax.ShapeDtypeStruct((t, D_MODEL), F32),
        compiler_params=pltpu.CompilerParams(
            dimension_semantics=("arbitrary",), vmem_limit_bytes=VMEM_LIMIT),
        name="dense_ffn",
    )(h, y, lw["w_out"], lw["norm_ffn"], lw["wg"], lw["wu"], lw["wd"], p_all,
      lw["ple_norm"], lw["ple_w_gate"], lw["ple_w_proj"])


R_I0, R_I1, R_W0, R_W1 = 0, 1, 2, 3
C_COPIES = 2


def _moe_pre_kernel(h_ref, y_ref, wo_ref, nf_ref, rt_ref, h2_ref, c_ref, r_ref):
    h2 = h_ref[...] + jnp.dot(y_ref[...], wo_ref[...], preferred_element_type=F32)
    h2_ref[...] = h2
    c = _rms(h2, nf_ref[...])
    cb = c.astype(BF16)
    for k in range(C_COPIES):
        c_ref[k] = cb
    lane = lax.broadcasted_iota(jnp.int32, (TM, LANES), 1)
    logits = jnp.where(lane < N_EXPERTS, _dot_hi(c, rt_ref[...]), -jnp.inf)
    m1 = jnp.max(logits, axis=1, keepdims=True)
    i1 = jnp.min(jnp.where(logits == m1, lane, LANES), axis=1, keepdims=True)
    rest = jnp.where(lane == i1, -jnp.inf, logits)
    m2 = jnp.max(rest, axis=1, keepdims=True)
    i2 = jnp.min(jnp.where(rest == m2, lane, LANES), axis=1, keepdims=True)
    e2 = jnp.exp(m2 - m1)
    den = 1.0 + e2
    r_ref[...] = jnp.where(lane == R_I0, i1.astype(F32),
                           jnp.where(lane == R_I1, i2.astype(F32),
                                     jnp.where(lane == R_W0, 1.0 / den,
                                               jnp.where(lane == R_W1, e2 / den, 0.0))))


def _moe_pre_call(h, y, lw):
    t = h.shape[0]
    rows = lambda w: pl.BlockSpec((TM, w), lambda i: (i, 0))
    return pl.pallas_call(
        _moe_pre_kernel,
        grid=(t // TM,),
        in_specs=[rows(D_MODEL), rows(D_MODEL), _resident((D_MODEL, D_MODEL)),
                  _resident((1, D_MODEL)), _resident((D_MODEL, LANES))],
        out_specs=[rows(D_MODEL),
                   pl.BlockSpec((C_COPIES, TM, D_MODEL), lambda i: (0, i, 0)),
                   rows(LANES)],
        out_shape=[jax.ShapeDtypeStruct((t, D_MODEL), F32),
                   jax.ShapeDtypeStruct((C_COPIES, t, D_MODEL), BF16),
                   jax.ShapeDtypeStruct((t, LANES), F32)],
        compiler_params=pltpu.CompilerParams(
            dimension_semantics=("arbitrary",), vmem_limit_bytes=VMEM_LIMIT),
        name="moe_pre",
    )(h, y, lw["w_out"], lw["norm_ffn"], lw["router"])


def _moe_ffn_kernel(be_ref, nb_ref, x_ref, wg_ref, wu_ref, wd_ref, o_ref):
    del be_ref
    i = pl.program_id(0)

    @pl.when(i < nb_ref[0])
    def _compute():
        x = x_ref[...]
        fw = FFN_EXPERT // FFN_SPLIT
        acc = jnp.zeros((TMM, D_MODEL), F32)
        for s in range(FFN_SPLIT):
            g = jnp.dot(x, wg_ref[0, :, fw * s:fw * (s + 1)], preferred_element_type=F32)
            u = jnp.dot(x, wu_ref[0, :, fw * s:fw * (s + 1)], preferred_element_type=F32)
            a = (_silu(g) * u).astype(BF16)
            acc = acc + jnp.dot(a, wd_ref[0, fw * s:fw * (s + 1), :], preferred_element_type=F32)
        o_ref[...] = acc.astype(o_ref.dtype)

    @pl.when(i >= nb_ref[0])
    def _skip():
        o_ref[...] = jnp.zeros_like(o_ref)


def _moe_ffn_call(xs, blk_e, nblk, lw, layer):
    n_rows = xs.shape[0]
    wspec = lambda shape: pl.BlockSpec(shape, lambda i, be, nb: (layer, be[i], 0, 0),
                                       pipeline_mode=pl.Buffered(1))
    grid_spec = pltpu.PrefetchScalarGridSpec(
        num_scalar_prefetch=2,
        grid=(n_rows // TMM,),
        in_specs=[
            pl.BlockSpec((TMM, D_MODEL), lambda i, be, nb: (i, 0)),
            wspec((None, 1, D_MODEL, FFN_EXPERT)), wspec((None, 1, D_MODEL, FFN_EXPERT)),
            wspec((None, 1, FFN_EXPERT, D_MODEL)),
        ],
        out_specs=pl.BlockSpec((TMM, D_MODEL), lambda i, be, nb: (i, 0)),
    )
    return pl.pallas_call(
        _moe_ffn_kernel,
        grid_spec=grid_spec,
        out_shape=jax.ShapeDtypeStruct((n_rows, D_MODEL), BF16),
        compiler_params=pltpu.CompilerParams(
            dimension_semantics=("arbitrary",), vmem_limit_bytes=VMEM_LIMIT),
        name="moe_ffn",
    )(blk_e, nblk, xs, lw["wg"], lw["wu"], lw["wd"])


def _moe_post_kernel(final, h2_ref, y0_ref, y1_ref, r_ref, p_ref, pn_ref, pg_ref, pp_ref, nfin_ref,
                     o_ref):
    r = r_ref[...]
    w0 = r[:, R_W0:R_W0 + 1]
    w1 = r[:, R_W1:R_W1 + 1]
    h3 = h2_ref[...] + (w0 * y0_ref[...].astype(F32) + w1 * y1_ref[...].astype(F32))
    h4 = _ple(h3, p_ref, pn_ref, pg_ref, pp_ref)
    o_ref[...] = _rms(h4, nfin_ref[...]) if final else h4


def _moe_post_call(h2, y0, y1, route, p_all, layer, lw, norm_final, final):
    t = h2.shape[0]
    rows = lambda w: pl.BlockSpec((TM, w), lambda i: (i, 0))
    return pl.pallas_call(
        functools.partial(_moe_post_kernel, final),
        grid=(t // TM,),
        in_specs=[rows(D_MODEL), rows(D_MODEL), rows(D_MODEL), rows(LANES),
                  pl.BlockSpec((None, TM, PLE_DIM), lambda i: (layer, i, 0)),
                  _resident((1, D_MODEL)), _resident((D_MODEL, D_MODEL)),
                  _resident((PLE_DIM, D_MODEL)), _resident((1, D_MODEL))],
        out_specs=rows(D_MODEL),
        out_shape=jax.ShapeDtypeStruct((t, D_MODEL), F32),
        compiler_params=pltpu.CompilerParams(
            dimension_semantics=("arbitrary",), vmem_limit_bytes=VMEM_LIMIT),
        name="moe_post",
    )(h2, y0, y1, route, p_all, lw["ple_norm"], lw["ple_w_gate"], lw["ple_w_proj"], norm_final)


def _route_plan(route, n_rows):
    t = route.shape[0]
    e_flat = route[:, R_I0:R_I1 + 1].astype(jnp.int32).reshape(-1)
    onehot = (jnp.arange(N_EXPERTS, dtype=jnp.int32)[:, None] == e_flat[None, :]).astype(jnp.int32)
    csum = jnp.cumsum(onehot, axis=1)
    counts = csum[:, -1]
    pcounts = ((counts + TMM - 1) // TMM) * TMM
    ends = jnp.cumsum(pcounts)
    starts = ends - pcounts
    pos = jnp.sum(onehot * (csum - 1 + starts[:, None]), axis=0)
    src = (jnp.arange(n_rows, dtype=jnp.int32) % t).at[pos].set(
        jnp.arange(2 * t, dtype=jnp.int32) // 2, unique_indices=True)
    blk_start = jnp.arange(n_rows // TMM, dtype=jnp.int32) * TMM
    blk_e = jnp.minimum(jnp.sum((blk_start[:, None] >= ends[None, :]).astype(jnp.int32), axis=1),
                        N_EXPERTS - 1).astype(jnp.int32)
    nblk = (ends[-1:] // TMM).astype(jnp.int32)
    return src, pos.reshape(t, 2), blk_e, nblk


def _rope_tables(pos):
    half = HEAD_DIM // 2
    inv_freq = ROPE_BASE ** (-jnp.arange(half, dtype=F32) / half)
    ang = pos[:, None] * inv_freq[None, :]
    cos, sin = jnp.cos(ang), jnp.sin(ang)
    cos_h = jnp.concatenate([cos, cos], axis=-1)
    sin_h = jnp.concatenate([-sin, sin], axis=-1)
    return jnp.tile(cos_h, (1, N_HEADS)), jnp.tile(sin_h, (1, N_HEADS))


def _pad_lanes(x, width=LANES):
    return jnp.pad(x, [(0, 0)] * (x.ndim - 1) + [(0, width - x.shape[-1])])


def kernel(x_prompt, x_sample, state_ret, state_mlstm_c, state_mlstm_n, state_mlstm_m, state_gdn, state_gdn_conv, state_ssm, state_ssm_conv, p_prompt, p_sample, norm_mix, w_in, w_out, ret_norm, mlstm_i_bias, mlstm_f_bias, mlstm_norm, gdn_conv_w, gdn_a_log, gdn_dt_bias, gdn_norm, ssm_conv_w, ssm_conv_b, ssm_dt_bias, ssm_a_log, ssm_d, ssm_norm, norm_ffn, ffn_w_gate, ffn_w_up, ffn_w_down, moe_router, moe_w_gate, moe_w_up, moe_w_down, ple_w_proj, ple_norm, ple_w_gate, norm_final):
    bp, tp = x_prompt.shape[:2]
    bs, ts = x_sample.shape[:2]
    n_prompt = bp * tp
    n_all = n_prompt + bs * ts
    sample_seqs = MIX_ROWS // ts
    assert tp % MIX_ROWS == 0 and bp % PROMPT_SEQS == 0 and n_all % TM == 0
    assert MIX_ROWS % ts == 0 and bs % sample_seqs == 0 and ts >= CONV_W - 1 and DEPTH % 2 == 0
    hd = HEAD_DIM

    h = jnp.concatenate([x_prompt.reshape(n_prompt, D_MODEL), x_sample.reshape(bs * ts, D_MODEL)])
    h = h.astype(F32)
    p_all = jnp.concatenate([p_prompt.reshape(DEPTH, n_prompt, PLE_DIM),
                             p_sample.reshape(DEPTH, bs * ts, PLE_DIM)], axis=1)

    o_ml = 4 * GROUP_W
    o_gdn = o_ml + 4 * GROUP_W + 2 * N_HEADS
    o_ssm = o_gdn + 4 * GROUP_W + 2 * N_HEADS
    gate_cols = jnp.concatenate([
        w_in[:, :, o_ml + 4 * GROUP_W:o_gdn],
        w_in[:, :, o_gdn + 4 * GROUP_W:o_ssm],
        w_in[:, :, o_ssm + 4 * GROUP_W:],
    ], axis=-1)
    w_in_r = jnp.concatenate([
        w_in[:, :, 0:o_ml + 4 * GROUP_W],
        w_in[:, :, o_gdn:o_gdn + 4 * GROUP_W],
        w_in[:, :, o_ssm:o_ssm + 4 * GROUP_W],
        jnp.repeat(gate_cols, hd, axis=-1),
    ], axis=-1)
    assert w_in_r.shape[-1] == D_PROJ
    col = jnp.arange(D_PROJ)
    k_cols = ((col >= C_RET + GROUP_W) & (col < C_RET + 2 * GROUP_W)) | (
        (col >= C_ML + GROUP_W) & (col < C_ML + 2 * GROUP_W))
    w_in_r = (w_in_r * jnp.where(k_cols, hd ** -0.5, 1.0).astype(F32)).astype(BF16)

    zeros4 = jnp.zeros((DEPTH, N_HEADS), F32)
    norms = jnp.stack([ret_norm, mlstm_norm, gdn_norm, ssm_norm], axis=1).astype(F32)
    gbx = jnp.repeat(jnp.stack([mlstm_i_bias, mlstm_f_bias, gdn_dt_bias, ssm_dt_bias, gdn_a_log,
                                ssm_a_log, ssm_d, zeros4], axis=1).astype(F32), hd, axis=-1)

    consts_p = _mixer_consts(1, MIX_ROWS)
    consts_s = _mixer_consts(sample_seqs, ts)
    ret_tabs_p = _ret_tables(1, MIX_ROWS)
    ret_tabs_s = _ret_tables(sample_seqs, ts)
    tab_prompt = _rope_tables(jnp.arange(tp, dtype=F32))
    tab_sample = tuple(jnp.tile(t, (sample_seqs, 1))
                       for t in _rope_tables(PAST_LEN + jnp.arange(ts, dtype=F32)))

    keep = SUBLANES - (CONV_W - 1)
    eye_h = jnp.eye(N_HEADS, dtype=F32)

    def conv_in(s):
        return jnp.pad(s.astype(F32), ((0, 0), (keep, 0), (0, 0)))

    big_in = dict(ret=state_ret.astype(F32), mc=state_mlstm_c.astype(F32),
                  gdn=state_gdn.astype(F32), ssm=state_ssm.astype(F32))
    big_states = {k: jnp.zeros(v.shape, F32) for k, v in big_in.items()}

    def sample_states_in(i):
        n_bd = (state_mlstm_n[i].astype(F32)[:, :, :, None, None]
                * eye_h[None, :, None, :, None])
        n_bd = jnp.broadcast_to(n_bd, (bs, N_HEADS, hd, N_HEADS, hd)).reshape(bs, GROUP_W, GROUP_W)
        m_x = jnp.repeat(state_mlstm_m[i].astype(F32), hd, axis=-1)[:, None, :]
        return (big_in["ret"], big_in["mc"], n_bd,
                jnp.broadcast_to(m_x, (bs, SUBLANES, GROUP_W)), big_in["gdn"],
                conv_in(state_gdn_conv[i]), big_in["ssm"], conv_in(state_ssm_conv[i]),
                big_states["ret"], big_states["mc"], big_states["gdn"], big_states["ssm"])

    def n_vec(s_mn):
        return jnp.stack([s_mn[:, hd * hh:hd * (hh + 1), hd * hh] for hh in range(N_HEADS)], axis=1)

    def sample_states_out(st):
        s_ret, s_mc, s_mn, s_mm, s_gdn, s_gconv, s_ssm, s_sconv = st
        big_states.update(ret=s_ret, mc=s_mc, gdn=s_gdn, ssm=s_ssm)
        return (n_vec(s_mn), s_mm[:, 0, ::hd], s_gconv[:, keep:], s_sconv[:, keep:])

    def prompt_states_out(st):
        s_ret, s_mc, s_mn, s_mm, s_gdn, s_gconv, s_ssm, s_sconv = st

        def diag(s):
            return jnp.stack([s[:, hd * hh:hd * (hh + 1), hd * hh:hd * (hh + 1)]
                              for hh in range(N_HEADS)], axis=1)

        per_group = N_HEADS // SSM_GROUPS
        ssm = jnp.stack([s_ssm[:, SSM_STATE * (hh // per_group):SSM_STATE * (hh // per_group + 1),
                               hd * hh:hd * (hh + 1)] for hh in range(N_HEADS)], axis=1)
        return (diag(s_ret), diag(s_mc), n_vec(s_mn), s_mm[:, 0, ::hd], diag(s_gdn),
                s_gconv[:, keep:], ssm, s_sconv[:, keep:])

    n_moe_rows = 2 * n_all + N_EXPERTS * TMM
    moe_wg, moe_wu, moe_wd = (w.astype(BF16) for w in (moe_w_gate, moe_w_up, moe_w_down))
    new_prompt, new_sample = [], []
    for i in range(DEPTH):
        lp = dict(norms=norms[i], gcw=gdn_conv_w[i].astype(F32), scw=ssm_conv_w[i].astype(F32),
                  scb=ssm_conv_b[i].astype(F32)[None, :], gbx=gbx[i])
        proj = _proj_call(h, norm_mix[i].astype(F32)[None, :], w_in_r[i])
        y_p, st_p = _mixer_call(proj, 0, bp, tp, PROMPT_SEQS, 1, True, tab_prompt, lp, consts_p,
                                ret_tabs_p)
        y_s, st_s = _mixer_call(proj, n_prompt, bs, ts, 1, sample_seqs, False, tab_sample, lp,
                                consts_s, ret_tabs_s, sample_states_in(i), layer=i)
        y_all = jnp.concatenate([y_p.reshape(n_prompt, D_MODEL), y_s.reshape(bs * ts, D_MODEL)])
        new_prompt.append(prompt_states_out(st_p))
        new_sample.append(sample_states_out(st_s))

        j = i // 2
        lw = dict(w_out=w_out[i].astype(BF16), norm_ffn=norm_ffn[i].astype(F32)[None, :],
                  ple_norm=ple_norm[i].astype(F32)[None, :], ple_w_gate=ple_w_gate[i].astype(BF16),
                  ple_w_proj=ple_w_proj[i].astype(BF16))
        if i % 2 == 0:
            lw.update(wg=ffn_w_gate[j].astype(BF16), wu=ffn_w_up[j].astype(BF16),
                      wd=ffn_w_down[j].astype(BF16))
            h = _dense_call(h, y_all, p_all, i, lw)
        else:
            lw.update(router=_pad_lanes(moe_router[j].astype(F32)), wg=moe_wg, wu=moe_wu, wd=moe_wd)
            h2, c, route = _moe_pre_call(h, y_all, lw)
            src, pos, blk_e, nblk = _route_plan(route, n_moe_rows)
            src = src + n_all * (jnp.arange(n_moe_rows, dtype=jnp.int32) % C_COPIES)
            xs = jnp.take(c.reshape(C_COPIES * n_all, D_MODEL), src, axis=0, mode="clip")
            ys = _moe_ffn_call(xs, blk_e, nblk, lw, j)
            y0 = jnp.take(ys, pos[:, 0], axis=0, mode="clip")
            y1 = jnp.take(ys, pos[:, 1], axis=0, mode="clip")
            h = _moe_post_call(h2, y0, y1, route, p_all, i, lw, norm_final.astype(F32)[None, :],
                               final=(i == DEPTH - 1))

    y = h.astype(x_prompt.dtype)
    y_prompt = y[:n_prompt].reshape(bp, tp, D_MODEL)
    y_sample = y[n_prompt:].reshape(bs, ts, D_MODEL)
    stack = lambda lst: tuple(jnp.stack([l[k] for l in lst]) for k in range(len(lst[0])))
    sa_n, sa_m, sa_gconv, sa_sconv = stack(new_sample)
    return (y_prompt, y_sample) + stack(new_prompt) + (
        big_states["ret"], big_states["mc"], sa_n, sa_m, big_states["gdn"], sa_gconv,
        big_states["ssm"], sa_sconv)
```

```python
import functools

import jax
import jax.numpy as jnp
from jax import lax
from jax.experimental import pallas as pl
from jax.experimental.pallas import tpu as pltpu

F32 = jnp.float32
BF16 = jnp.bfloat16
HIGHEST = lax.Precision.HIGHEST

D_MODEL = 1024
DEPTH = 4
PAST_LEN = 16384
N_HEADS = 4
HEAD_DIM = 64
GROUP_W = N_HEADS * HEAD_DIM
SSM_STATE = 128
SSM_GROUPS = 2
CONV_W = 4
CONV_CH = 3 * GROUP_W
ROPE_BASE = 10000.0
FFN_DENSE = 2816
N_EXPERTS = 8
FFN_EXPERT = 3584
PLE_DIM = 256
EPS = 1e-6

LANES = 128
SUBLANES = 8
VMEM_LIMIT = 56 * 1024 * 1024

C_RET = 0
C_ML = 1024
C_GDN = 2048
C_SSM = 3072
C_GATE = 4096
D_PROJ = C_GATE + LANES
X_MI, X_MF, X_GB, X_GA, X_DT = range(5)
N_GATES = 5

MIX_ROWS = 64
PROMPT_SEQS = 8
TM = 512
TMM = 512
FFN_SPLIT = 2


def _dot_hi(a, b):
    return jnp.dot(a, b, precision=HIGHEST, preferred_element_type=F32)


def _sigmoid(x):
    return 1.0 / (1.0 + jnp.exp(-x))


def _silu(x):
    return x * _sigmoid(x)


def _softplus(x):
    return jnp.maximum(x, 0.0) + jnp.log(1.0 + jnp.exp(-jnp.abs(x)))


def _rms(x, g):
    return x * lax.rsqrt(jnp.mean(x * x, -1, keepdims=True) + EPS) * g


def _proj_kernel(h_ref, g_ref, w_ref, o_ref):
    a = _rms(h_ref[...], g_ref[...])
    o_ref[...] = jnp.dot(a.astype(BF16), w_ref[...], preferred_element_type=F32)


def _proj_call(h, g, w):
    t = h.shape[0]
    return pl.pallas_call(
        _proj_kernel,
        grid=(t // TM,),
        in_specs=[
            pl.BlockSpec((TM, D_MODEL), lambda i: (i, 0)),
            pl.BlockSpec((1, D_MODEL), lambda i: (0, 0)),
            pl.BlockSpec((D_MODEL, D_PROJ), lambda i: (0, 0), pipeline_mode=pl.Buffered(1)),
        ],
        out_specs=pl.BlockSpec((TM, D_PROJ), lambda i: (i, 0)),
        out_shape=jax.ShapeDtypeStruct((t, D_PROJ), F32),
        compiler_params=pltpu.CompilerParams(
            dimension_semantics=("arbitrary",), vmem_limit_bytes=VMEM_LIMIT),
        name="proj",
    )(h, g, w)


def _split3(x):
    hi = x.astype(BF16)
    r1 = x - hi.astype(F32)
    mid = r1.astype(BF16)
    lo = (r1 - mid.astype(F32)).astype(BF16)
    return hi, mid, lo


def _dot_tn(a, b):
    return lax.dot_general(a, b, (((0,), (0,)), ((), ())), preferred_element_type=F32)


def _mixer_kernel(nb, G, T, carry, *refs):
    R = G * T
    hd = HEAD_DIM
    n_levels = T.bit_length() - 1
    per_group = N_HEADS // SSM_GROUPS
    proj_refs = refs[:nb]
    (cos_ref, sin_ref, gb_ref, norms_ref, gcw_ref, scw_ref, scb_ref, mbd_b_ref, mbd_f_ref,
     m2_b_ref, gexp_ref, pm_ref, lm_ref, tril3_ref, rt_ref, rs_ref) = refs[nb:nb + 16]
    rest = refs[nb + 16:]
    if carry:
        (y_ref, o_ret, o_mc, o_mn, o_mm, o_gdn, o_gconv, o_ssm, o_sconv, xg_ref, xs_ref) = rest
    else:
        (i_ret, i_mc, i_mn, i_mm, i_gdn, i_gconv, i_ssm, i_sconv) = rest[:8]
        (y_ref, o_ret, o_mc, o_mn, o_mm, o_gdn, o_gconv, o_ssm, o_sconv, xg_ref, xs_ref) = rest[12:]

    if carry:
        @pl.when(pl.program_id(1) == 0)
        def _init():
            for r in (o_ret, o_mc, o_mn, o_mm, o_gdn, o_ssm, xg_ref, xs_ref):
                r[...] = jnp.zeros_like(r)

    def to_bd(s4):
        rows = []
        for h in range(N_HEADS):
            parts = [jnp.zeros((hd, hd * h), F32)] if h else []
            parts.append(s4[h])
            if h < N_HEADS - 1:
                parts.append(jnp.zeros((hd, hd * (N_HEADS - 1 - h)), F32))
            rows.append(jnp.concatenate(parts, axis=1))
        return jnp.concatenate(rows, axis=0)

    def to_st(s4):
        rows = []
        for g in range(SSM_GROUPS):
            parts = [jnp.zeros((SSM_STATE, SSM_STATE * g), F32)] if g else []
            parts += [s4[per_group * g + k] for k in range(per_group)]
            if g < SSM_GROUPS - 1:
                parts.append(jnp.zeros((SSM_STATE, SSM_STATE * (SSM_GROUPS - 1 - g)), F32))
            rows.append(jnp.concatenate(parts, axis=1))
        return jnp.concatenate(rows, axis=0)

    out_refs = dict(ret=o_ret, mc=o_mc, mn=o_mn, gdn=o_gdn, ssm=o_ssm)

    def live_blocks(kind):
        if kind == "ssm":
            return [(SSM_STATE * g, SSM_STATE, SSM_STATE * g, SSM_STATE) for g in range(SSM_GROUPS)]
        return [(hd * h, hd, hd * h, hd) for h in range(N_HEADS)]

    if carry:
        def get(kind, j, i):
            return out_refs[kind][j]

        def update(kind, j, i, dec, dlt):
            ref = out_refs[kind]
            for r0, nr, c0, nc in live_blocks(kind):
                ref[j, r0:r0 + nr, c0:c0 + nc] = (ref[j, r0:r0 + nr, c0:c0 + nc] * dec[:, c0:c0 + nc]
                                                  + dlt[r0:r0 + nr, c0:c0 + nc])

        def get_m(j, i):
            return o_mm[j, 0:1, :]

        def put_m(j, i, row):
            o_mm[j] = jnp.broadcast_to(row, (SUBLANES, GROUP_W))
    else:
        in_refs = dict(ret=i_ret, mc=i_mc, gdn=i_gdn, ssm=i_ssm)

        def get(kind, j, i):
            if kind == "mn":
                return i_mn[i]
            if kind == "ssm":
                return to_st(i_ssm[i])
            return to_bd(in_refs[kind][i])

        def update(kind, j, i, dec, dlt):
            if kind == "mn":
                o_mn[i] = i_mn[i] * dec + mbd_f * dlt
                return
            for h in range(N_HEADS):
                r0, nr = (SSM_STATE * (h // per_group), SSM_STATE) if kind == "ssm" else (hd * h, hd)
                out_refs[kind][i, h] = (in_refs[kind][i, h] * dec[:, hd * h:hd * (h + 1)]
                                        + dlt[r0:r0 + nr, hd * h:hd * (h + 1)])

        def get_m(j, i):
            return i_mm[i, 0:1, :]

        def put_m(j, i, row):
            o_mm[i] = jnp.broadcast_to(row, (SUBLANES, GROUP_W))

    def seq_rows(x, i):
        return x if G == 1 else x[T * i:T * (i + 1)]

    def seq_row(x, i):
        return x if x.shape[0] == 1 else x[T * i:T * i + 1]

    def seg_last(x):
        if G == 1:
            return x[R - 1:R]
        last = x.reshape(G, T, x.shape[1])[:, T - 1:T, :]
        return jnp.broadcast_to(last, (G, T, x.shape[1])).reshape(R, x.shape[1])

    def seg_max0(x):
        if G == 1:
            return jnp.max(x, axis=0, keepdims=True)
        mx = jnp.max(x.reshape(G, T, x.shape[1]), axis=1, keepdims=True)
        return jnp.broadcast_to(mx, (G, T, x.shape[1])).reshape(R, x.shape[1])

    def m_rows(j):
        if G == 1:
            return get_m(j, 0)
        return jnp.concatenate([jnp.broadcast_to(get_m(j, i), (T, GROUP_W)) for i in range(G)],
                               axis=0)

    mbd_b = mbd_b_ref[...]
    mbd_f = mbd_f_ref[...]
    m2_b = m2_b_ref[...]
    eye_p = pm_ref[0]
    ut_p = pm_ref[1]
    ninf_p = pm_ref[2]
    tril3 = tril3_ref[...]
    gb = gb_ref[...]
    norms = norms_ref[...]
    cosv = cos_ref[...]
    sinv = sin_ref[...]
    lane_w = lax.broadcasted_iota(jnp.int32, (R, GROUP_W), 1)
    first_half = (lane_w % hd) < (hd // 2)
    ones_b = jnp.ones((R, GROUP_W), BF16)
    neg_a_gdn = -jnp.exp(gb[4:5, :])
    neg_a_ssm = -jnp.exp(gb[5:6, :])
    inv_hd = 1.0 / hd

    def bf(x):
        return x.astype(BF16)

    def bd(xb, mask):
        return jnp.concatenate([xb] * (GROUP_W // R), axis=0) * mask

    def mm(a, b):
        return jnp.dot(a, b, preferred_element_type=F32)

    def mm_nt(a, b):
        return lax.dot_general(a, b, (((1,), (1,)), ((), ())), preferred_element_type=F32)

    def mm_state(a, states):
        outs = [mm(seq_rows(a, i), bf(states[i])) for i in range(G)]
        return outs[0] if G == 1 else jnp.concatenate(outs, axis=0)

    def delta(a, b, i):
        return _dot_tn(seq_rows(a, i), seq_rows(b, i))

    def cumsum(g):
        return mm(tril3, jnp.concatenate(_split3(g), axis=0))

    gate_cache = {}

    def gate(j, kind):
        if j not in gate_cache:
            slab = proj_refs[j][:, C_GATE:C_GATE + LANES]
            gate_cache[j] = mm(jnp.concatenate(_split3(slab), axis=1), gexp_ref[...])
        return gate_cache[j][:, GROUP_W * kind:GROUP_W * (kind + 1)]

    def sum0(x):
        return jnp.sum(x, axis=0, keepdims=True)

    def segmax(x):
        parts = []
        for h in range(N_HEADS):
            mh = jnp.max(x[:, hd * h:hd * (h + 1)], axis=1, keepdims=True)
            parts.append(jnp.broadcast_to(mh, (R, hd)))
        return jnp.concatenate(parts, axis=1)

    def rope(x):
        swapped = jnp.where(first_half, pltpu.roll(x, GROUP_W - hd // 2, 1),
                            pltpu.roll(x, hd // 2, 1))
        return x * cosv + swapped * sinv

    def conv4(x_ref, j, x, w, hist_in, hist_out):
        if not carry:
            x_ref[j, :, 0:SUBLANES, :] = hist_in[...]
        x_ref[j, :, SUBLANES:SUBLANES + T, :] = x.reshape(G, T, CONV_CH)
        acc = w[0:1, :] * x_ref[j, :, pl.ds(SUBLANES - 3, T), :]
        for k in range(1, CONV_W):
            acc = acc + w[k:k + 1, :] * x_ref[j, :, pl.ds(SUBLANES - 3 + k, T), :]
        hist = x_ref[j, :, T:T + SUBLANES, :]
        if carry:
            hist_out[j] = hist[0]
            x_ref[j, :, 0:SUBLANES, :] = hist
        else:
            hist_out[...] = hist
        return acc.reshape(R, CONV_CH)


    def ret_chain(j):
        P = proj_refs[j]
        rq = rope(P[:, C_RET:C_RET + GROUP_W])
        rk = rope(P[:, C_RET + GROUP_W:C_RET + 2 * GROUP_W])
        rvb = bf(P[:, C_RET + 2 * GROUP_W:C_RET + 3 * GROUP_W])
        S = [get("ret", j, i) for i in range(G)]
        scores = mm_nt(bf(rq), bd(bf(rk), mbd_b)) * rt_ref[0]
        o_inter = mm_state(bf(rq * rt_ref[1]), S)
        yield
        ke = bf(rk * rt_ref[2])
        for i in range(G):
            update("ret", j, i, rs_ref[0:1, :], delta(ke, rvb, i))
        o = mm(bf(scores), bd(rvb, mbd_b)) + o_inter
        yield
        mean = mm(bf(o), mbd_b) * inv_hd
        yield
        xc = o - mean
        var = mm(bf(xc * xc), mbd_b) * inv_hd
        yield
        ro = xc * lax.rsqrt(var + EPS) * norms[0:1, :]
        rg = P[:, C_RET + 3 * GROUP_W:C_RET + 4 * GROUP_W]
        y_ref[j, :, 0:GROUP_W] = (_silu(rg) * ro).astype(y_ref.dtype)

    def mlstm_chain(j):
        P = proj_refs[j]
        mq = bf(P[:, C_ML:C_ML + GROUP_W])
        mk = P[:, C_ML + GROUP_W:C_ML + 2 * GROUP_W]
        mvb = bf(P[:, C_ML + 2 * GROUP_W:C_ML + 3 * GROUP_W])
        i_x = gate(j, X_MI) + gb[0:1, :]
        g_f = -_softplus(-(gate(j, X_MF) + gb[1:2, :]))
        b_col = cumsum(g_f)
        qk = mm_nt(mq, bd(bf(mk), mbd_b))
        C = [get("mc", j, i) for i in range(G)]
        N = [get("mn", j, i) for i in range(G)]
        q_c = mm_state(mq, C)
        q_n = mm_state(mq, N)
        yield
        b_row = sum0(g_f * ut_p)
        i_row = sum0(i_x * eye_p)
        b_last = seg_last(b_col)
        m_prev = m_rows(j)
        dd = b_col - b_row + i_row
        inter = b_col + m_prev
        m_row = jnp.maximum(segmax(dd + ninf_p), inter)
        w_intra = jnp.exp(jnp.minimum(dd - m_row, ninf_p))
        w_state = jnp.exp(inter - m_row)
        sb = bf(qk * w_intra)
        num = mm(sb, bd(mvb, mbd_b)) + w_state * q_c
        den = mm(sb, mbd_b) + w_state * q_n
        g_end = b_last - b_col + i_x
        m_new = jnp.maximum(b_last + m_prev, seg_max0(g_end))
        w_c = jnp.exp(b_last + m_prev - m_new)
        kwb = bf(mk * jnp.exp(g_end - m_new))
        for i in range(G):
            wc_i = seq_row(w_c, i)
            update("mc", j, i, wc_i, delta(kwb, mvb, i))
            update("mn", j, i, wc_i, delta(kwb, ones_b, i))
            put_m(j, i, seq_row(m_new, i))
        yield
        hh = num / jnp.maximum(jnp.abs(den), jnp.exp(-m_row))
        mean = mm(bf(hh), mbd_b) * inv_hd
        yield
        xc = hh - mean
        var = mm(bf(xc * xc), mbd_b) * inv_hd
        yield
        mh = xc * lax.rsqrt(var + EPS) * norms[1:2, :]
        mo = P[:, C_ML + 3 * GROUP_W:C_ML + 4 * GROUP_W]
        y_ref[j, :, GROUP_W:2 * GROUP_W] = (_sigmoid(mo) * mh).astype(y_ref.dtype)

    def gdn_chain(j):
        P = proj_refs[j]
        conv = conv4(xg_ref, j, P[:, C_GDN:C_GDN + CONV_CH], gcw_ref[...],
                     None if carry else i_gconv, o_gconv)
        act = _silu(conv)
        gq, gk, gv = act[:, 0:GROUP_W], act[:, GROUP_W:2 * GROUP_W], act[:, 2 * GROUP_W:]
        q_ss = mm(bf(gq * gq), mbd_b)
        k_ss = mm(bf(gk * gk), mbd_b)
        g_g = neg_a_gdn * _softplus(gate(j, X_GA) + gb[2:3, :])
        b_col = cumsum(g_g)
        yield
        qn = gq * lax.rsqrt(q_ss + EPS) * hd ** -0.5
        kn = gk * lax.rsqrt(k_ss + EPS)
        beta = _sigmoid(gate(j, X_GB))
        b_row = sum0(g_g * ut_p)
        b_last = seg_last(b_col)
        e_col = jnp.exp(b_col)
        decay = jnp.exp(jnp.minimum(b_col - b_row, ninf_p))
        kb = kn * beta
        bd_k = bd(bf(kn), mbd_b)
        kk = mm_nt(bf(kb), bd_k)
        qk = mm_nt(bf(qn), bd_k)
        S = [get("gdn", j, i) for i in range(G)]
        q_s = mm_state(bf(qn), S)
        yield
        a_low = kk * decay
        attn = qk * decay
        t_inv = eye_p - a_low * lm_ref[0]
        for lev in range(1, n_levels):
            m1 = mm(bf(a_low * lm_ref[lev]), bd(bf(t_inv), mbd_b))
            yield
            t_inv = t_inv - mm(bf(t_inv), bd(bf(m1), mbd_b))
            yield
        tb = bf(t_inv)
        u = mm(tb, bd(bf(gv * beta), mbd_b))
        w = mm(tb, bd(bf(kb * e_col), mbd_b))
        yield
        v_new = u - mm_state(bf(w), S)
        yield
        vnb = bf(v_new)
        o = e_col * q_s + mm(bf(attn), bd(vnb, mbd_b))
        keb = bf(kn * jnp.exp(b_last - b_col))
        s_dec = jnp.exp(b_last)
        for i in range(G):
            update("gdn", j, i, seq_row(s_dec, i), delta(keb, vnb, i))
        yield
        o_ss = mm(bf(o * o), mbd_b)
        yield
        go = o * lax.rsqrt(o_ss * inv_hd + EPS) * norms[2:3, :]
        gz = P[:, C_GDN + CONV_CH:C_GDN + CONV_CH + GROUP_W]
        y_ref[j, :, 2 * GROUP_W:3 * GROUP_W] = (go * _silu(gz)).astype(y_ref.dtype)

    def ssd_chain(j):
        P = proj_refs[j]
        conv = conv4(xs_ref, j, P[:, C_SSM + GROUP_W:C_SSM + GROUP_W + CONV_CH], scw_ref[...],
                     None if carry else i_sconv, o_sconv) + scb_ref[...]
        act = _silu(conv)
        sx = act[:, 0:GROUP_W]
        s_b = bf(act[:, GROUP_W:2 * GROUP_W])
        s_c = bf(act[:, 2 * GROUP_W:])
        dt = _softplus(gate(j, X_DT) + gb[3:4, :])
        g_s = dt * neg_a_ssm
        b_col = cumsum(g_s)
        cb = mm_nt(s_c, bd(s_b, m2_b))
        S = [get("ssm", j, i) for i in range(G)]
        c_s = mm_state(s_c, S)
        yield
        b_row = sum0(g_s * ut_p)
        b_last = seg_last(b_col)
        decay = jnp.exp(jnp.minimum(b_col - b_row, ninf_p))
        v = sx * dt
        o = mm(bf(cb * decay), bd(bf(v), mbd_b)) + jnp.exp(b_col) * c_s
        v_end = bf(v * jnp.exp(b_last - b_col))
        s_dec = jnp.exp(b_last)
        for i in range(G):
            update("ssm", j, i, seq_row(s_dec, i), delta(s_b, v_end, i))
        yield
        sz = P[:, C_SSM:C_SSM + GROUP_W]
        yv = (o + gb[6:7, :] * sx) * _silu(sz)
        ms = mm(bf(yv * yv), m2_b) * (1.0 / (GROUP_W // SSM_GROUPS))
        yield
        y_ref[j, :, 3 * GROUP_W:] = (yv * lax.rsqrt(ms + EPS) * norms[3:4, :]).astype(y_ref.dtype)

    chains = [chain(j) for chain in (gdn_chain, mlstm_chain, ret_chain, ssd_chain)
              for j in range(nb)]
    while chains:
        alive = []
        for ch in chains:
            try:
                next(ch)
                alive.append(ch)
            except StopIteration:
                pass
        chains = alive


def _mixer_consts(G, T):
    R = G * T
    n_levels = T.bit_length() - 1
    r = jnp.arange(GROUP_W)
    mbd = (r[:, None] // HEAD_DIM == r[None, :] // HEAD_DIM)
    half = GROUP_W // SSM_GROUPS
    m2 = (r[:, None] // half == r[None, :] // half)
    row = jnp.arange(R)[:, None]
    col = (r % R)[None, :]
    same = (row // T) == (col // T)
    tl, ts = row % T, col % T
    causal = same & (tl >= ts)
    pm = jnp.stack([(row == col).astype(F32), (same & (tl <= ts)).astype(F32),
                    jnp.where(causal, 0.0, -jnp.inf).astype(F32)])
    lm = jnp.stack([same & ((tl >> (j + 1)) == (ts >> (j + 1))) & (((tl >> j) & 1) == 1)
                    & (((ts >> j) & 1) == 0) for j in range(n_levels)]).astype(F32)
    tril = causal[:, :R].astype(BF16)
    src_lane = jnp.arange(LANES)[:, None]
    dst = jnp.arange(N_GATES * GROUP_W)[None, :]
    gexp = (src_lane == N_HEADS * (dst // GROUP_W) + (dst % GROUP_W) // HEAD_DIM).astype(BF16)
    return dict(mbd_b=mbd.astype(BF16), mbd_f=mbd.astype(F32), m2_b=m2.astype(BF16),
                gexp=jnp.concatenate([gexp] * 3, axis=0), pm=pm, lm=lm,
                tril3=jnp.concatenate([tril] * 3, axis=1))


def _ret_tables(G, T):
    R = G * T
    log_gamma = jnp.log1p(-jnp.exp2(-5.0 - jnp.arange(N_HEADS, dtype=F32)))
    b = jnp.cumsum(jnp.broadcast_to(log_gamma[:, None], (N_HEADS, T)), axis=1)
    b_r = jnp.tile(b, (1, G))
    pos = jnp.arange(R)
    causal = ((pos[:, None] // T) == (pos[None, :] // T)) & ((pos[:, None] % T) >= (pos[None, :] % T))
    diff = b_r[:, :, None] - b_r[:, None, :]
    dec = jnp.exp(jnp.where(causal[None], diff, -jnp.inf))
    dec_p = dec.transpose(1, 0, 2).reshape(R, GROUP_W)
    expand = lambda x: jnp.repeat(x.T, HEAD_DIM, axis=1)
    q_scale = expand(jnp.exp(b_r))
    k_scale = expand(jnp.exp(b[:, -1:] - b_r))
    s_dec = jnp.repeat(jnp.exp(b[:, -1]), HEAD_DIM)[None, :]
    return (jnp.stack([dec_p, q_scale, k_scale]),
            jnp.broadcast_to(s_dec, (SUBLANES, GROUP_W)))


def _mixer_call(proj, row0, n_seq, seq_len, nb, G, carry, tables, lp, consts, ret_tabs,
                states=(), layer=0):
    R = MIX_ROWS
    T = R // G
    n_chunks = seq_len // T
    n_levels = T.bit_length() - 1
    blk0 = row0 // R
    cos_t, sin_t = tables
    const2 = lambda g, c: (0, 0)
    const3 = lambda g, c: (0, 0, 0)
    st3 = lambda g, c: (g, 0, 0)
    st4 = lambda g, c: (g, 0, 0, 0)
    if carry:
        proj_specs = [pl.BlockSpec((R, D_PROJ), functools.partial(
            lambda g, c, j: (blk0 + (g * nb + j) * n_chunks + c, 0), j=j)) for j in range(nb)]
        tab_map = lambda g, c: (c, 0)
        n_steps = n_seq // nb
    else:
        assert nb == 1 and n_chunks == 1
        proj_specs = [pl.BlockSpec((R, D_PROJ), lambda g, c: (blk0 + g, 0))]
        tab_map = const2
        n_steps = n_seq // G
    sq = pl.BlockSpec((GROUP_W, GROUP_W), const2)
    in_specs = proj_specs + [
        pl.BlockSpec((R, GROUP_W), tab_map),
        pl.BlockSpec((R, GROUP_W), tab_map),
        pl.BlockSpec((SUBLANES, GROUP_W), const2),
        pl.BlockSpec((N_HEADS, GROUP_W), const2),
        pl.BlockSpec((CONV_W, CONV_CH), const2),
        pl.BlockSpec((CONV_W, CONV_CH), const2),
        pl.BlockSpec((1, CONV_CH), const2),
        sq, sq, sq,
        pl.BlockSpec((3 * LANES, N_GATES * GROUP_W), const2),
        pl.BlockSpec((3, R, GROUP_W), const3),
        pl.BlockSpec((n_levels, R, GROUP_W), const3),
        pl.BlockSpec((R, 3 * R), const2),
        pl.BlockSpec((3, R, GROUP_W), const3),
        pl.BlockSpec((SUBLANES, GROUP_W), const2),
    ]
    ns = nb if carry else G
    big = pl.BlockSpec((ns, GROUP_W, GROUP_W), st3)
    hist = pl.BlockSpec((ns, SUBLANES, CONV_CH), st3)
    mrow = pl.BlockSpec((ns, SUBLANES, GROUP_W), st3)
    sq_shape = jax.ShapeDtypeStruct((n_seq, GROUP_W, GROUP_W), F32)
    hist_shape = jax.ShapeDtypeStruct((n_seq, SUBLANES, CONV_CH), F32)
    m_shape = jax.ShapeDtypeStruct((n_seq, SUBLANES, GROUP_W), F32)
    if carry:
        state_specs = [big, big, big, mrow, big, hist, big, hist]
        state_shapes = [sq_shape, sq_shape, sq_shape, m_shape, sq_shape, hist_shape, sq_shape,
                        hist_shape]
        y_spec = pl.BlockSpec((nb, R, D_MODEL), lambda g, c: (g, c, 0))
        y_shape = jax.ShapeDtypeStruct((n_seq, seq_len, D_MODEL), BF16)
        state_in_specs = []
        aliases = {}
    else:
        st5 = lambda g, c: (layer, g, 0, 0, 0)
        nat = pl.BlockSpec((None, G, N_HEADS, HEAD_DIM, HEAD_DIM), st5)
        nat_ssm = pl.BlockSpec((None, G, N_HEADS, SSM_STATE, HEAD_DIM), st5)
        nat_shape = jax.ShapeDtypeStruct((DEPTH, n_seq, N_HEADS, HEAD_DIM, HEAD_DIM), F32)
        ssm_shape = jax.ShapeDtypeStruct((DEPTH, n_seq, N_HEADS, SSM_STATE, HEAD_DIM), F32)
        state_specs = [nat, nat, big, mrow, nat, hist, nat_ssm, hist]
        state_shapes = [nat_shape, nat_shape, sq_shape, m_shape, nat_shape, hist_shape, ssm_shape,
                        hist_shape]
        y_spec = pl.BlockSpec((1, R, D_MODEL), lambda g, c: (g, 0, 0))
        y_shape = jax.ShapeDtypeStruct((n_seq // G, R, D_MODEL), BF16)
        state_in_specs = state_specs + [pl.BlockSpec(memory_space=pl.ANY)] * 4
        n_fixed = len(in_specs) + len(state_specs)
        aliases = {n_fixed + q: 1 + k for q, k in enumerate((0, 1, 4, 6))}
    res = pl.pallas_call(
        functools.partial(_mixer_kernel, nb, G, T, carry),
        grid=(n_steps, n_chunks),
        in_specs=in_specs + state_in_specs,
        out_specs=[y_spec] + state_specs,
        out_shape=[y_shape] + state_shapes,
        scratch_shapes=[pltpu.VMEM((nb, G, T + SUBLANES, CONV_CH), F32),
                        pltpu.VMEM((nb, G, T + SUBLANES, CONV_CH), F32)],
        input_output_aliases=aliases,
        compiler_params=pltpu.CompilerParams(
            dimension_semantics=("arbitrary", "arbitrary"), vmem_limit_bytes=VMEM_LIMIT),
        name="mixer_prompt" if carry else "mixer_sample",
    )(*([proj] * nb), cos_t, sin_t, lp["gbx"], lp["norms"], lp["gcw"], lp["scw"], lp["scb"],
      consts["mbd_b"], consts["mbd_f"], consts["m2_b"], consts["gexp"], consts["pm"],
      consts["lm"], consts["tril3"], ret_tabs[0], ret_tabs[1], *states)
    return res[0], res[1:]


def _ple(h3, p_ref, pn_ref, pg_ref, pp_ref):
    r = _rms(h3, pn_ref[...])
    gate = _sigmoid(jnp.dot(r.astype(BF16), pg_ref[...], preferred_element_type=F32))
    proj = jnp.dot(p_ref[...].astype(BF16), pp_ref[...], preferred_element_type=F32)
    return h3 + gate * proj


def _dense_kernel(h_ref, y_ref, wo_ref, nf_ref, wg_ref, wu_ref, wd_ref, p_ref, pn_ref, pg_ref,
                  pp_ref, o_ref):
    h2 = h_ref[...] + jnp.dot(y_ref[...], wo_ref[...], preferred_element_type=F32)
    c = _rms(h2, nf_ref[...]).astype(BF16)
    fw = FFN_DENSE // FFN_SPLIT
    h3 = h2
    for s in range(FFN_SPLIT):
        g = jnp.dot(c, wg_ref[:, fw * s:fw * (s + 1)], preferred_element_type=F32)
        u = jnp.dot(c, wu_ref[:, fw * s:fw * (s + 1)], preferred_element_type=F32)
        a = (_silu(g) * u).astype(BF16)
        h3 = h3 + jnp.dot(a, wd_ref[fw * s:fw * (s + 1), :], preferred_element_type=F32)
    o_ref[...] = _ple(h3, p_ref, pn_ref, pg_ref, pp_ref)


def _resident(shape):
    return pl.BlockSpec(shape, lambda i: (0,) * len(shape), pipeline_mode=pl.Buffered(1))


def _dense_call(h, y, p_all, layer, lw):
    t = h.shape[0]
    rows = lambda w: pl.BlockSpec((TM, w), lambda i: (i, 0))
    return pl.pallas_call(
        _dense_kernel,
        grid=(t // TM,),
        in_specs=[
            rows(D_MODEL), rows(D_MODEL),
            _resident((D_MODEL, D_MODEL)), _resident((1, D_MODEL)),
            _resident((D_MODEL, FFN_DENSE)), _resident((D_MODEL, FFN_DENSE)),
            _resident((FFN_DENSE, D_MODEL)),
            pl.BlockSpec((None, TM, PLE_DIM), lambda i: (layer, i, 0)),
            _resident((1, D_MODEL)), _resident((D_MODEL, D_MODEL)), _resident((PLE_DIM, D_MODEL)),
        ],
        out_specs=rows(D_MODEL),
        out_shape=jax.ShapeDtypeStruct((t, D_MODEL), F32),
        compiler_params=pltpu.CompilerParams(
            dimension_semantics=("arbitrary",), vmem_limit_bytes=VMEM_LIMIT),
        name="dense_ffn",
    )(h, y, lw["w_out"], lw["norm_ffn"], lw["wg"], lw["wu"], lw["wd"], p_all,
      lw["ple_norm"], lw["ple_w_gate"], lw["ple_w_proj"])


R_I0, R_I1, R_W0, R_W1 = 0, 1, 2, 3
C_COPIES = 2


def _moe_pre_kernel(h_ref, y_ref, wo_ref, nf_ref, rt_ref, h2_ref, c_ref, r_ref):
    h2 = h_ref[...] + jnp.dot(y_ref[...], wo_ref[...], preferred_element_type=F32)
    h2_ref[...] = h2
    c = _rms(h2, nf_ref[...])
    cb = c.astype(BF16)
    for k in range(C_COPIES):
        c_ref[k] = cb
    lane = lax.broadcasted_iota(jnp.int32, (TM, LANES), 1)
    logits = jnp.where(lane < N_EXPERTS, _dot_hi(c, rt_ref[...]), -jnp.inf)
    m1 = jnp.max(logits, axis=1, keepdims=True)
    i1 = jnp.min(jnp.where(logits == m1, lane, LANES), axis=1, keepdims=True)
    rest = jnp.where(lane == i1, -jnp.inf, logits)
    m2 = jnp.max(rest, axis=1, keepdims=True)
    i2 = jnp.min(jnp.where(rest == m2, lane, LANES), axis=1, keepdims=True)
    e2 = jnp.exp(m2 - m1)
    den = 1.0 + e2
    r_ref[...] = jnp.where(lane == R_I0, i1.astype(F32),
                           jnp.where(lane == R_I1, i2.astype(F32),
                                     jnp.where(lane == R_W0, 1.0 / den,
                                               jnp.where(lane == R_W1, e2 / den, 0.0))))


def _moe_pre_call(h, y, lw):
    t = h.shape[0]
    rows = lambda w: pl.BlockSpec((TM, w), lambda i: (i, 0))
    return pl.pallas_call(
        _moe_pre_kernel,
        grid=(t // TM,),
        in_specs=[rows(D_MODEL), rows(D_MODEL), _resident((D_MODEL, D_MODEL)),
                  _resident((1, D_MODEL)), _resident((D_MODEL, LANES))],
        out_specs=[rows(D_MODEL),
                   pl.BlockSpec((C_COPIES, TM, D_MODEL), lambda i: (0, i, 0)),
                   rows(LANES)],
        out_shape=[jax.ShapeDtypeStruct((t, D_MODEL), F32),
                   jax.ShapeDtypeStruct((C_COPIES, t, D_MODEL), BF16),
                   jax.ShapeDtypeStruct((t, LANES), F32)],
        compiler_params=pltpu.CompilerParams(
            dimension_semantics=("arbitrary",), vmem_limit_bytes=VMEM_LIMIT),
        name="moe_pre",
    )(h, y, lw["w_out"], lw["norm_ffn"], lw["router"])


def _moe_ffn_kernel(be_ref, nb_ref, x_ref, wg_ref, wu_ref, wd_ref, o_ref):
    del be_ref
    i = pl.program_id(0)

    @pl.when(i < nb_ref[0])
    def _compute():
        x = x_ref[...]
        fw = FFN_EXPERT // FFN_SPLIT
        acc = jnp.zeros((TMM, D_MODEL), F32)
        for s in range(FFN_SPLIT):
            g = jnp.dot(x, wg_ref[0, :, fw * s:fw * (s + 1)], preferred_element_type=F32)
            u = jnp.dot(x, wu_ref[0, :, fw * s:fw * (s + 1)], preferred_element_type=F32)
            a = (_silu(g) * u).astype(BF16)
            acc = acc + jnp.dot(a, wd_ref[0, fw * s:fw * (s + 1), :], preferred_element_type=F32)
        o_ref[...] = acc.astype(o_ref.dtype)

    @pl.when(i >= nb_ref[0])
    def _skip():
        o_ref[...] = jnp.zeros_like(o_ref)


def _moe_ffn_call(xs, blk_e, nblk, lw, layer):
    n_rows = xs.shape[0]
    wspec = lambda shape: pl.BlockSpec(shape, lambda i, be, nb: (layer, be[i], 0, 0),
                                       pipeline_mode=pl.Buffered(1))
    grid_spec = pltpu.PrefetchScalarGridSpec(
        num_scalar_prefetch=2,
        grid=(n_rows // TMM,),
        in_specs=[
            pl.BlockSpec((TMM, D_MODEL), lambda i, be, nb: (i, 0)),
            wspec((None, 1, D_MODEL, FFN_EXPERT)), wspec((None, 1, D_MODEL, FFN_EXPERT)),
            wspec((None, 1, FFN_EXPERT, D_MODEL)),
        ],
        out_specs=pl.BlockSpec((TMM, D_MODEL), lambda i, be, nb: (i, 0)),
    )
    return pl.pallas_call(
        _moe_ffn_kernel,
        grid_spec=grid_spec,
        out_shape=jax.ShapeDtypeStruct((n_rows, D_MODEL), BF16),
        compiler_params=pltpu.CompilerParams(
            dimension_semantics=("arbitrary",), vmem_limit_bytes=VMEM_LIMIT),
        name="moe_ffn",
    )(blk_e, nblk, xs, lw["wg"], lw["wu"], lw["wd"])


def _moe_post_kernel(final, h2_ref, y0_ref, y1_ref, r_ref, p_ref, pn_ref, pg_ref, pp_ref, nfin_ref,
                     o_ref):
    r = r_ref[...]
    w0 = r[:, R_W0:R_W0 + 1]
    w1 = r[:, R_W1:R_W1 + 1]
    h3 = h2_ref[...] + (w0 * y0_ref[...].astype(F32) + w1 * y1_ref[...].astype(F32))
    h4 = _ple(h3, p_ref, pn_ref, pg_ref, pp_ref)
    o_ref[...] = _rms(h4, nfin_ref[...]) if final else h4


def _moe_post_call(h2, y0, y1, route, p_all, layer, lw, norm_final, final):
    t = h2.shape[0]
    rows = lambda w: pl.BlockSpec((TM, w), lambda i: (i, 0))
    return pl.pallas_call(
        functools.partial(_moe_post_kernel, final),
        grid=(t // TM,),
        in_specs=[rows(D_MODEL), rows(D_MODEL), rows(D_MODEL), rows(LANES),
                  pl.BlockSpec((None, TM, PLE_DIM), lambda i: (layer, i, 0)),
                  _resident((1, D_MODEL)), _resident((D_MODEL, D_MODEL)),
                  _resident((PLE_DIM, D_MODEL)), _resident((1, D_MODEL))],
        out_specs=rows(D_MODEL),
        out_shape=jax.ShapeDtypeStruct((t, D_MODEL), F32),
        compiler_params=pltpu.CompilerParams(
            dimension_semantics=("arbitrary",), vmem_limit_bytes=VMEM_LIMIT),
        name="moe_post",
    )(h2, y0, y1, route, p_all, lw["ple_norm"], lw["ple_w_gate"], lw["ple_w_proj"], norm_final)


def _route_plan(route, n_rows):
    t = route.shape[0]
    e_flat = route[:, R_I0:R_I1 + 1].astype(jnp.int32).reshape(-1)
    onehot = (jnp.arange(N_EXPERTS, dtype=jnp.int32)[:, None] == e_flat[None, :]).astype(jnp.int32)
    csum = jnp.cumsum(onehot, axis=1)
    counts = csum[:, -1]
    pcounts = ((counts + TMM - 1) // TMM) * TMM
    ends = jnp.cumsum(pcounts)
    starts = ends - pcounts
    pos = jnp.sum(onehot * (csum - 1 + starts[:, None]), axis=0)
    src = (jnp.arange(n_rows, dtype=jnp.int32) % t).at[pos].set(
        jnp.arange(2 * t, dtype=jnp.int32) // 2, unique_indices=True)
    blk_start = jnp.arange(n_rows // TMM, dtype=jnp.int32) * TMM
    blk_e = jnp.minimum(jnp.sum((blk_start[:, None] >= ends[None, :]).astype(jnp.int32), axis=1),
                        N_EXPERTS - 1).astype(jnp.int32)
    nblk = (ends[-1:] // TMM).astype(jnp.int32)
    return src, pos.reshape(t, 2), blk_e, nblk


def _rope_tables(pos):
    half = HEAD_DIM // 2
    inv_freq = ROPE_BASE ** (-jnp.arange(half, dtype=F32) / half)
    ang = pos[:, None] * inv_freq[None, :]
    cos, sin = jnp.cos(ang), jnp.sin(ang)
    cos_h = jnp.concatenate([cos, cos], axis=-1)
    sin_h = jnp.concatenate([-sin, sin], axis=-1)
    return jnp.tile(cos_h, (1, N_HEADS)), jnp.tile(sin_h, (1, N_HEADS))


def _pad_lanes(x, width=LANES):
    return jnp.pad(x, [(0, 0)] * (x.ndim - 1) + [(0, width - x.shape[-1])])


def kernel(x_prompt, x_sample, state_ret, state_mlstm_c, state_mlstm_n, state_mlstm_m, state_gdn, state_gdn_conv, state_ssm, state_ssm_conv, p_prompt, p_sample, norm_mix, w_in, w_out, ret_norm, mlstm_i_bias, mlstm_f_bias, mlstm_norm, gdn_conv_w, gdn_a_log, gdn_dt_bias, gdn_norm, ssm_conv_w, ssm_conv_b, ssm_dt_bias, ssm_a_log, ssm_d, ssm_norm, norm_ffn, ffn_w_gate, ffn_w_up, ffn_w_down, moe_router, moe_w_gate, moe_w_up, moe_w_down, ple_w_proj, ple_norm, ple_w_gate, norm_final):
    bp, tp = x_prompt.shape[:2]
    bs, ts = x_sample.shape[:2]
    n_prompt = bp * tp
    n_all = n_prompt + bs * ts
    sample_seqs = MIX_ROWS // ts
    assert tp % MIX_ROWS == 0 and bp % PROMPT_SEQS == 0 and n_all % TM == 0
    assert MIX_ROWS % ts == 0 and bs % sample_seqs == 0 and ts >= CONV_W - 1 and DEPTH % 2 == 0
    hd = HEAD_DIM

    h = jnp.concatenate([x_prompt.reshape(n_prompt, D_MODEL), x_sample.reshape(bs * ts, D_MODEL)])
    h = h.astype(F32)
    p_all = jnp.concatenate([p_prompt.reshape(DEPTH, n_prompt, PLE_DIM),
                             p_sample.reshape(DEPTH, bs * ts, PLE_DIM)], axis=1)

    o_ml = 4 * GROUP_W
    o_gdn = o_ml + 4 * GROUP_W + 2 * N_HEADS
    o_ssm = o_gdn + 4 * GROUP_W + 2 * N_HEADS
    gate_cols = jnp.concatenate([
        w_in[:, :, o_ml + 4 * GROUP_W:o_gdn],
        w_in[:, :, o_gdn + 4 * GROUP_W:o_ssm],
        w_in[:, :, o_ssm + 4 * GROUP_W:],
    ], axis=-1)
    w_in_r = jnp.concatenate([
        w_in[:, :, 0:o_ml + 4 * GROUP_W],
        w_in[:, :, o_gdn:o_gdn + 4 * GROUP_W],
        w_in[:, :, o_ssm:o_ssm + 4 * GROUP_W],
        _pad_lanes(gate_cols),
    ], axis=-1)
    assert w_in_r.shape[-1] == D_PROJ
    col = jnp.arange(D_PROJ)
    k_cols = ((col >= C_RET + GROUP_W) & (col < C_RET + 2 * GROUP_W)) | (
        (col >= C_ML + GROUP_W) & (col < C_ML + 2 * GROUP_W))
    w_in_r = (w_in_r * jnp.where(k_cols, hd ** -0.5, 1.0).astype(F32)).astype(BF16)

    zeros4 = jnp.zeros((DEPTH, N_HEADS), F32)
    norms = jnp.stack([ret_norm, mlstm_norm, gdn_norm, ssm_norm], axis=1).astype(F32)
    gbx = jnp.repeat(jnp.stack([mlstm_i_bias, mlstm_f_bias, gdn_dt_bias, ssm_dt_bias, gdn_a_log,
                                ssm_a_log, ssm_d, zeros4], axis=1).astype(F32), hd, axis=-1)

    consts_p = _mixer_consts(1, MIX_ROWS)
    consts_s = _mixer_consts(sample_seqs, ts)
    ret_tabs_p = _ret_tables(1, MIX_ROWS)
    ret_tabs_s = _ret_tables(sample_seqs, ts)
    tab_prompt = _rope_tables(jnp.arange(tp, dtype=F32))
    tab_sample = tuple(jnp.tile(t, (sample_seqs, 1))
                       for t in _rope_tables(PAST_LEN + jnp.arange(ts, dtype=F32)))

    keep = SUBLANES - (CONV_W - 1)
    eye_h = jnp.eye(N_HEADS, dtype=F32)

    def conv_in(s):
        return jnp.pad(s.astype(F32), ((0, 0), (keep, 0), (0, 0)))

    big_in = dict(ret=state_ret.astype(F32), mc=state_mlstm_c.astype(F32),
                  gdn=state_gdn.astype(F32), ssm=state_ssm.astype(F32))
    big_states = {k: jnp.zeros(v.shape, F32) for k, v in big_in.items()}

    def sample_states_in(i):
        n_bd = (state_mlstm_n[i].astype(F32)[:, :, :, None, None]
                * eye_h[None, :, None, :, None])
        n_bd = jnp.broadcast_to(n_bd, (bs, N_HEADS, hd, N_HEADS, hd)).reshape(bs, GROUP_W, GROUP_W)
        m_x = jnp.repeat(state_mlstm_m[i].astype(F32), hd, axis=-1)[:, None, :]
        return (big_in["ret"], big_in["mc"], n_bd,
                jnp.broadcast_to(m_x, (bs, SUBLANES, GROUP_W)), big_in["gdn"],
                conv_in(state_gdn_conv[i]), big_in["ssm"], conv_in(state_ssm_conv[i]),
                big_states["ret"], big_states["mc"], big_states["gdn"], big_states["ssm"])

    def n_vec(s_mn):
        return jnp.stack([s_mn[:, hd * hh:hd * (hh + 1), hd * hh] for hh in range(N_HEADS)], axis=1)

    def sample_states_out(st):
        s_ret, s_mc, s_mn, s_mm, s_gdn, s_gconv, s_ssm, s_sconv = st
        big_states.update(ret=s_ret, mc=s_mc, gdn=s_gdn, ssm=s_ssm)
        return (n_vec(s_mn), s_mm[:, 0, ::hd], s_gconv[:, keep:], s_sconv[:, keep:])

    def prompt_states_out(st):
        s_ret, s_mc, s_mn, s_mm, s_gdn, s_gconv, s_ssm, s_sconv = st

        def diag(s):
            return jnp.stack([s[:, hd * hh:hd * (hh + 1), hd * hh:hd * (hh + 1)]
                              for hh in range(N_HEADS)], axis=1)

        per_group = N_HEADS // SSM_GROUPS
        ssm = jnp.stack([s_ssm[:, SSM_STATE * (hh // per_group):SSM_STATE * (hh // per_group + 1),
                               hd * hh:hd * (hh + 1)] for hh in range(N_HEADS)], axis=1)
        return (diag(s_ret), diag(s_mc), n_vec(s_mn), s_mm[:, 0, ::hd], diag(s_gdn),
                s_gconv[:, keep:], ssm, s_sconv[:, keep:])

    n_moe_rows = 2 * n_all + N_EXPERTS * TMM
    moe_wg, moe_wu, moe_wd = (w.astype(BF16) for w in (moe_w_gate, moe_w_up, moe_w_down))
    new_prompt, new_sample = [], []
    for i in range(DEPTH):
        lp = dict(norms=norms[i], gcw=gdn_conv_w[i].astype(F32), scw=ssm_conv_w[i].astype(F32),
                  scb=ssm_conv_b[i].astype(F32)[None, :], gbx=gbx[i])
        proj = _proj_call(h, norm_mix[i].astype(F32)[None, :], w_in_r[i])
        y_p, st_p = _mixer_call(proj, 0, bp, tp, PROMPT_SEQS, 1, True, tab_prompt, lp, consts_p,
                                ret_tabs_p)
        y_s, st_s = _mixer_call(proj, n_prompt, bs, ts, 1, sample_seqs, False, tab_sample, lp,
                                consts_s, ret_tabs_s, sample_states_in(i), layer=i)
        y_all = jnp.concatenate([y_p.reshape(n_prompt, D_MODEL), y_s.reshape(bs * ts, D_MODEL)])
        new_prompt.append(prompt_states_out(st_p))
        new_sample.append(sample_states_out(st_s))

        j = i // 2
        lw = dict(w_out=w_out[i].astype(BF16), norm_ffn=norm_ffn[i].astype(F32)[None, :],
                  ple_norm=ple_norm[i].astype(F32)[None, :], ple_w_gate=ple_w_gate[i].astype(BF16),
                  ple_w_proj=ple_w_proj[i].astype(BF16))
        if i % 2 == 0:
            lw.update(wg=ffn_w_gate[j].astype(BF16), wu=ffn_w_up[j].astype(BF16),
                      wd=ffn_w_down[j].astype(BF16))
            h = _dense_call(h, y_all, p_all, i, lw)
        else:
            lw.update(router=_pad_lanes(moe_router[j].astype(F32)), wg=moe_wg, wu=moe_wu, wd=moe_wd)
            h2, c, route = _moe_pre_call(h, y_all, lw)
            src, pos, blk_e, nblk = _route_plan(route, n_moe_rows)
            src = src + n_all * (jnp.arange(n_moe_rows, dtype=jnp.int32) % C_COPIES)
            xs = jnp.take(c.reshape(C_COPIES * n_all, D_MODEL), src, axis=0, mode="clip")
            ys = _moe_ffn_call(xs, blk_e, nblk, lw, j)
            y0 = jnp.take(ys, pos[:, 0], axis=0, mode="clip")
            y1 = jnp.take(ys, pos[:, 1], axis=0, mode="clip")
            h = _moe_post_call(h2, y0, y1, route, p_all, i, lw, norm_final.astype(F32)[None, :],
                               final=(i == DEPTH - 1))

    y = h.astype(x_prompt.dtype)
    y_prompt = y[:n_prompt].reshape(bp, tp, D_MODEL)
    y_sample = y[n_prompt:].reshape(bs, ts, D_MODEL)
    stack = lambda lst: tuple(jnp.stack([l[k] for l in lst]) for k in range(len(lst[0])))
    sa_n, sa_m, sa_gconv, sa_sconv = stack(new_sample)
    return (y_prompt, y_sample) + stack(new_prompt) + (
        big_states["ret"], big_states["mc"], sa_n, sa_m, big_states["gdn"], sa_gconv,
        big_states["ssm"], sa_sconv)
```

```python
import functools

import jax
import jax.numpy as jnp
from jax import lax
from jax.experimental import pallas as pl
from jax.experimental.pallas import tpu as pltpu

F32 = jnp.float32
BF16 = jnp.bfloat16
HIGHEST = lax.Precision.HIGHEST

D_MODEL = 1024
DEPTH = 4
PAST_LEN = 16384
N_HEADS = 4
HEAD_DIM = 64
GROUP_W = N_HEADS * HEAD_DIM
SSM_STATE = 128
SSM_GROUPS = 2
CONV_W = 4
CONV_CH = 3 * GROUP_W
ROPE_BASE = 10000.0
FFN_DENSE = 2816
N_EXPERTS = 8
FFN_EXPERT = 3584
PLE_DIM = 256
EPS = 1e-6

LANES = 128
SUBLANES = 8
VMEM_LIMIT = 56 * 1024 * 1024

C_RET = 0
C_ML = 1024
C_GDN = 2048
C_SSM = 3072
C_GATE = 4096
D_PROJ = C_GATE + LANES
X_MI, X_MF, X_GB, X_GA, X_DT = range(5)
N_GATES = 5

MIX_ROWS = 64
PROMPT_SEQS = 8
TM = 512
TMM = 512
FFN_SPLIT = 2


def _dot_split(a, b):
    a_hi = a.astype(BF16)
    a_lo = (a - a_hi.astype(F32)).astype(BF16)
    b_hi = b.astype(BF16)
    b_lo = (b - b_hi.astype(F32)).astype(BF16)
    return jnp.dot(jnp.concatenate([a_hi, a_lo, a_hi], axis=1),
                   jnp.concatenate([b_hi, b_hi, b_lo], axis=0), preferred_element_type=F32)


def _sigmoid(x):
    return 1.0 / (1.0 + jnp.exp(-x))


def _silu(x):
    return x * _sigmoid(x)


def _softplus(x):
    return jnp.maximum(x, 0.0) + jnp.log(1.0 + jnp.exp(-jnp.abs(x)))


def _rms(x, g):
    return x * lax.rsqrt(jnp.mean(x * x, -1, keepdims=True) + EPS) * g


def _proj_kernel(h_ref, g_ref, w_ref, o_ref):
    a = _rms(h_ref[...], g_ref[...])
    o_ref[...] = jnp.dot(a.astype(BF16), w_ref[...], preferred_element_type=F32)


def _proj_call(h, g, w):
    t = h.shape[0]
    return pl.pallas_call(
        _proj_kernel,
        grid=(t // TM,),
        in_specs=[
            pl.BlockSpec((TM, D_MODEL), lambda i: (i, 0)),
            pl.BlockSpec((1, D_MODEL), lambda i: (0, 0)),
            pl.BlockSpec((D_MODEL, D_PROJ), lambda i: (0, 0), pipeline_mode=pl.Buffered(1)),
        ],
        out_specs=pl.BlockSpec((TM, D_PROJ), lambda i: (i, 0)),
        out_shape=jax.ShapeDtypeStruct((t, D_PROJ), F32),
        compiler_params=pltpu.CompilerParams(
            dimension_semantics=("arbitrary",), vmem_limit_bytes=VMEM_LIMIT),
        name="proj",
    )(h, g, w)


def _split3(x):
    hi = x.astype(BF16)
    r1 = x - hi.astype(F32)
    mid = r1.astype(BF16)
    lo = (r1 - mid.astype(F32)).astype(BF16)
    return hi, mid, lo


def _dot_tn(a, b):
    return lax.dot_general(a, b, (((0,), (0,)), ((), ())), preferred_element_type=F32)


def _mixer_kernel(nb, G, T, carry, *refs):
    R = G * T
    hd = HEAD_DIM
    n_levels = T.bit_length() - 1
    per_group = N_HEADS // SSM_GROUPS
    proj_refs = refs[:nb]
    (cos_ref, sin_ref, gb_ref, norms_ref, gcw_ref, scw_ref, scb_ref, mbd_b_ref, mbd_f_ref,
     m2_b_ref, gexp_ref, pm_ref, lm_ref, tril3_ref, rt_ref, rs_ref) = refs[nb:nb + 16]
    rest = refs[nb + 16:]
    if carry:
        (y_ref, o_ret, o_mc, o_mn, o_mm, o_gdn, o_gconv, o_ssm, o_sconv, xg_ref, xs_ref) = rest
    else:
        (i_ret, i_mc, i_mn, i_mm, i_gdn, i_gconv, i_ssm, i_sconv) = rest[:8]
        (y_ref, o_ret, o_mc, o_mn, o_mm, o_gdn, o_gconv, o_ssm, o_sconv, xg_ref, xs_ref) = rest[12:]

    if carry:
        @pl.when(pl.program_id(1) == 0)
        def _init():
            for r in (o_ret, o_mc, o_mn, o_mm, o_gdn, o_ssm, xg_ref, xs_ref):
                r[...] = jnp.zeros_like(r)

    def to_bd(s4):
        rows = []
        for h in range(N_HEADS):
            parts = [jnp.zeros((hd, hd * h), F32)] if h else []
            parts.append(s4[h])
            if h < N_HEADS - 1:
                parts.append(jnp.zeros((hd, hd * (N_HEADS - 1 - h)), F32))
            rows.append(jnp.concatenate(parts, axis=1))
        return jnp.concatenate(rows, axis=0)

    def to_st(s4):
        rows = []
        for g in range(SSM_GROUPS):
            parts = [jnp.zeros((SSM_STATE, SSM_STATE * g), F32)] if g else []
            parts += [s4[per_group * g + k] for k in range(per_group)]
            if g < SSM_GROUPS - 1:
                parts.append(jnp.zeros((SSM_STATE, SSM_STATE * (SSM_GROUPS - 1 - g)), F32))
            rows.append(jnp.concatenate(parts, axis=1))
        return jnp.concatenate(rows, axis=0)

    out_refs = dict(ret=o_ret, mc=o_mc, mn=o_mn, gdn=o_gdn, ssm=o_ssm)

    def live_blocks(kind):
        if kind == "ssm":
            return [(SSM_STATE * g, SSM_STATE, SSM_STATE * g, SSM_STATE) for g in range(SSM_GROUPS)]
        return [(hd * h, hd, hd * h, hd) for h in range(N_HEADS)]

    if carry:
        def get(kind, j, i):
            return out_refs[kind][j]

        def update(kind, j, i, dec, dlt):
            ref = out_refs[kind]
            for r0, nr, c0, nc in live_blocks(kind):
                ref[j, r0:r0 + nr, c0:c0 + nc] = (ref[j, r0:r0 + nr, c0:c0 + nc] * dec[:, c0:c0 + nc]
                                                  + dlt[r0:r0 + nr, c0:c0 + nc])

        def get_m(j, i):
            return o_mm[j, 0:1, :]

        def put_m(j, i, row):
            o_mm[j] = jnp.broadcast_to(row, (SUBLANES, GROUP_W))
    else:
        in_refs = dict(ret=i_ret, mc=i_mc, gdn=i_gdn, ssm=i_ssm)

        def get(kind, j, i):
            if kind == "mn":
                return i_mn[i]
            if kind == "ssm":
                return to_st(i_ssm[i])
            return to_bd(in_refs[kind][i])

        def update(kind, j, i, dec, dlt):
            if kind == "mn":
                o_mn[i] = i_mn[i] * dec + mbd_f * dlt
                return
            for h in range(N_HEADS):
                r0, nr = (SSM_STATE * (h // per_group), SSM_STATE) if kind == "ssm" else (hd * h, hd)
                out_refs[kind][i, h] = (in_refs[kind][i, h] * dec[:, hd * h:hd * (h + 1)]
                                        + dlt[r0:r0 + nr, hd * h:hd * (h + 1)])

        def get_m(j, i):
            return i_mm[i, 0:1, :]

        def put_m(j, i, row):
            o_mm[i] = jnp.broadcast_to(row, (SUBLANES, GROUP_W))

    def seq_rows(x, i):
        return x if G == 1 else x[T * i:T * (i + 1)]

    def seq_row(x, i):
        return x if x.shape[0] == 1 else x[T * i:T * i + 1]

    def seg_last(x):
        if G == 1:
            return x[R - 1:R]
        last = x.reshape(G, T, x.shape[1])[:, T - 1:T, :]
        return jnp.broadcast_to(last, (G, T, x.shape[1])).reshape(R, x.shape[1])

    def seg_max0(x):
        if G == 1:
            return jnp.max(x, axis=0, keepdims=True)
        mx = jnp.max(x.reshape(G, T, x.shape[1]), axis=1, keepdims=True)
        return jnp.broadcast_to(mx, (G, T, x.shape[1])).reshape(R, x.shape[1])

    def m_rows(j):
        if G == 1:
            return get_m(j, 0)
        return jnp.concatenate([jnp.broadcast_to(get_m(j, i), (T, GROUP_W)) for i in range(G)],
                               axis=0)

    mbd_b = mbd_b_ref[...]
    mbd_f = mbd_f_ref[...]
    m2_b = m2_b_ref[...]
    eye_p = pm_ref[0]
    ut_p = pm_ref[1]
    ninf_p = pm_ref[2]
    tril3 = tril3_ref[...]
    gb = gb_ref[...]
    norms = norms_ref[...]
    cosv = cos_ref[...]
    sinv = sin_ref[...]
    lane_w = lax.broadcasted_iota(jnp.int32, (R, GROUP_W), 1)
    first_half = (lane_w % hd) < (hd // 2)
    ones_b = jnp.ones((R, GROUP_W), BF16)
    neg_a_gdn = -jnp.exp(gb[4:5, :])
    neg_a_ssm = -jnp.exp(gb[5:6, :])
    inv_hd = 1.0 / hd

    def bf(x):
        return x.astype(BF16)

    def bd(xb, mask):
        return jnp.concatenate([xb] * (GROUP_W // R), axis=0) * mask

    def mm(a, b):
        return jnp.dot(a, b, preferred_element_type=F32)

    def mm_nt(a, b):
        return lax.dot_general(a, b, (((1,), (1,)), ((), ())), preferred_element_type=F32)

    def mm_state(a, states):
        outs = [mm(seq_rows(a, i), bf(states[i])) for i in range(G)]
        return outs[0] if G == 1 else jnp.concatenate(outs, axis=0)

    def delta(a, b, i):
        return _dot_tn(seq_rows(a, i), seq_rows(b, i))

    def cumsum(g):
        return mm(tril3, jnp.concatenate(_split3(g), axis=0))

    gate_cache = {}

    def gate(j, kind):
        if j not in gate_cache:
            slab = proj_refs[j][:, C_GATE:C_GATE + LANES]
            gate_cache[j] = mm(jnp.concatenate(_split3(slab), axis=1), gexp_ref[...])
        return gate_cache[j][:, GROUP_W * kind:GROUP_W * (kind + 1)]

    def sum0(x):
        return jnp.sum(x, axis=0, keepdims=True)

    def segmax(x):
        parts = []
        for h in range(N_HEADS):
            mh = jnp.max(x[:, hd * h:hd * (h + 1)], axis=1, keepdims=True)
            parts.append(jnp.broadcast_to(mh, (R, hd)))
        return jnp.concatenate(parts, axis=1)

    def rope(x):
        swapped = jnp.where(first_half, pltpu.roll(x, GROUP_W - hd // 2, 1),
                            pltpu.roll(x, hd // 2, 1))
        return x * cosv + swapped * sinv

    def conv4(x_ref, j, x, w, hist_in, hist_out):
        if not carry:
            x_ref[j, :, 0:SUBLANES, :] = hist_in[...]
        x_ref[j, :, SUBLANES:SUBLANES + T, :] = x.reshape(G, T, CONV_CH)
        acc = w[0:1, :] * x_ref[j, :, pl.ds(SUBLANES - 3, T), :]
        for k in range(1, CONV_W):
            acc = acc + w[k:k + 1, :] * x_ref[j, :, pl.ds(SUBLANES - 3 + k, T), :]
        hist = x_ref[j, :, T:T + SUBLANES, :]
        if carry:
            hist_out[j] = hist[0]
            x_ref[j, :, 0:SUBLANES, :] = hist
        else:
            hist_out[...] = hist
        return acc.reshape(R, CONV_CH)


    def ret_chain(j):
        P = proj_refs[j]
        rq = rope(P[:, C_RET:C_RET + GROUP_W])
        rk = rope(P[:, C_RET + GROUP_W:C_RET + 2 * GROUP_W])
        rvb = bf(P[:, C_RET + 2 * GROUP_W:C_RET + 3 * GROUP_W])
        S = [get("ret", j, i) for i in range(G)]
        scores = mm_nt(bf(rq), bd(bf(rk), mbd_b)) * rt_ref[0]
        o_inter = mm_state(bf(rq * rt_ref[1]), S)
        yield
        ke = bf(rk * rt_ref[2])
        for i in range(G):
            update("ret", j, i, rs_ref[0:1, :], delta(ke, rvb, i))
        o = mm(bf(scores), bd(rvb, mbd_b)) + o_inter
        yield
        mean = mm(bf(o), mbd_b) * inv_hd
        yield
        xc = o - mean
        var = mm(bf(xc * xc), mbd_b) * inv_hd
        yield
        ro = xc * lax.rsqrt(var + EPS) * norms[0:1, :]
        rg = P[:, C_RET + 3 * GROUP_W:C_RET + 4 * GROUP_W]
        y_ref[j, :, 0:GROUP_W] = (_silu(rg) * ro).astype(y_ref.dtype)

    def mlstm_chain(j):
        P = proj_refs[j]
        mq = bf(P[:, C_ML:C_ML + GROUP_W])
        mk = P[:, C_ML + GROUP_W:C_ML + 2 * GROUP_W]
        mvb = bf(P[:, C_ML + 2 * GROUP_W:C_ML + 3 * GROUP_W])
        i_x = gate(j, X_MI) + gb[0:1, :]
        g_f = -_softplus(-(gate(j, X_MF) + gb[1:2, :]))
        b_col = cumsum(g_f)
        qk = mm_nt(mq, bd(bf(mk), mbd_b))
        C = [get("mc", j, i) for i in range(G)]
        N = [get("mn", j, i) for i in range(G)]
        q_c = mm_state(mq, C)
        q_n = mm_state(mq, N)
        yield
        b_row = sum0(g_f * ut_p)
        i_row = sum0(i_x * eye_p)
        b_last = seg_last(b_col)
        m_prev = m_rows(j)
        dd = b_col - b_row + i_row
        inter = b_col + m_prev
        m_row = jnp.maximum(segmax(dd + ninf_p), inter)
        w_intra = jnp.exp(jnp.minimum(dd - m_row, ninf_p))
        w_state = jnp.exp(inter - m_row)
        sb = bf(qk * w_intra)
        num = mm(sb, bd(mvb, mbd_b)) + w_state * q_c
        den = mm(sb, mbd_b) + w_state * q_n
        g_end = b_last - b_col + i_x
        m_new = jnp.maximum(b_last + m_prev, seg_max0(g_end))
        w_c = jnp.exp(b_last + m_prev - m_new)
        kwb = bf(mk * jnp.exp(g_end - m_new))
        for i in range(G):
            wc_i = seq_row(w_c, i)
            update("mc", j, i, wc_i, delta(kwb, mvb, i))
            update("mn", j, i, wc_i, delta(kwb, ones_b, i))
            put_m(j, i, seq_row(m_new, i))
        yield
        hh = num / jnp.maximum(jnp.abs(den), jnp.exp(-m_row))
        mean = mm(bf(hh), mbd_b) * inv_hd
        yield
        xc = hh - mean
        var = mm(bf(xc * xc), mbd_b) * inv_hd
        yield
        mh = xc * lax.rsqrt(var + EPS) * norms[1:2, :]
        mo = P[:, C_ML + 3 * GROUP_W:C_ML + 4 * GROUP_W]
        y_ref[j, :, GROUP_W:2 * GROUP_W] = (_sigmoid(mo) * mh).astype(y_ref.dtype)

    def gdn_chain(j):
        P = proj_refs[j]
        conv = conv4(xg_ref, j, P[:, C_GDN:C_GDN + CONV_CH], gcw_ref[...],
                     None if carry else i_gconv, o_gconv)
        act = _silu(conv)
        gq, gk, gv = act[:, 0:GROUP_W], act[:, GROUP_W:2 * GROUP_W], act[:, 2 * GROUP_W:]
        q_ss = mm(bf(gq * gq), mbd_b)
        k_ss = mm(bf(gk * gk), mbd_b)
        g_g = neg_a_gdn * _softplus(gate(j, X_GA) + gb[2:3, :])
        b_col = cumsum(g_g)
        yield
        qn = gq * lax.rsqrt(q_ss + EPS) * hd ** -0.5
        kn = gk * lax.rsqrt(k_ss + EPS)
        beta = _sigmoid(gate(j, X_GB))
        b_row = sum0(g_g * ut_p)
        b_last = seg_last(b_col)
        e_col = jnp.exp(b_col)
        decay = jnp.exp(jnp.minimum(b_col - b_row, ninf_p))
        kb = kn * beta
        bd_k = bd(bf(kn), mbd_b)
        kk = mm_nt(bf(kb), bd_k)
        qk = mm_nt(bf(qn), bd_k)
        S = [get("gdn", j, i) for i in range(G)]
        q_s = mm_state(bf(qn), S)
        yield
        a_low = kk * decay
        attn = qk * decay
        t_inv = eye_p - a_low * lm_ref[0]
        for lev in range(1, n_levels):
            m1 = mm(bf(a_low * lm_ref[lev]), bd(bf(t_inv), mbd_b))
            yield
            t_inv = t_inv - mm(bf(t_inv), bd(bf(m1), mbd_b))
            yield
        tb = bf(t_inv)
        u = mm(tb, bd(bf(gv * beta), mbd_b))
        w = mm(tb, bd(bf(kb * e_col), mbd_b))
        yield
        v_new = u - mm_state(bf(w), S)
        yield
        vnb = bf(v_new)
        o = e_col * q_s + mm(bf(attn), bd(vnb, mbd_b))
        keb = bf(kn * jnp.exp(b_last - b_col))
        s_dec = jnp.exp(b_last)
        for i in range(G):
            update("gdn", j, i, seq_row(s_dec, i), delta(keb, vnb, i))
        yield
        o_ss = mm(bf(o * o), mbd_b)
        yield
        go = o * lax.rsqrt(o_ss * inv_hd + EPS) * norms[2:3, :]
        gz = P[:, C_GDN + CONV_CH:C_GDN + CONV_CH + GROUP_W]
        y_ref[j, :, 2 * GROUP_W:3 * GROUP_W] = (go * _silu(gz)).astype(y_ref.dtype)

    def ssd_chain(j):
        P = proj_refs[j]
        conv = conv4(xs_ref, j, P[:, C_SSM + GROUP_W:C_SSM + GROUP_W + CONV_CH], scw_ref[...],
                     None if carry else i_sconv, o_sconv) + scb_ref[...]
        act = _silu(conv)
        sx = act[:, 0:GROUP_W]
        s_b = bf(act[:, GROUP_W:2 * GROUP_W])
        s_c = bf(act[:, 2 * GROUP_W:])
        dt = _softplus(gate(j, X_DT) + gb[3:4, :])
        g_s = dt * neg_a_ssm
        b_col = cumsum(g_s)
        cb = mm_nt(s_c, bd(s_b, m2_b))
        S = [get("ssm", j, i) for i in range(G)]
        c_s = mm_state(s_c, S)
        yield
        b_row = sum0(g_s * ut_p)
        b_last = seg_last(b_col)
        decay = jnp.exp(jnp.minimum(b_col - b_row, ninf_p))
        v = sx * dt
        o = mm(bf(cb * decay), bd(bf(v), mbd_b)) + jnp.exp(b_col) * c_s
        v_end = bf(v * jnp.exp(b_last - b_col))
        s_dec = jnp.exp(b_last)
        for i in range(G):
            update("ssm", j, i, seq_row(s_dec, i), delta(s_b, v_end, i))
        yield
        sz = P[:, C_SSM:C_SSM + GROUP_W]
        yv = (o + gb[6:7, :] * sx) * _silu(sz)
        ms = mm(bf(yv * yv), m2_b) * (1.0 / (GROUP_W // SSM_GROUPS))
        yield
        y_ref[j, :, 3 * GROUP_W:] = (yv * lax.rsqrt(ms + EPS) * norms[3:4, :]).astype(y_ref.dtype)

    chains = [chain(j) for chain in (gdn_chain, mlstm_chain, ret_chain, ssd_chain)
              for j in range(nb)]
    while chains:
        alive = []
        for ch in chains:
            try:
                next(ch)
                alive.append(ch)
            except StopIteration:
                pass
        chains = alive


def _mixer_consts(G, T):
    R = G * T
    n_levels = T.bit_length() - 1
    r = jnp.arange(GROUP_W)
    mbd = (r[:, None] // HEAD_DIM == r[None, :] // HEAD_DIM)
    half = GROUP_W // SSM_GROUPS
    m2 = (r[:, None] // half == r[None, :] // half)
    row = jnp.arange(R)[:, None]
    col = (r % R)[None, :]
    same = (row // T) == (col // T)
    tl, ts = row % T, col % T
    causal = same & (tl >= ts)
    pm = jnp.stack([(row == col).astype(F32), (same & (tl <= ts)).astype(F32),
                    jnp.where(causal, 0.0, -jnp.inf).astype(F32)])
    lm = jnp.stack([same & ((tl >> (j + 1)) == (ts >> (j + 1))) & (((tl >> j) & 1) == 1)
                    & (((ts >> j) & 1) == 0) for j in range(n_levels)]).astype(F32)
    tril = causal[:, :R].astype(BF16)
    src_lane = jnp.arange(LANES)[:, None]
    dst = jnp.arange(N_GATES * GROUP_W)[None, :]
    gexp = (src_lane == N_HEADS * (dst // GROUP_W) + (dst % GROUP_W) // HEAD_DIM).astype(BF16)
    return dict(mbd_b=mbd.astype(BF16), mbd_f=mbd.astype(F32), m2_b=m2.astype(BF16),
                gexp=jnp.concatenate([gexp] * 3, axis=0), pm=pm, lm=lm,
                tril3=jnp.concatenate([tril] * 3, axis=1))


def _ret_tables(G, T):
    R = G * T
    log_gamma = jnp.log1p(-jnp.exp2(-5.0 - jnp.arange(N_HEADS, dtype=F32)))
    b = jnp.cumsum(jnp.broadcast_to(log_gamma[:, None], (N_HEADS, T)), axis=1)
    b_r = jnp.tile(b, (1, G))
    pos = jnp.arange(R)
    causal = ((pos[:, None] // T) == (pos[None, :] // T)) & ((pos[:, None] % T) >= (pos[None, :] % T))
    diff = b_r[:, :, None] - b_r[:, None, :]
    dec = jnp.exp(jnp.where(causal[None], diff, -jnp.inf))
    dec_p = dec.transpose(1, 0, 2).reshape(R, GROUP_W)
    expand = lambda x: jnp.repeat(x.T, HEAD_DIM, axis=1)
    q_scale = expand(jnp.exp(b_r))
    k_scale = expand(jnp.exp(b[:, -1:] - b_r))
    s_dec = jnp.repeat(jnp.exp(b[:, -1]), HEAD_DIM)[None, :]
    return (jnp.stack([dec_p, q_scale, k_scale]),
            jnp.broadcast_to(s_dec, (SUBLANES, GROUP_W)))


def _mixer_call(proj, row0, n_seq, seq_len, nb, G, carry, tables, lp, consts, ret_tabs,
                states=(), layer=0):
    R = MIX_ROWS
    T = R // G
    n_chunks = seq_len // T
    n_levels = T.bit_length() - 1
    blk0 = row0 // R
    cos_t, sin_t = tables
    const2 = lambda g, c: (0, 0)
    const3 = lambda g, c: (0, 0, 0)
    st3 = lambda g, c: (g, 0, 0)
    st4 = lambda g, c: (g, 0, 0, 0)
    if carry:
        proj_specs = [pl.BlockSpec((R, D_PROJ), functools.partial(
            lambda g, c, j: (blk0 + (g * nb + j) * n_chunks + c, 0), j=j)) for j in range(nb)]
        tab_map = lambda g, c: (c, 0)
        n_steps = n_seq // nb
    else:
        assert nb == 1 and n_chunks == 1
        proj_specs = [pl.BlockSpec((R, D_PROJ), lambda g, c: (blk0 + g, 0))]
        tab_map = const2
        n_steps = n_seq // G
    sq = pl.BlockSpec((GROUP_W, GROUP_W), const2)
    in_specs = proj_specs + [
        pl.BlockSpec((R, GROUP_W), tab_map),
        pl.BlockSpec((R, GROUP_W), tab_map),
        pl.BlockSpec((SUBLANES, GROUP_W), const2),
        pl.BlockSpec((N_HEADS, GROUP_W), const2),
        pl.BlockSpec((CONV_W, CONV_CH), const2),
        pl.BlockSpec((CONV_W, CONV_CH), const2),
        pl.BlockSpec((1, CONV_CH), const2),
        sq, sq, sq,
        pl.BlockSpec((3 * LANES, N_GATES * GROUP_W), const2),
        pl.BlockSpec((3, R, GROUP_W), const3),
        pl.BlockSpec((n_levels, R, GROUP_W), const3),
        pl.BlockSpec((R, 3 * R), const2),
        pl.BlockSpec((3, R, GROUP_W), const3),
        pl.BlockSpec((SUBLANES, GROUP_W), const2),
    ]
    ns = nb if carry else G
    big = pl.BlockSpec((ns, GROUP_W, GROUP_W), st3)
    hist = pl.BlockSpec((ns, SUBLANES, CONV_CH), st3)
    mrow = pl.BlockSpec((ns, SUBLANES, GROUP_W), st3)
    sq_shape = jax.ShapeDtypeStruct((n_seq, GROUP_W, GROUP_W), F32)
    hist_shape = jax.ShapeDtypeStruct((n_seq, SUBLANES, CONV_CH), F32)
    m_shape = jax.ShapeDtypeStruct((n_seq, SUBLANES, GROUP_W), F32)
    if carry:
        state_specs = [big, big, big, mrow, big, hist, big, hist]
        state_shapes = [sq_shape, sq_shape, sq_shape, m_shape, sq_shape, hist_shape, sq_shape,
                        hist_shape]
        y_spec = pl.BlockSpec((nb, R, D_MODEL), lambda g, c: (g, c, 0))
        y_shape = jax.ShapeDtypeStruct((n_seq, seq_len, D_MODEL), BF16)
        state_in_specs = []
        aliases = {}
    else:
        st5 = lambda g, c: (layer, g, 0, 0, 0)
        nat = pl.BlockSpec((None, G, N_HEADS, HEAD_DIM, HEAD_DIM), st5)
        nat_ssm = pl.BlockSpec((None, G, N_HEADS, SSM_STATE, HEAD_DIM), st5)
        nat_shape = jax.ShapeDtypeStruct((DEPTH, n_seq, N_HEADS, HEAD_DIM, HEAD_DIM), F32)
        ssm_shape = jax.ShapeDtypeStruct((DEPTH, n_seq, N_HEADS, SSM_STATE, HEAD_DIM), F32)
        state_specs = [nat, nat, big, mrow, nat, hist, nat_ssm, hist]
        state_shapes = [nat_shape, nat_shape, sq_shape, m_shape, nat_shape, hist_shape, ssm_shape,
                        hist_shape]
        y_spec = pl.BlockSpec((1, R, D_MODEL), lambda g, c: (g, 0, 0))
        y_shape = jax.ShapeDtypeStruct((n_seq // G, R, D_MODEL), BF16)
        state_in_specs = state_specs + [pl.BlockSpec(memory_space=pl.ANY)] * 4
        n_fixed = len(in_specs) + len(state_specs)
        aliases = {n_fixed + q: 1 + k for q, k in enumerate((0, 1, 4, 6))}
    res = pl.pallas_call(
        functools.partial(_mixer_kernel, nb, G, T, carry),
        grid=(n_steps, n_chunks),
        in_specs=in_specs + state_in_specs,
        out_specs=[y_spec] + state_specs,
        out_shape=[y_shape] + state_shapes,
        scratch_shapes=[pltpu.VMEM((nb, G, T + SUBLANES, CONV_CH), F32),
                        pltpu.VMEM((nb, G, T + SUBLANES, CONV_CH), F32)],
        input_output_aliases=aliases,
        compiler_params=pltpu.CompilerParams(
            dimension_semantics=("arbitrary", "arbitrary"), vmem_limit_bytes=VMEM_LIMIT),
        name="mixer_prompt" if carry else "mixer_sample",
    )(*([proj] * nb), cos_t, sin_t, lp["gbx"], lp["norms"], lp["gcw"], lp["scw"], lp["scb"],
      consts["mbd_b"], consts["mbd_f"], consts["m2_b"], consts["gexp"], consts["pm"],
      consts["lm"], consts["tril3"], ret_tabs[0], ret_tabs[1], *states)
    return res[0], res[1:]


def _ple(h3, p_ref, pn_ref, pg_ref, pp_ref):
    r = _rms(h3, pn_ref[...])
    gate = _sigmoid(jnp.dot(r.astype(BF16), pg_ref[...], preferred_element_type=F32))
    proj = jnp.dot(p_ref[...].astype(BF16), pp_ref[...], preferred_element_type=F32)
    return h3 + gate * proj


def _dense_kernel(h_ref, y_ref, wo_ref, nf_ref, wg_ref, wu_ref, wd_ref, p_ref, pn_ref, pg_ref,
                  pp_ref, o_ref):
    h2 = h_ref[...] + jnp.dot(y_ref[...], wo_ref[...], preferred_element_type=F32)
    c = _rms(h2, nf_ref[...]).astype(BF16)
    fw = FFN_DENSE // FFN_SPLIT
    h3 = h2
    for s in range(FFN_SPLIT):
        g = jnp.dot(c, wg_ref[:, fw * s:fw * (s + 1)], preferred_element_type=F32)
        u = jnp.dot(c, wu_ref[:, fw * s:fw * (s + 1)], preferred_element_type=F32)
        a = (_silu(g) * u).astype(BF16)
        h3 = h3 + jnp.dot(a, wd_ref[fw * s:fw * (s + 1), :], preferred_element_type=F32)
    o_ref[...] = _ple(h3, p_ref, pn_ref, pg_ref, pp_ref)


def _resident(shape):
    return pl.BlockSpec(shape, lambda i: (0,) * len(shape), pipeline_mode=pl.Buffered(1))


def _dense_call(h, y, p_all, layer, lw):
    t = h.shape[0]
    rows = lambda w: pl.BlockSpec((TM, w), lambda i: (i, 0))
    return pl.pallas_call(
        _dense_kernel,
        grid=(t // TM,),
        in_specs=[
            rows(D_MODEL), rows(D_MODEL),
            _resident((D_MODEL, D_MODEL)), _resident((1, D_MODEL)),
            _resident((D_MODEL, FFN_DENSE)), _resident((D_MODEL, FFN_DENSE)),
            _resident((FFN_DENSE, D_MODEL)),
            pl.BlockSpec((None, TM, PLE_DIM), lambda i: (layer, i, 0)),
            _resident((1, D_MODEL)), _resident((D_MODEL, D_MODEL)), _resident((PLE_DIM, D_MODEL)),
        ],
        out_specs=rows(D_MODEL),
        out_shape=jax.ShapeDtypeStruct((t, D_MODEL), F32),
        compiler_params=pltpu.CompilerParams(
            dimension_semantics=("arbitrary",), vmem_limit_bytes=VMEM_LIMIT),
        name="dense_ffn",
    )(h, y, lw["w_out"], lw["norm_ffn"], lw["wg"], lw["wu"], lw["wd"], p_all,
      lw["ple_norm"], lw["ple_w_gate"], lw["ple_w_proj"])


R_I0, R_I1, R_W0, R_W1 = 0, 1, 2, 3
C_COPIES = 2


def _moe_pre_kernel(h_ref, y_ref, wo_ref, nf_ref, rt_ref, h2_ref, c_ref, r_ref):
    h2 = h_ref[...] + jnp.dot(y_ref[...], wo_ref[...], preferred_element_type=F32)
    h2_ref[...] = h2
    c = _rms(h2, nf_ref[...])
    cb = c.astype(BF16)
    for k in range(C_COPIES):
        c_ref[k] = cb
    lane = lax.broadcasted_iota(jnp.int32, (TM, LANES), 1)
    logits = jnp.where(lane < N_EXPERTS, _dot_split(c, rt_ref[...]), -jnp.inf)
    m1 = jnp.max(logits, axis=1, keepdims=True)
    i1 = jnp.min(jnp.where(logits == m1, lane, LANES), axis=1, keepdims=True)
    rest = jnp.where(lane == i1, -jnp.inf, logits)
    m2 = jnp.max(rest, axis=1, keepdims=True)
    i2 = jnp.min(jnp.where(rest == m2, lane, LANES), axis=1, keepdims=True)
    e2 = jnp.exp(m2 - m1)
    den = 1.0 + e2
    r_ref[...] = jnp.where(lane == R_I0, i1.astype(F32),
                           jnp.where(lane == R_I1, i2.astype(F32),
                                     jnp.where(lane == R_W0, 1.0 / den,
                                               jnp.where(lane == R_W1, e2 / den, 0.0))))


def _moe_pre_call(h, y, lw):
    t = h.shape[0]
    rows = lambda w: pl.BlockSpec((TM, w), lambda i: (i, 0))
    return pl.pallas_call(
        _moe_pre_kernel,
        grid=(t // TM,),
        in_specs=[rows(D_MODEL), rows(D_MODEL), _resident((D_MODEL, D_MODEL)),
                  _resident((1, D_MODEL)), _resident((D_MODEL, LANES))],
        out_specs=[rows(D_MODEL),
                   pl.BlockSpec((C_COPIES, TM, D_MODEL), lambda i: (0, i, 0)),
                   rows(LANES)],
        out_shape=[jax.ShapeDtypeStruct((t, D_MODEL), F32),
                   jax.ShapeDtypeStruct((C_COPIES, t, D_MODEL), BF16),
                   jax.ShapeDtypeStruct((t, LANES), F32)],
        compiler_params=pltpu.CompilerParams(
            dimension_semantics=("arbitrary",), vmem_limit_bytes=VMEM_LIMIT),
        name="moe_pre",
    )(h, y, lw["w_out"], lw["norm_ffn"], lw["router"])


def _moe_ffn_kernel(be_ref, nb_ref, x_ref, wg_ref, wu_ref, wd_ref, o_ref):
    del be_ref
    i = pl.program_id(0)

    @pl.when(i < nb_ref[0])
    def _compute():
        x = x_ref[...]
        fw = FFN_EXPERT // FFN_SPLIT
        acc = jnp.zeros((TMM, D_MODEL), F32)
        for s in range(FFN_SPLIT):
            g = jnp.dot(x, wg_ref[0, :, fw * s:fw * (s + 1)], preferred_element_type=F32)
            u = jnp.dot(x, wu_ref[0, :, fw * s:fw * (s + 1)], preferred_element_type=F32)
            a = (_silu(g) * u).astype(BF16)
            acc = acc + jnp.dot(a, wd_ref[0, fw * s:fw * (s + 1), :], preferred_element_type=F32)
        o_ref[...] = acc.astype(o_ref.dtype)

    @pl.when(i >= nb_ref[0])
    def _skip():
        o_ref[...] = jnp.zeros_like(o_ref)


def _moe_ffn_call(xs, blk_e, nblk, lw, layer):
    n_rows = xs.shape[0]
    wspec = lambda shape: pl.BlockSpec(shape, lambda i, be, nb: (layer, be[i], 0, 0),
                                       pipeline_mode=pl.Buffered(1))
    grid_spec = pltpu.PrefetchScalarGridSpec(
        num_scalar_prefetch=2,
        grid=(n_rows // TMM,),
        in_specs=[
            pl.BlockSpec((TMM, D_MODEL), lambda i, be, nb: (i, 0)),
            wspec((None, 1, D_MODEL, FFN_EXPERT)), wspec((None, 1, D_MODEL, FFN_EXPERT)),
            wspec((None, 1, FFN_EXPERT, D_MODEL)),
        ],
        out_specs=pl.BlockSpec((TMM, D_MODEL), lambda i, be, nb: (i, 0)),
    )
    return pl.pallas_call(
        _moe_ffn_kernel,
        grid_spec=grid_spec,
        out_shape=jax.ShapeDtypeStruct((n_rows, D_MODEL), BF16),
        compiler_params=pltpu.CompilerParams(
            dimension_semantics=("arbitrary",), vmem_limit_bytes=VMEM_LIMIT),
        name="moe_ffn",
    )(blk_e, nblk, xs, lw["wg"], lw["wu"], lw["wd"])


def _moe_post_kernel(final, h2_ref, y0_ref, y1_ref, r_ref, p_ref, pn_ref, pg_ref, pp_ref, nfin_ref,
                     o_ref):
    r = r_ref[...]
    w0 = r[:, R_W0:R_W0 + 1]
    w1 = r[:, R_W1:R_W1 + 1]
    h3 = h2_ref[...] + (w0 * y0_ref[...].astype(F32) + w1 * y1_ref[...].astype(F32))
    h4 = _ple(h3, p_ref, pn_ref, pg_ref, pp_ref)
    o_ref[...] = _rms(h4, nfin_ref[...]) if final else h4


def _moe_post_call(h2, y0, y1, route, p_all, layer, lw, norm_final, final):
    t = h2.shape[0]
    rows = lambda w: pl.BlockSpec((TM, w), lambda i: (i, 0))
    return pl.pallas_call(
        functools.partial(_moe_post_kernel, final),
        grid=(t // TM,),
        in_specs=[rows(D_MODEL), rows(D_MODEL), rows(D_MODEL), rows(LANES),
                  pl.BlockSpec((None, TM, PLE_DIM), lambda i: (layer, i, 0)),
                  _resident((1, D_MODEL)), _resident((D_MODEL, D_MODEL)),
                  _resident((PLE_DIM, D_MODEL)), _resident((1, D_MODEL))],
        out_specs=rows(D_MODEL),
        out_shape=jax.ShapeDtypeStruct((t, D_MODEL), F32),
        compiler_params=pltpu.CompilerParams(
            dimension_semantics=("arbitrary",), vmem_limit_bytes=VMEM_LIMIT),
        name="moe_post",
    )(h2, y0, y1, route, p_all, lw["ple_norm"], lw["ple_w_gate"], lw["ple_w_proj"], norm_final)


def _route_plan(route, n_rows):
    t = route.shape[0]
    e_flat = route[:, R_I0:R_I1 + 1].astype(jnp.int32).reshape(-1)
    onehot = (jnp.arange(N_EXPERTS, dtype=jnp.int32)[:, None] == e_flat[None, :]).astype(jnp.int32)
    csum = jnp.cumsum(onehot, axis=1)
    counts = csum[:, -1]
    pcounts = ((counts + TMM - 1) // TMM) * TMM
    ends = jnp.cumsum(pcounts)
    starts = ends - pcounts
    pos = jnp.sum(onehot * (csum - 1 + starts[:, None]), axis=0)
    src = (jnp.arange(n_rows, dtype=jnp.int32) % t).at[pos].set(
        jnp.arange(2 * t, dtype=jnp.int32) // 2, unique_indices=True)
    blk_start = jnp.arange(n_rows // TMM, dtype=jnp.int32) * TMM
    blk_e = jnp.minimum(jnp.sum((blk_start[:, None] >= ends[None, :]).astype(jnp.int32), axis=1),
                        N_EXPERTS - 1).astype(jnp.int32)
    nblk = (ends[-1:] // TMM).astype(jnp.int32)
    return src, pos.reshape(t, 2), blk_e, nblk


def _rope_tables(pos):
    half = HEAD_DIM // 2
    inv_freq = ROPE_BASE ** (-jnp.arange(half, dtype=F32) / half)
    ang = pos[:, None] * inv_freq[None, :]
    cos, sin = jnp.cos(ang), jnp.sin(ang)
    cos_h = jnp.concatenate([cos, cos], axis=-1)
    sin_h = jnp.concatenate([-sin, sin], axis=-1)
    return jnp.tile(cos_h, (1, N_HEADS)), jnp.tile(sin_h, (1, N_HEADS))


def _pad_lanes(x, width=LANES):
    return jnp.pad(x, [(0, 0)] * (x.ndim - 1) + [(0, width - x.shape[-1])])


def kernel(x_prompt, x_sample, state_ret, state_mlstm_c, state_mlstm_n, state_mlstm_m, state_gdn, state_gdn_conv, state_ssm, state_ssm_conv, p_prompt, p_sample, norm_mix, w_in, w_out, ret_norm, mlstm_i_bias, mlstm_f_bias, mlstm_norm, gdn_conv_w, gdn_a_log, gdn_dt_bias, gdn_norm, ssm_conv_w, ssm_conv_b, ssm_dt_bias, ssm_a_log, ssm_d, ssm_norm, norm_ffn, ffn_w_gate, ffn_w_up, ffn_w_down, moe_router, moe_w_gate, moe_w_up, moe_w_down, ple_w_proj, ple_norm, ple_w_gate, norm_final):
    bp, tp = x_prompt.shape[:2]
    bs, ts = x_sample.shape[:2]
    n_prompt = bp * tp
    n_all = n_prompt + bs * ts
    sample_seqs = MIX_ROWS // ts
    assert tp % MIX_ROWS == 0 and bp % PROMPT_SEQS == 0 and n_all % TM == 0
    assert MIX_ROWS % ts == 0 and bs % sample_seqs == 0 and ts >= CONV_W - 1 and DEPTH % 2 == 0
    hd = HEAD_DIM

    h = jnp.concatenate([x_prompt.reshape(n_prompt, D_MODEL), x_sample.reshape(bs * ts, D_MODEL)])
    h = h.astype(F32)
    p_all = jnp.concatenate([p_prompt.reshape(DEPTH, n_prompt, PLE_DIM),
                             p_sample.reshape(DEPTH, bs * ts, PLE_DIM)], axis=1)

    o_ml = 4 * GROUP_W
    o_gdn = o_ml + 4 * GROUP_W + 2 * N_HEADS
    o_ssm = o_gdn + 4 * GROUP_W + 2 * N_HEADS
    gate_cols = jnp.concatenate([
        w_in[:, :, o_ml + 4 * GROUP_W:o_gdn],
        w_in[:, :, o_gdn + 4 * GROUP_W:o_ssm],
        w_in[:, :, o_ssm + 4 * GROUP_W:],
    ], axis=-1)
    w_in_r = jnp.concatenate([
        w_in[:, :, 0:o_ml + 4 * GROUP_W],
        w_in[:, :, o_gdn:o_gdn + 4 * GROUP_W],
        w_in[:, :, o_ssm:o_ssm + 4 * GROUP_W],
        _pad_lanes(gate_cols),
    ], axis=-1)
    assert w_in_r.shape[-1] == D_PROJ
    col = jnp.arange(D_PROJ)
    k_cols = ((col >= C_RET + GROUP_W) & (col < C_RET + 2 * GROUP_W)) | (
        (col >= C_ML + GROUP_W) & (col < C_ML + 2 * GROUP_W))
    w_in_r = (w_in_r * jnp.where(k_cols, hd ** -0.5, 1.0).astype(F32)).astype(BF16)

    zeros4 = jnp.zeros((DEPTH, N_HEADS), F32)
    norms = jnp.stack([ret_norm, mlstm_norm, gdn_norm, ssm_norm], axis=1).astype(F32)
    gbx = jnp.repeat(jnp.stack([mlstm_i_bias, mlstm_f_bias, gdn_dt_bias, ssm_dt_bias, gdn_a_log,
                                ssm_a_log, ssm_d, zeros4], axis=1).astype(F32), hd, axis=-1)

    consts_p = _mixer_consts(1, MIX_ROWS)
    consts_s = _mixer_consts(sample_seqs, ts)
    ret_tabs_p = _ret_tables(1, MIX_ROWS)
    ret_tabs_s = _ret_tables(sample_seqs, ts)
    tab_prompt = _rope_tables(jnp.arange(tp, dtype=F32))
    tab_sample = tuple(jnp.tile(t, (sample_seqs, 1))
                       for t in _rope_tables(PAST_LEN + jnp.arange(ts, dtype=F32)))

    keep = SUBLANES - (CONV_W - 1)
    eye_h = jnp.eye(N_HEADS, dtype=F32)

    def conv_in(s):
        return jnp.pad(s.astype(F32), ((0, 0), (keep, 0), (0, 0)))

    big_in = dict(ret=state_ret.astype(F32), mc=state_mlstm_c.astype(F32),
                  gdn=state_gdn.astype(F32), ssm=state_ssm.astype(F32))
    big_states = {k: jnp.zeros(v.shape, F32) for k, v in big_in.items()}

    def sample_states_in(i):
        n_bd = (state_mlstm_n[i].astype(F32)[:, :, :, None, None]
                * eye_h[None, :, None, :, None])
        n_bd = jnp.broadcast_to(n_bd, (bs, N_HEADS, hd, N_HEADS, hd)).reshape(bs, GROUP_W, GROUP_W)
        m_x = jnp.repeat(state_mlstm_m[i].astype(F32), hd, axis=-1)[:, None, :]
        return (big_in["ret"], big_in["mc"], n_bd,
                jnp.broadcast_to(m_x, (bs, SUBLANES, GROUP_W)), big_in["gdn"],
                conv_in(state_gdn_conv[i]), big_in["ssm"], conv_in(state_ssm_conv[i]),
                big_states["ret"], big_states["mc"], big_states["gdn"], big_states["ssm"])

    def n_vec(s_mn):
        return jnp.stack([s_mn[:, hd * hh:hd * (hh + 1), hd * hh] for hh in range(N_HEADS)], axis=1)

    def sample_states_out(st):
        s_ret, s_mc, s_mn, s_mm, s_gdn, s_gconv, s_ssm, s_sconv = st
        big_states.update(ret=s_ret, mc=s_mc, gdn=s_gdn, ssm=s_ssm)
        return (n_vec(s_mn), s_mm[:, 0, ::hd], s_gconv[:, keep:], s_sconv[:, keep:])

    def prompt_states_out(st):
        s_ret, s_mc, s_mn, s_mm, s_gdn, s_gconv, s_ssm, s_sconv = st

        def diag(s):
            return jnp.stack([s[:, hd * hh:hd * (hh + 1), hd * hh:hd * (hh + 1)]
                              for hh in range(N_HEADS)], axis=1)

        per_group = N_HEADS // SSM_GROUPS
        ssm = jnp.stack([s_ssm[:, SSM_STATE * (hh // per_group):SSM_STATE * (hh // per_group + 1),
                               hd * hh:hd * (hh + 1)] for hh in range(N_HEADS)], axis=1)
        return (diag(s_ret), diag(s_mc), n_vec(s_mn), s_mm[:, 0, ::hd], diag(s_gdn),
                s_gconv[:, keep:], ssm, s_sconv[:, keep:])

    n_moe_rows = 2 * n_all + N_EXPERTS * TMM
    moe_wg, moe_wu, moe_wd = (w.astype(BF16) for w in (moe_w_gate, moe_w_up, moe_w_down))
    new_prompt, new_sample = [], []
    for i in range(DEPTH):
        lp = dict(norms=norms[i], gcw=gdn_conv_w[i].astype(F32), scw=ssm_conv_w[i].astype(F32),
                  scb=ssm_conv_b[i].astype(F32)[None, :], gbx=gbx[i])
        proj = _proj_call(h, norm_mix[i].astype(F32)[None, :], w_in_r[i])
        y_p, st_p = _mixer_call(proj, 0, bp, tp, PROMPT_SEQS, 1, True, tab_prompt, lp, consts_p,
                                ret_tabs_p)
        y_s, st_s = _mixer_call(proj, n_prompt, bs, ts, 1, sample_seqs, False, tab_sample, lp,
                                consts_s, ret_tabs_s, sample_states_in(i), layer=i)
        y_all = jnp.concatenate([y_p.reshape(n_prompt, D_MODEL), y_s.reshape(bs * ts, D_MODEL)])
        new_prompt.append(prompt_states_out(st_p))
        new_sample.append(sample_states_out(st_s))

        j = i // 2
        lw = dict(w_out=w_out[i].astype(BF16), norm_ffn=norm_ffn[i].astype(F32)[None, :],
                  ple_norm=ple_norm[i].astype(F32)[None, :], ple_w_gate=ple_w_gate[i].astype(BF16),
                  ple_w_proj=ple_w_proj[i].astype(BF16))
        if i % 2 == 0:
            lw.update(wg=ffn_w_gate[j].astype(BF16), wu=ffn_w_up[j].astype(BF16),
                      wd=ffn_w_down[j].astype(BF16))
            h = _dense_call(h, y_all, p_all, i, lw)
        else:
            lw.update(router=_pad_lanes(moe_router[j].astype(F32)), wg=moe_wg, wu=moe_wu, wd=moe_wd)
            h2, c, route = _moe_pre_call(h, y_all, lw)
            src, pos, blk_e, nblk = _route_plan(route, n_moe_rows)
            src = src + n_all * (jnp.arange(n_moe_rows, dtype=jnp.int32) % C_COPIES)
            xs = jnp.take(c.reshape(C_COPIES * n_all, D_MODEL), src, axis=0, mode="clip")
            ys = _moe_ffn_call(xs, blk_e, nblk, lw, j)
            y0 = jnp.take(ys, pos[:, 0], axis=0, mode="clip")
            y1 = jnp.take(ys, pos[:, 1], axis=0, mode="clip")
            h = _moe_post_call(h2, y0, y1, route, p_all, i, lw, norm_final.astype(F32)[None, :],
                               final=(i == DEPTH - 1))

    y = h.astype(x_prompt.dtype)
    y_prompt = y[:n_prompt].reshape(bp, tp, D_MODEL)
    y_sample = y[n_prompt:].reshape(bs, ts, D_MODEL)
    stack = lambda lst: tuple(jnp.stack([l[k] for l in lst]) for k in range(len(lst[0])))
    sa_n, sa_m, sa_gconv, sa_sconv = stack(new_sample)
    return (y_prompt, y_sample) + stack(new_prompt) + (
        big_states["ret"], big_states["mc"], sa_n, sa_m, big_states["gdn"], sa_gconv,
        big_states["ssm"], sa_sconv)
```
